```python
import math
import jax
import jax.numpy as jnp
from jax import lax
import numpy as np

D_MODEL = 1024
BATCH = 8
SEQ = 2048
DEPTH = 2
DEC_BATCH = 128
DEC_SEQ = 8
PAST_LEN = 8192
PAGE_SIZE = 128

GLA_H = 4
GLA_DK = D_MODEL // 8
GLA_DV = D_MODEL // 4
GLA_QK = GLA_H * GLA_DK
GLA_V = GLA_H * GLA_DV
GLA_RANK = 16
GLA_TAU = 16.0
GLA_CHUNK = 32
SWA_HD = 64
SWA_H = D_MODEL // SWA_HD
SWA_KVH = 2
SWA_G = SWA_H // SWA_KVH
SWA_W = SWA_H * SWA_HD
SWA_KV = SWA_KVH * SWA_HD
WINDOW = 128
MEM_LEN = 256
X_H = 4
X_HD = D_MODEL // X_H
X_W = X_H * X_HD
N_BRANCH = 3
IN_SPLITS = (GLA_QK, GLA_QK, GLA_V, GLA_V, GLA_RANK, SWA_W, SWA_KV, SWA_KV, SWA_W, X_W, X_W, N_BRANCH * D_MODEL)
IN_COLS = sum(IN_SPLITS)
EPS = 1e-6

kernel_name = 'gla_swa_sink_memory_hybrid_step'


def _rmsnorm(x, g):
    xf = x.astype(jnp.float32)
    y = xf * lax.rsqrt(jnp.mean(xf * xf, axis=-1, keepdims=True) + EPS)
    return (y * g.astype(jnp.float32)).astype(x.dtype)


def _split_in(z):
    cuts = [int(c) for c in np.cumsum(IN_SPLITS)[:-1]]
    return jnp.split(z, cuts, axis=-1)


def _gla_chunked(q, k, v, g, s0):
    B, L, H, _ = q.shape
    c = min(GLA_CHUNK, L)
    n = -(-L // c)
    pad = n * c - L

    def blocks(a):
        a = jnp.pad(a.astype(jnp.float32), ((0, 0), (0, pad), (0, 0), (0, 0)))
        return a.reshape(B, n, c, H, a.shape[-1]).transpose(1, 0, 3, 2, 4)

    q, k, v, g = blocks(q), blocks(k), blocks(v), blocks(g)
    b = jnp.cumsum(g, axis=3)
    b_last = b[:, :, :, -1:, :]
    qb = q * jnp.exp(b)
    kb = k * jnp.exp(-b)
    kd = k * jnp.exp(b_last - b)
    decay = jnp.exp(b_last[:, :, :, 0, :])
    mask = jnp.tril(jnp.ones((c, c), dtype=bool))

    def step(S, inp):
        qb_c, kb_c, kd_c, v_c, dec_c = inp
        a = jnp.where(mask, jnp.einsum('bhcd,bhsd->bhcs', qb_c, kb_c), 0.0)
        o = jnp.einsum('bhcs,bhsv->bhcv', a, v_c) + jnp.einsum('bhcd,bhdv->bhcv', qb_c, S)
        S = dec_c[..., None] * S + jnp.einsum('bhsd,bhsv->bhdv', kd_c, v_c)
        return S, o

    S, o = lax.scan(step, s0.astype(jnp.float32), (qb, kb, kd, v, decay))
    o = o.transpose(1, 0, 3, 2, 4).reshape(B, n * c, H, -1)[:, :L]
    return o, S


def _sink_softmax(s, sink, mask):
    s = jnp.where(mask, s, -jnp.inf)
    m = jnp.maximum(jnp.max(s, axis=-1, keepdims=True), sink)
    p = jnp.exp(s - m)
    return p / (jnp.sum(p, axis=-1, keepdims=True) + jnp.exp(sink - m))


def _swa_prompt(q, k, v, sink):
    B, S = q.shape[:2]
    nb = S // WINDOW
    qb = q.reshape(B, nb, WINDOW, SWA_KVH, SWA_G, SWA_HD)

    def band(a):
        a = a.reshape(B, nb, WINDOW, SWA_KVH, SWA_HD)
        a = jnp.pad(a, ((0, 0), (1, 0), (0, 0), (0, 0), (0, 0)))
        return jnp.concatenate([a[:, :-1], a[:, 1:]], axis=2)

    kb, vb = band(k), band(v)
    s = jnp.einsum('bnikgd,bnjkd->bnkgij', qb, kb).astype(jnp.float32) * (SWA_HD ** -0.5)
    i = jnp.arange(WINDOW)[:, None]
    j = jnp.arange(2 * WINDOW)[None, :]
    diff = i + WINDOW - j
    blk = jnp.arange(nb)[:, None, None]
    mask = (diff >= 0) & (diff <= WINDOW) & ((blk > 0) | (j >= WINDOW))
    sk = sink.astype(jnp.float32).reshape(SWA_KVH, SWA_G, 1, 1)
    p = _sink_softmax(s, sk, mask[None, :, None, None])
    o = jnp.einsum('bnkgij,bnjkd->bnikgd', p.astype(vb.dtype), vb)
    return o.reshape(B, S, SWA_W)


def _swa_sample(q, k, v, k_buf, v_buf, sink):
    B, L = q.shape[:2]
    Wb = k_buf.shape[1]
    k_all = jnp.concatenate([k_buf.astype(k.dtype), k], axis=1)
    v_all = jnp.concatenate([v_buf.astype(v.dtype), v], axis=1)
    qg = q.reshape(B, L, SWA_KVH, SWA_G, SWA_HD)
    s = jnp.einsum('bikgd,bjkd->bkgij', qg, k_all).astype(jnp.float32) * (SWA_HD ** -0.5)
    diff = jnp.arange(L)[:, None] + Wb - jnp.arange(Wb + L)[None, :]
    mask = (diff >= 0) & (diff <= WINDOW)
    sk = sink.astype(jnp.float32).reshape(SWA_KVH, SWA_G, 1, 1)
    p = _sink_softmax(s, sk, mask)
    o = jnp.einsum('bkgij,bjkd->bikgd', p.astype(v_all.dtype), v_all).reshape(B, L, SWA_W)
    return o, k_all[:, -Wb:], v_all[:, -Wb:]


def _mem_kv(mem, g_mem, w_mem_kv):
    B, M, _ = mem.shape
    kv = _rmsnorm(mem, g_mem) @ w_mem_kv
    k, v = jnp.split(kv, 2, axis=-1)
    return k.reshape(B, M, X_H, X_HD), v.reshape(B, M, X_H, X_HD)


def _mem_attend(q, mk, mv):
    s = jnp.einsum('blhd,bmhd->bhlm', q, mk.astype(q.dtype)).astype(jnp.float32) * (X_HD ** -0.5)
    p = jax.nn.softmax(s, axis=-1)
    return jnp.einsum('bhlm,bmhd->blhd', p.astype(q.dtype), mv.astype(q.dtype))


def _layer(x, s0, k_buf, v_buf, mem_k, mem_v, g_norm, w_in, w_gla_a, b_gla_a, g_gla_out,
           swa_sink, w_br_gla, w_br_swa, w_br_mem, w_out):
    B, L, _ = x.shape
    h = _rmsnorm(x, g_norm)
    gq, gk, gv, gz, ga, sq, sk, sv, sz, mq, mz, mg = _split_in(h @ w_in)
    log_a = jax.nn.log_sigmoid((ga @ w_gla_a + b_gla_a).astype(jnp.float32)) / GLA_TAU
    o_gla, s_new = _gla_chunked(gq.reshape(B, L, GLA_H, GLA_DK) * (GLA_DK ** -0.5),
                                gk.reshape(B, L, GLA_H, GLA_DK),
                                gv.reshape(B, L, GLA_H, GLA_DV),
                                log_a.reshape(B, L, GLA_H, GLA_DK), s0)
    o_gla = _rmsnorm(o_gla, g_gla_out).reshape(B, L, GLA_V).astype(x.dtype)
    br_gla = (o_gla * jax.nn.silu(gz)) @ w_br_gla
    k_s = sk.reshape(B, L, SWA_KVH, SWA_HD)
    v_s = sv.reshape(B, L, SWA_KVH, SWA_HD)
    if k_buf is None:
        o_swa = _swa_prompt(sq, k_s, v_s, swa_sink)
        k_new, v_new = k_s[:, -WINDOW:], v_s[:, -WINDOW:]
    else:
        o_swa, k_new, v_new = _swa_sample(sq, k_s, v_s, k_buf, v_buf, swa_sink)
    br_swa = (o_swa * jax.nn.silu(sz)) @ w_br_swa
    o_mem = _mem_attend(mq.reshape(B, L, X_H, X_HD), mem_k, mem_v).reshape(B, L, X_W)
    br_mem = (o_mem * jax.nn.silu(mz)) @ w_br_mem
    m_gla, m_swa, m_mem = jnp.split(jax.nn.sigmoid(mg), N_BRANCH, axis=-1)
    merged = m_gla * br_gla + m_swa * br_swa + m_mem * br_mem
    return x + merged @ w_out, s_new, k_new, v_new


def setup_inputs(seed: int = 0) -> dict:
    key = jax.random.key(seed)
    ks = jax.random.split(key, 21)
    f32 = jnp.float32

    def nrm(k, shape, scale):
        return scale * jax.random.normal(k, shape, f32)

    win_buf = min(WINDOW, PAST_LEN)
    return {
        'x_prompt': nrm(ks[0], (BATCH, SEQ, D_MODEL), 1.0),
        'x_sample': nrm(ks[1], (DEC_BATCH, DEC_SEQ, D_MODEL), 1.0),
        'mem_prompt': nrm(ks[2], (BATCH, MEM_LEN, D_MODEL), 1.0),
        'state_gla': nrm(ks[3], (DEPTH, DEC_BATCH, GLA_H, GLA_DK, GLA_DV), 1.0),
        'cache_swa_k': nrm(ks[4], (DEPTH, DEC_BATCH, win_buf, SWA_KVH, SWA_HD), 1.0),
        'cache_swa_v': nrm(ks[5], (DEPTH, DEC_BATCH, win_buf, SWA_KVH, SWA_HD), 1.0),
        'cache_mem_k': nrm(ks[6], (DEPTH, DEC_BATCH, MEM_LEN, X_H, X_HD), 1.0),
        'cache_mem_v': nrm(ks[7], (DEPTH, DEC_BATCH, MEM_LEN, X_H, X_HD), 1.0),
        'g_norm': 1.0 + nrm(ks[8], (DEPTH, D_MODEL), 0.02),
        'w_in': nrm(ks[9], (DEPTH, D_MODEL, IN_COLS), D_MODEL ** -0.5),
        'w_gla_a': nrm(ks[10], (DEPTH, GLA_RANK, GLA_QK), GLA_RANK ** -0.5),
        'b_gla_a': nrm(ks[11], (DEPTH, GLA_QK), 0.1),
        'g_gla_out': 1.0 + nrm(ks[12], (DEPTH, GLA_DV), 0.02),
        'swa_sink': nrm(ks[13], (DEPTH, SWA_H), 0.5),
        'g_mem': 1.0 + nrm(ks[14], (DEPTH, D_MODEL), 0.02),
        'w_mem_kv': nrm(ks[15], (DEPTH, D_MODEL, 2 * X_W), D_MODEL ** -0.5),
        'w_br_gla': nrm(ks[16], (DEPTH, GLA_V, D_MODEL), GLA_V ** -0.5),
        'w_br_swa': nrm(ks[17], (DEPTH, SWA_W, D_MODEL), SWA_W ** -0.5),
        'w_br_mem': nrm(ks[18], (DEPTH, X_W, D_MODEL), X_W ** -0.5),
        'w_out': nrm(ks[19], (DEPTH, D_MODEL, D_MODEL), D_MODEL ** -0.5),
        'g_final': 1.0 + nrm(ks[20], (D_MODEL,), 0.02),
    }


def reference(x_prompt, x_sample, mem_prompt, state_gla, cache_swa_k, cache_swa_v, cache_mem_k,
              cache_mem_v, g_norm, w_in, w_gla_a, b_gla_a, g_gla_out, swa_sink, g_mem, w_mem_kv,
              w_br_gla, w_br_swa, w_br_mem, w_out, g_final):
    yp, ys = x_prompt, x_sample
    gla_p, kp_l, vp_l, mk_l, mv_l = [], [], [], [], []
    gla_s, ks_l, vs_l = [], [], []
    for l in range(DEPTH):
        lw = (g_norm[l], w_in[l], w_gla_a[l], b_gla_a[l], g_gla_out[l], swa_sink[l],
              w_br_gla[l], w_br_swa[l], w_br_mem[l], w_out[l])
        mk, mv = _mem_kv(mem_prompt, g_mem[l], w_mem_kv[l])
        s0 = jnp.zeros((yp.shape[0], GLA_H, GLA_DK, GLA_DV), jnp.float32)
        yp, sp, kp, vp = _layer(yp, s0, None, None, mk, mv, *lw)
        ys, ss, kss, vss = _layer(ys, state_gla[l], cache_swa_k[l], cache_swa_v[l],
                                  cache_mem_k[l], cache_mem_v[l], *lw)
        gla_p.append(sp)
        kp_l.append(kp)
        vp_l.append(vp)
        mk_l.append(mk)
        mv_l.append(mv)
        gla_s.append(ss)
        ks_l.append(kss)
        vs_l.append(vss)
    yp = _rmsnorm(yp, g_final)
    ys = _rmsnorm(ys, g_final)
    return (yp, ys, jnp.stack(gla_p), jnp.stack(kp_l), jnp.stack(vp_l), jnp.stack(mk_l),
            jnp.stack(mv_l), jnp.stack(gla_s), jnp.stack(ks_l), jnp.stack(vs_l))
```

```python
import functools

import jax
import jax.numpy as jnp
from jax import lax
from jax.experimental import pallas as pl
from jax.experimental.pallas import tpu as pltpu

F32 = jnp.float32
BF16 = jnp.bfloat16

D_MODEL = 1024
GLA_H = 4
GLA_DK = 128
GLA_DV = 256
GLA_QK = GLA_H * GLA_DK
GLA_V = GLA_H * GLA_DV
GLA_RANK = 16
GLA_TAU = 16.0
GLA_CHUNK = 32
SWA_HD = 64
SWA_H = 16
SWA_KVH = 2
SWA_G = SWA_H // SWA_KVH
SWA_W = SWA_H * SWA_HD
SWA_KV = SWA_KVH * SWA_HD
WINDOW = 128
MEM_LEN = 256
X_H = 4
X_HD = 256
X_W = X_H * X_HD
N_BRANCH = 3
IN_SPLITS = (GLA_QK, GLA_QK, GLA_V, GLA_V, GLA_RANK, SWA_W, SWA_KV, SWA_KV, SWA_W, X_W, X_W,
             N_BRANCH * D_MODEL)
EPS = 1e-6

LANES = 128
BF16_ROWS = 16
VMEM_LIMIT = 56 * 1024 * 1024

Z_GQ = 0
Z_GK = Z_GQ + GLA_QK
Z_GV = Z_GK + GLA_QK
Z_GZ = Z_GV + GLA_V
Z_SQ = Z_GZ + GLA_V
Z_SZ = Z_SQ + SWA_W
Z_MQ = Z_SZ + SWA_W
Z_MZ = Z_MQ + X_W
Z_MG = Z_MZ + X_W
Z_GA = Z_MG + N_BRANCH * D_MODEL
Z_COLS = Z_GA + LANES
KV_COLS = 2 * SWA_KV


def _params(*semantics):
    return pltpu.CompilerParams(dimension_semantics=semantics, vmem_limit_bytes=VMEM_LIMIT)


def _sigmoid(x):
    return 1.0 / (1.0 + jnp.exp(-x))


def _log_sigmoid(x):
    return jnp.minimum(x, 0.0) - jnp.log(1.0 + jnp.exp(-jnp.abs(x)))


def _rmsnorm_rows(x, g):
    return x * lax.rsqrt(jnp.mean(x * x, axis=-1, keepdims=True) + EPS) * g


def _dot(a, b):
    return jnp.dot(a, b, preferred_element_type=F32)


def _dot_nt(a, b):
    return lax.dot_general(a, b, (((1,), (1,)), ((), ())), preferred_element_type=F32)


def _dot_tn(a, b):
    return lax.dot_general(a, b, (((0,), (0,)), ((), ())), preferred_element_type=F32)


def _norm_matmul_kernel(x_ref, g_ref, w_ref, *out_refs, plan):
    h = _rmsnorm_rows(x_ref[...], g_ref[...]).astype(BF16)
    for out_idx, out_c0, w_c0, width in plan:
        o_ref = out_refs[out_idx]
        o_ref[:, out_c0:out_c0 + width] = _dot(h, w_ref[:, w_c0:w_c0 + width]).astype(o_ref.dtype)


def _norm_matmul(x2d, g, w, outs, plan, tm, name):
    t, d = x2d.shape
    n = w.shape[1]
    return pl.pallas_call(
        functools.partial(_norm_matmul_kernel, plan=tuple(plan)),
        grid=(t // tm,),
        in_specs=[
            pl.BlockSpec((tm, d), lambda i: (i, 0)),
            pl.BlockSpec((1, d), lambda i: (0, 0)),
            pl.BlockSpec((d, n), lambda i: (0, 0), pipeline_mode=pl.Buffered(1)),
        ],
        out_specs=[pl.BlockSpec((tm, c), lambda i: (i, 0)) for c, _ in outs],
        out_shape=[jax.ShapeDtypeStruct((t, c), dt) for c, dt in outs],
        compiler_params=_params("parallel"),
        name=name,
    )(x2d, g, w)


def _chunk_plan(out_idx, out_c0, w_c0, width, chunk):
    plan = []
    off = 0
    while off < width:
        step = min(chunk, width - off)
        plan.append((out_idx, out_c0 + off, w_c0 + off, step))
        off += step
    return plan


def _in_proj(x2d, g, w):
    plan = _chunk_plan(0, 0, 0, Z_COLS, 512) + _chunk_plan(1, 0, Z_COLS, KV_COLS, KV_COLS)
    return _norm_matmul(x2d, g, w, [(Z_COLS, BF16), (KV_COLS, F32)], plan, tm=256, name="in_proj")


def _mem_kv(mem2d, g, w):
    plan = _chunk_plan(0, 0, 0, X_W, 512) + _chunk_plan(1, 0, X_W, X_W, 512)
    return _norm_matmul(mem2d, g, w, [(X_W, F32), (X_W, F32)], plan, tm=256, name="mem_kv")


def _chunk_cumsum(g, chunk):
    pos = lax.broadcasted_iota(jnp.int32, g.shape, 0) & (chunk - 1)
    b = g
    shift = 1
    while shift < chunk:
        b = b + jnp.where(pos >= shift, pltpu.roll(b, shift, 0), 0.0)
        shift *= 2
    return b


def _pad_rows(x, rows):
    if x.shape[0] >= rows:
        return x
    return jnp.concatenate([x, jnp.zeros((rows - x.shape[0], x.shape[1]), x.dtype)], axis=0)


def _gla_kernel(q_ref, k_ref, v_ref, gz_ref, ga_ref, wa_ref, ba_ref, gout_ref, *rest, chunk,
                nchunks, carry):
    if carry:
        o_ref, s_ref = rest
    else:
        s0_ref, o_ref, s_ref = rest
    rows = chunk * nchunks
    mm_rows = max(chunk, BF16_ROWS)

    x = _dot(ga_ref[...], wa_ref[...]) + ba_ref[...]
    g = _log_sigmoid(x) * (1.0 / GLA_TAU)
    b = _chunk_cumsum(g, chunk)
    b_last = [b[j * chunk + chunk - 1:(j + 1) * chunk, :] for j in range(nchunks)]
    b_last_full = jnp.concatenate([jnp.broadcast_to(r, (chunk, GLA_DK)) for r in b_last], axis=0)

    q = q_ref[...].astype(F32) * (GLA_DK ** -0.5)
    k = k_ref[...].astype(F32)
    qb = q * jnp.exp(b)
    kb = k * jnp.exp(-b)
    kd = k * jnp.exp(b_last_full - b)

    a = _dot_nt(qb.astype(BF16), kb.astype(BF16))
    r_idx = lax.broadcasted_iota(jnp.int32, (rows, rows), 0)
    c_idx = lax.broadcasted_iota(jnp.int32, (rows, rows), 1)
    diff = r_idx - c_idx
    a = jnp.where((diff >= 0) & (diff <= (r_idx & (chunk - 1))), a, 0.0).astype(BF16)
    v_bf = v_ref[...]
    o = _dot(a, v_bf)

    bl = jnp.concatenate(b_last + [jnp.zeros((LANES - nchunks, GLA_DK), F32)], axis=0)
    dec_cols = jnp.exp(bl.T)

    v_rows = v_bf if chunk >= BF16_ROWS else v_bf.astype(F32)
    if carry:
        @pl.when(pl.program_id(2) == 0)
        def _():
            s_ref[...] = jnp.zeros_like(s_ref)
        state = s_ref[...]
    o_inter = []
    for j in range(nchunks):
        sl = slice(j * chunk, (j + 1) * chunk)
        if not carry:
            state = s0_ref[j]
        qb_j = _pad_rows(qb[sl], mm_rows).astype(BF16)
        o_inter.append(_dot(qb_j, state.astype(BF16))[:chunk])
        kd_j = _pad_rows(kd[sl], mm_rows).astype(BF16)
        v_j = _pad_rows(v_rows[sl], mm_rows).astype(BF16)
        new_state = dec_cols[:, j:j + 1] * state + _dot_tn(kd_j, v_j)
        if carry:
            state = new_state
        else:
            s_ref[j] = new_state
    if carry:
        s_ref[...] = state
    o = o + jnp.concatenate(o_inter, axis=0)

    gz = gz_ref[...].astype(F32)
    o_ref[...] = (_rmsnorm_rows(o, gout_ref[...]) * (gz * _sigmoid(gz))).astype(BF16)


def _gla_in_specs(rows, row_map):
    def spec(width, col0):
        base = col0 // width
        return pl.BlockSpec((rows, width), lambda *ids: (row_map(*ids)[0], base + row_map(*ids)[1]))
    ga_block = Z_GA // LANES
    return [
        spec(GLA_DK, Z_GQ), spec(GLA_DK, Z_GK), spec(GLA_DV, Z_GV), spec(GLA_DV, Z_GZ),
        pl.BlockSpec((rows, LANES), lambda *ids: (row_map(*ids)[0], ga_block)),
        pl.BlockSpec((LANES, GLA_DK), lambda *ids: (0, row_map(*ids)[1])),
        pl.BlockSpec((1, GLA_DK), lambda *ids: (0, row_map(*ids)[1])),
        pl.BlockSpec((1, GLA_DV), lambda *ids: (0, 0)),
    ]


def _gla_prompt(z, wa, ba, gout, batch, seq):
    nchunks = 8
    rows = GLA_CHUNK * nchunks
    nblk = seq // rows
    row_map = lambda b, h, n: (b * nblk + n, h)
    return pl.pallas_call(
        functools.partial(_gla_kernel, chunk=GLA_CHUNK, nchunks=nchunks, carry=True),
        grid=(batch, GLA_H, nblk),
        in_specs=_gla_in_specs(rows, row_map),
        out_specs=[
            pl.BlockSpec((rows, GLA_DV), lambda b, h, n: (b * nblk + n, h)),
            pl.BlockSpec((None, None, GLA_DK, GLA_DV), lambda b, h, n: (b, h, 0, 0)),
        ],
        out_shape=[
            jax.ShapeDtypeStruct((batch * seq, GLA_V), BF16),
            jax.ShapeDtypeStruct((batch, GLA_H, GLA_DK, GLA_DV), F32),
        ],
        compiler_params=_params("parallel", "parallel", "arbitrary"),
        name="gla_prompt",
    )(z, z, z, z, z, wa, ba, gout)


def _gla_sample(z, wa, ba, gout, s0, batch, seq):
    group = 16
    rows = group * seq
    row_map = lambda i, h: (i, h)
    return pl.pallas_call(
        functools.partial(_gla_kernel, chunk=seq, nchunks=group, carry=False),
        grid=(batch // group, GLA_H),
        in_specs=_gla_in_specs(rows, row_map) + [
            pl.BlockSpec((group, None, GLA_DK, GLA_DV), lambda i, h: (i, h, 0, 0)),
        ],
        out_specs=[
            pl.BlockSpec((rows, GLA_DV), lambda i, h: (i, h)),
            pl.BlockSpec((group, None, GLA_DK, GLA_DV), lambda i, h: (i, h, 0, 0)),
        ],
        out_shape=[
            jax.ShapeDtypeStruct((batch * seq, GLA_V), BF16),
            jax.ShapeDtypeStruct((batch, GLA_H, GLA_DK, GLA_DV), F32),
        ],
        compiler_params=_params("parallel", "parallel"),
        name="gla_sample",
    )(z, z, z, z, z, wa, ba, gout, s0)


def _swa_head_of_block(kv_head, blk):
    pairs = SWA_G // 2
    within = 2 * blk if blk < pairs else 2 * (blk - pairs) + 1
    return kv_head * SWA_G + within


def _swa_attend(sink_ref, q, kk, vv, valid, nq):
    nk = kk.shape[0]
    pairs = SWA_G // 2
    lo_q = lax.broadcasted_iota(jnp.int32, (nq, LANES), 1) < SWA_HD
    lo_k = lax.broadcasted_iota(jnp.int32, (nk, LANES), 1) < SWA_HD
    kk_sw = pltpu.roll(kk, SWA_HD, 1)
    vv_sw = pltpu.roll(vv, SWA_HD, 1)
    outs = []
    for kh in range(SWA_KVH):
        k_this, k_other = (kk, kk_sw) if kh == 0 else (kk_sw, kk)
        v_this, v_other = (vv, vv_sw) if kh == 0 else (vv_sw, vv)
        k_dup = jnp.where(lo_k, k_this, k_other).astype(BF16)
        v_even = jnp.where(lo_k, v_this, 0.0).astype(BF16)
        v_odd = jnp.where(lo_k, 0.0, v_other).astype(BF16)
        even, odd = [], []
        for p in range(pairs):
            c0 = kh * SWA_G * SWA_HD + p * LANES
            q_pair = q[:, c0:c0 + LANES]
            even.append(jnp.where(lo_q, q_pair, jnp.zeros_like(q_pair)))
            odd.append(jnp.where(lo_q, jnp.zeros_like(q_pair), q_pair))
        lhs = jnp.concatenate(even + odd, axis=0).astype(BF16)
        s = _dot_nt(lhs, k_dup) * (SWA_HD ** -0.5)
        probs, rinv = [], []
        for blk in range(SWA_G):
            sink = sink_ref[_swa_head_of_block(kh, blk)]
            s_h = jnp.where(valid, s[blk * nq:(blk + 1) * nq], -jnp.inf)
            m = jnp.maximum(jnp.max(s_h, axis=-1, keepdims=True), sink)
            p_h = jnp.exp(s_h - m)
            den = jnp.sum(p_h, axis=-1, keepdims=True) + jnp.exp(sink - m)
            probs.append(p_h)
            rinv.append(1.0 / den)
        p_even = jnp.concatenate(probs[:pairs], axis=0).astype(BF16)
        p_odd = jnp.concatenate(probs[pairs:], axis=0).astype(BF16)
        o = _dot(p_even, v_even) + _dot(p_odd, v_odd)
        for p in range(pairs):
            scale = jnp.where(lo_q, rinv[p], rinv[pairs + p])
            outs.append(o[p * nq:(p + 1) * nq] * scale)
    return jnp.concatenate(outs, axis=1)


def _swa_prompt_kernel(sink_ref, q_ref, sz_ref, kvc_ref, kvp_ref, o_ref):
    n = pl.program_id(1)
    kv = jnp.concatenate([kvp_ref[...], kvc_ref[...]], axis=0)
    i = lax.broadcasted_iota(jnp.int32, (WINDOW, 2 * WINDOW), 0)
    j = lax.broadcasted_iota(jnp.int32, (WINDOW, 2 * WINDOW), 1)
    diff = i + WINDOW - j
    first_key = jnp.where(n > 0, 0, WINDOW)
    valid = (diff >= 0) & (diff <= WINDOW) & (j >= first_key)
    o = _swa_attend(sink_ref, q_ref[...], kv[:, :SWA_KV], kv[:, SWA_KV:], valid, WINDOW)
    sz = sz_ref[...].astype(F32)
    o_ref[...] = (o * (sz * _sigmoid(sz))).astype(BF16)


def _swa_prompt(z, kv, sink, batch, seq):
    nblk = seq // WINDOW
    return pl.pallas_call(
        _swa_prompt_kernel,
        grid=(batch, nblk),
        in_specs=[
            pl.BlockSpec(memory_space=pltpu.SMEM),
            pl.BlockSpec((WINDOW, SWA_W), lambda b, n: (b * nblk + n, Z_SQ // SWA_W)),
            pl.BlockSpec((WINDOW, SWA_W), lambda b, n: (b * nblk + n, Z_SZ // SWA_W)),
            pl.BlockSpec((WINDOW, KV_COLS), lambda b, n: (b * nblk + n, 0)),
            pl.BlockSpec((WINDOW, KV_COLS), lambda b, n: (b * nblk + jnp.maximum(n - 1, 0), 0)),
        ],
        out_specs=pl.BlockSpec((WINDOW, SWA_W), lambda b, n: (b * nblk + n, 0)),
        out_shape=jax.ShapeDtypeStruct((batch * seq, SWA_W), BF16),
        compiler_params=_params("parallel", "parallel"),
        name="swa_prompt",
    )(sink, z, z, kv, kv)


def _swa_sample_kernel(sink_ref, q_ref, sz_ref, kvn_ref, ck_ref, cv_ref, o_ref, nk_ref, nv_ref, *,
                       group, seq):
    wb = ck_ref.shape[1]
    nk = 2 * wb
    i = lax.broadcasted_iota(jnp.int32, (seq, nk), 0)
    j = lax.broadcasted_iota(jnp.int32, (seq, nk), 1)
    diff = i + wb - j
    valid = (diff >= 0) & (diff <= WINDOW)
    q = q_ref[...].astype(F32)
    kvn = kvn_ref[...]
    pad = jnp.zeros((nk - wb - seq, SWA_KV), F32)
    outs = []
    for e in range(group):
        rows = slice(e * seq, (e + 1) * seq)
        k_old, v_old = ck_ref[e], cv_ref[e]
        k_new, v_new = kvn[rows, :SWA_KV], kvn[rows, SWA_KV:]
        nk_ref[e] = jnp.concatenate([k_old[seq:], k_new], axis=0)
        nv_ref[e] = jnp.concatenate([v_old[seq:], v_new], axis=0)
        kk = jnp.concatenate([k_old, k_new, pad], axis=0)
        vv = jnp.concatenate([v_old, v_new, pad], axis=0)
        outs.append(_swa_attend(sink_ref, q[rows], kk, vv, valid, seq))
    sz = sz_ref[...].astype(F32)
    o_ref[...] = (jnp.concatenate(outs, axis=0) * (sz * _sigmoid(sz))).astype(BF16)


def _swa_sample(z, kv, sink, ck, cv, batch, seq):
    group = 8
    rows = group * seq
    wb = ck.shape[1]
    cache_spec = pl.BlockSpec((group, wb, SWA_KV), lambda i: (i, 0, 0))
    return pl.pallas_call(
        functools.partial(_swa_sample_kernel, group=group, seq=seq),
        grid=(batch // group,),
        in_specs=[
            pl.BlockSpec(memory_space=pltpu.SMEM),
            pl.BlockSpec((rows, SWA_W), lambda i: (i, Z_SQ // SWA_W)),
            pl.BlockSpec((rows, SWA_W), lambda i: (i, Z_SZ // SWA_W)),
            pl.BlockSpec((rows, KV_COLS), lambda i: (i, 0)),
            cache_spec, cache_spec,
        ],
        out_specs=[pl.BlockSpec((rows, SWA_W), lambda i: (i, 0)), cache_spec, cache_spec],
        out_shape=[
            jax.ShapeDtypeStruct((batch * seq, SWA_W), BF16),
            jax.ShapeDtypeStruct(ck.shape, F32),
            jax.ShapeDtypeStruct(cv.shape, F32),
        ],
        compiler_params=_params("parallel"),
        name="swa_sample",
    )(sink, z, z, kv, ck, cv)


def _mem_attend_head(q_h, k_h, v_h):
    s = _dot_nt(q_h, k_h) * (X_HD ** -0.5)
    p = jnp.exp(s - jnp.max(s, axis=-1, keepdims=True))
    den = jnp.sum(p, axis=-1, keepdims=True)
    return _dot(p.astype(BF16), v_h) * (1.0 / den)


def _mem_prompt_kernel(q_ref, mz_ref, mk_ref, mv_ref, o_ref):
    outs = []
    for h in range(X_H):
        cols = slice(h * X_HD, (h + 1) * X_HD)
        outs.append(_mem_attend_head(q_ref[:, cols], mk_ref[:, cols].astype(BF16),
                                     mv_ref[:, cols].astype(BF16)))
    mz = mz_ref[...].astype(F32)
    o_ref[...] = (jnp.concatenate(outs, axis=1) * (mz * _sigmoid(mz))).astype(BF16)


def _mem_prompt(z, mk, mv, batch, seq):
    tm = 256
    nblk = seq // tm
    mem_spec = pl.BlockSpec((None, MEM_LEN, X_W), lambda b, n: (b, 0, 0))
    return pl.pallas_call(
        _mem_prompt_kernel,
        grid=(batch, nblk),
        in_specs=[
            pl.BlockSpec((tm, X_W), lambda b, n: (b * nblk + n, Z_MQ // X_W)),
            pl.BlockSpec((tm, X_W), lambda b, n: (b * nblk + n, Z_MZ // X_W)),
            mem_spec, mem_spec,
        ],
        out_specs=pl.BlockSpec((tm, X_W), lambda b, n: (b * nblk + n, 0)),
        out_shape=jax.ShapeDtypeStruct((batch * seq, X_W), BF16),
        compiler_params=_params("parallel", "parallel"),
        name="mem_prompt",
    )(z, z, mk, mv)


def _mem_sample_kernel(q_ref, mz_ref, mk_ref, mv_ref, o_ref, *, group, seq):
    q = q_ref[...].astype(F32)
    rows_out = []
    for e in range(group):
        q_e = _pad_rows(q[e * seq:(e + 1) * seq], BF16_ROWS).astype(BF16)
        outs = []
        for h in range(X_H):
            cols = slice(h * X_HD, (h + 1) * X_HD)
            o = _mem_attend_head(q_e[:, cols], mk_ref[e, :, cols].astype(BF16),
                                 mv_ref[e, :, cols].astype(BF16))
            outs.append(o[:seq])
        rows_out.append(jnp.concatenate(outs, axis=1))
    mz = mz_ref[...].astype(F32)
    o_ref[...] = (jnp.concatenate(rows_out, axis=0) * (mz * _sigmoid(mz))).astype(BF16)


def _mem_sample(z, mk, mv, batch, seq):
    group = 4
    rows = group * seq
    mem_spec = pl.BlockSpec((group, MEM_LEN, X_W), lambda i: (i, 0, 0))
    return pl.pallas_call(
        functools.partial(_mem_sample_kernel, group=group, seq=seq),
        grid=(batch // group,),
        in_specs=[
            pl.BlockSpec((rows, X_W), lambda i: (i, Z_MQ // X_W)),
            pl.BlockSpec((rows, X_W), lambda i: (i, Z_MZ // X_W)),
            mem_spec, mem_spec,
        ],
        out_specs=pl.BlockSpec((rows, X_W), lambda i: (i, 0)),
        out_shape=jax.ShapeDtypeStruct((batch * seq, X_W), BF16),
        compiler_params=_params("parallel"),
        name="mem_sample",
    )(z, z, mk, mv)


def _merge_kernel(x_ref, a_ref, s_ref, m_ref, ga_ref, gs_ref, gm_ref, wa_ref, ws_ref, wm_ref,
                  wo_ref, *rest, final):
    if final:
        gf_ref, y_ref = rest
    else:
        (y_ref,) = rest
    merged = (_sigmoid(ga_ref[...].astype(F32)) * _dot(a_ref[...], wa_ref[...])
              + _sigmoid(gs_ref[...].astype(F32)) * _dot(s_ref[...], ws_ref[...])
              + _sigmoid(gm_ref[...].astype(F32)) * _dot(m_ref[...], wm_ref[...]))
    y = x_ref[...] + _dot(merged.astype(BF16), wo_ref[...])
    if final:
        y = _rmsnorm_rows(y, gf_ref[...])
    y_ref[...] = y


def _merge(x2d, o_gla, o_swa, o_mem, z, w_gla, w_swa, w_mem, w_out, g_final):
    t, d = x2d.shape
    tm = 256
    final = g_final is not None
    row = lambda width: pl.BlockSpec((tm, width), lambda i: (i, 0))
    gate = lambda k: pl.BlockSpec((tm, d), lambda i: (i, Z_MG // d + k))
    weight = pl.BlockSpec((d, d), lambda i: (0, 0))
    in_specs = [row(d), row(d), row(d), row(d), gate(0), gate(1), gate(2),
                weight, weight, weight, weight]
    args = [x2d, o_gla, o_swa, o_mem, z, z, z, w_gla, w_swa, w_mem, w_out]
    if final:
        in_specs.append(pl.BlockSpec((1, d), lambda i: (0, 0)))
        args.append(g_final)
    return pl.pallas_call(
        functools.partial(_merge_kernel, final=final),
        grid=(t // tm,),
        in_specs=in_specs,
        out_specs=row(d),
        out_shape=jax.ShapeDtypeStruct((t, d), F32),
        compiler_params=_params("parallel"),
        name="merge_final" if final else "merge",
    )(*args)


def _regroup_w_in(w_in):
    cuts, acc = [], 0
    for width in IN_SPLITS[:-1]:
        acc += width
        cuts.append(acc)
    gq, gk, gv, gz, ga, sq, sk, sv, sz, mq, mz, mg = jnp.split(w_in, cuts, axis=-1)
    ga = jnp.pad(ga, ((0, 0), (0, 0), (0, LANES - GLA_RANK)))
    return jnp.concatenate([gq, gk, gv, gz, sq, sz, mq, mz, mg, ga, sk, sv], axis=-1).astype(BF16)


def kernel(x_prompt, x_sample, mem_prompt, state_gla, cache_swa_k, cache_swa_v, cache_mem_k,
           cache_mem_v, g_norm, w_in, w_gla_a, b_gla_a, g_gla_out, swa_sink, g_mem, w_mem_kv,
           w_br_gla, w_br_swa, w_br_mem, w_out, g_final):
    depth = w_in.shape[0]
    batch, seq, d = x_prompt.shape
    dec_batch, dec_seq, _ = x_sample.shape
    wb = cache_swa_k.shape[2]

    w_in_r = _regroup_w_in(w_in)
    w_a = jnp.pad(w_gla_a, ((0, 0), (0, LANES - GLA_RANK), (0, 0))).astype(BF16)
    w_mem_b = w_mem_kv.astype(BF16)
    w_g, w_s, w_m, w_o = (w.astype(BF16) for w in (w_br_gla, w_br_swa, w_br_mem, w_out))
    g_fin = g_final.reshape(1, d)

    yp = x_prompt.reshape(batch * seq, d)
    ys = x_sample.reshape(dec_batch * dec_seq, d)
    mem2d = mem_prompt.reshape(batch * MEM_LEN, d)
    gla_p, kp_l, vp_l, mk_l, mv_l, gla_s, ks_l, vs_l = ([] for _ in range(8))
    for l in range(depth):
        last = l == depth - 1
        gn = g_norm[l].reshape(1, d)
        ba = b_gla_a[l].reshape(1, GLA_QK)
        gout = g_gla_out[l].reshape(1, GLA_DV)
        sink = swa_sink[l]

        mk, mv = _mem_kv(mem2d, g_mem[l].reshape(1, d), w_mem_b[l])
        mk3 = mk.reshape(batch, MEM_LEN, X_W)
        mv3 = mv.reshape(batch, MEM_LEN, X_W)

        z, kv = _in_proj(yp, gn, w_in_r[l])
        o_gla, s_p = _gla_prompt(z, w_a[l], ba, gout, batch, seq)
        o_swa = _swa_prompt(z, kv, sink, batch, seq)
        o_mem = _mem_prompt(z, mk3, mv3, batch, seq)
        yp = _merge(yp, o_gla, o_swa, o_mem, z, w_g[l], w_s[l], w_m[l], w_o[l],
                    g_fin if last else None)
        kv3 = kv.reshape(batch, seq, KV_COLS)[:, seq - WINDOW:]
        gla_p.append(s_p)
        kp_l.append(kv3[..., :SWA_KV].reshape(batch, WINDOW, SWA_KVH, SWA_HD))
        vp_l.append(kv3[..., SWA_KV:].reshape(batch, WINDOW, SWA_KVH, SWA_HD))
        mk_l.append(mk.reshape(batch, MEM_LEN, X_H, X_HD))
        mv_l.append(mv.reshape(batch, MEM_LEN, X_H, X_HD))

        z, kv = _in_proj(ys, gn, w_in_r[l])
        o_gla, s_s = _gla_sample(z, w_a[l], ba, gout, state_gla[l], dec_batch, dec_seq)
        o_swa, k_s, v_s = _swa_sample(z, kv, sink, cache_swa_k[l].reshape(dec_batch, wb, SWA_KV),
                                      cache_swa_v[l].reshape(dec_batch, wb, SWA_KV),
                                      dec_batch, dec_seq)
        o_mem = _mem_sample(z, cache_mem_k[l].reshape(dec_batch, MEM_LEN, X_W),
                            cache_mem_v[l].reshape(dec_batch, MEM_LEN, X_W), dec_batch, dec_seq)
        ys = _merge(ys, o_gla, o_swa, o_mem, z, w_g[l], w_s[l], w_m[l], w_o[l],
                    g_fin if last else None)
        gla_s.append(s_s)
        ks_l.append(k_s.reshape(dec_batch, wb, SWA_KVH, SWA_HD))
        vs_l.append(v_s.reshape(dec_batch, wb, SWA_KVH, SWA_HD))

    return (yp.reshape(batch, seq, d), ys.reshape(dec_batch, dec_seq, d), jnp.stack(gla_p),
            jnp.stack(kp_l), jnp.stack(vp_l), jnp.stack(mk_l), jnp.stack(mv_l), jnp.stack(gla_s),
            jnp.stack(ks_l), jnp.stack(vs_l))
```

```python
import functools

import jax
import jax.numpy as jnp
from jax import lax
from jax.experimental import pallas as pl
from jax.experimental.pallas import tpu as pltpu

F32 = jnp.float32
BF16 = jnp.bfloat16

D_MODEL = 1024
GLA_H = 4
GLA_DK = 128
GLA_DV = 256
GLA_QK = GLA_H * GLA_DK
GLA_V = GLA_H * GLA_DV
GLA_RANK = 16
GLA_TAU = 16.0
GLA_CHUNK = 32
SWA_HD = 64
SWA_H = 16
SWA_KVH = 2
SWA_G = SWA_H // SWA_KVH
SWA_W = SWA_H * SWA_HD
SWA_KV = SWA_KVH * SWA_HD
WINDOW = 128
MEM_LEN = 256
X_H = 4
X_HD = 256
X_W = X_H * X_HD
N_BRANCH = 3
IN_SPLITS = (GLA_QK, GLA_QK, GLA_V, GLA_V, GLA_RANK, SWA_W, SWA_KV, SWA_KV, SWA_W, X_W, X_W,
             N_BRANCH * D_MODEL)
EPS = 1e-6

LANES = 128
BF16_ROWS = 16
VMEM_LIMIT = 56 * 1024 * 1024

Z_GQ = 0
Z_GK = Z_GQ + GLA_QK
Z_GV = Z_GK + GLA_QK
Z_GZ = Z_GV + GLA_V
Z_SQ = Z_GZ + GLA_V
Z_SZ = Z_SQ + SWA_W
Z_MQ = Z_SZ + SWA_W
Z_MZ = Z_MQ + X_W
Z_MG = Z_MZ + X_W
Z_GA = Z_MG + N_BRANCH * D_MODEL
Z_COLS = Z_GA + LANES
KV_COLS = 2 * SWA_KV


def _row_tile(rows):
    return 512 if rows >= 4096 else 256


def _params(*semantics):
    return pltpu.CompilerParams(dimension_semantics=semantics, vmem_limit_bytes=VMEM_LIMIT)


def _sigmoid(x):
    return 1.0 / (1.0 + jnp.exp(-x))


def _log_sigmoid(x):
    return jnp.minimum(x, 0.0) - jnp.log(1.0 + jnp.exp(-jnp.abs(x)))


def _rmsnorm_rows(x, g):
    return x * lax.rsqrt(jnp.mean(x * x, axis=-1, keepdims=True) + EPS) * g


def _dot(a, b):
    return jnp.dot(a, b, preferred_element_type=F32)


def _dot_nt(a, b):
    return lax.dot_general(a, b, (((1,), (1,)), ((), ())), preferred_element_type=F32)


def _dot_tn(a, b):
    return lax.dot_general(a, b, (((0,), (0,)), ((), ())), preferred_element_type=F32)


def _norm_matmul_kernel(x_ref, g_ref, w_ref, *out_refs, plan):
    h = _rmsnorm_rows(x_ref[...], g_ref[...]).astype(BF16)
    for out_idx, out_c0, w_c0, width in plan:
        o_ref = out_refs[out_idx]
        o_ref[:, out_c0:out_c0 + width] = _dot(h, w_ref[:, w_c0:w_c0 + width]).astype(o_ref.dtype)


def _norm_matmul(x2d, g, w, outs, plan, tm, name):
    t, d = x2d.shape
    n = w.shape[1]
    return pl.pallas_call(
        functools.partial(_norm_matmul_kernel, plan=tuple(plan)),
        grid=(t // tm,),
        in_specs=[
            pl.BlockSpec((tm, d), lambda i: (i, 0)),
            pl.BlockSpec((1, d), lambda i: (0, 0)),
            pl.BlockSpec((d, n), lambda i: (0, 0), pipeline_mode=pl.Buffered(1)),
        ],
        out_specs=[pl.BlockSpec((tm, c), lambda i: (i, 0)) for c, _ in outs],
        out_shape=[jax.ShapeDtypeStruct((t, c), dt) for c, dt in outs],
        compiler_params=_params("parallel"),
        name=name,
    )(x2d, g, w)


def _chunk_plan(out_idx, out_c0, w_c0, width, chunk):
    plan = []
    off = 0
    while off < width:
        step = min(chunk, width - off)
        plan.append((out_idx, out_c0 + off, w_c0 + off, step))
        off += step
    return plan


def _in_proj(x2d, g, w):
    plan = _chunk_plan(0, 0, 0, Z_COLS, 512) + _chunk_plan(1, 0, Z_COLS, KV_COLS, KV_COLS)
    return _norm_matmul(x2d, g, w, [(Z_COLS, BF16), (KV_COLS, F32)], plan,
                        tm=_row_tile(x2d.shape[0]), name="in_proj")


def _mem_kv(mem2d, g, w):
    plan = _chunk_plan(0, 0, 0, X_W, 512) + _chunk_plan(1, 0, X_W, X_W, 512)
    return _norm_matmul(mem2d, g, w, [(X_W, F32), (X_W, F32)], plan, tm=256, name="mem_kv")


def _chunk_cumsum(g, chunk):
    pos = lax.broadcasted_iota(jnp.int32, g.shape, 0) & (chunk - 1)
    b = g
    shift = 1
    while shift < chunk:
        b = b + jnp.where(pos >= shift, pltpu.roll(b, shift, 0), 0.0)
        shift *= 2
    return b


def _pad_rows(x, rows):
    if x.shape[0] >= rows:
        return x
    return jnp.concatenate([x, jnp.zeros((rows - x.shape[0], x.shape[1]), x.dtype)], axis=0)


def _gla_kernel(q_ref, k_ref, v_ref, gz_ref, ga_ref, wa_ref, ba_ref, gout_ref, *rest, chunk,
                nchunks, carry, aliased, anchored):
    refs = list(rest)
    s0_ref = None if carry else refs.pop(0)
    if aliased:
        refs.pop(0)
    o_ref, s_ref = refs
    rows = chunk * nchunks
    mm_rows = max(chunk, BF16_ROWS)

    def chunk_row(x, row):
        picked = [x[j * chunk + row:j * chunk + row + 1, :] for j in range(nchunks)]
        return picked, jnp.concatenate([jnp.broadcast_to(r, (chunk, x.shape[1])) for r in picked],
                                       axis=0)

    x = _dot(ga_ref[...], wa_ref[...]) + ba_ref[...]
    g = _log_sigmoid(x) * (1.0 / GLA_TAU)
    b = _chunk_cumsum(g, chunk)
    b_last, b_last_full = chunk_row(b, chunk - 1)
    b_last_rows = jnp.concatenate(b_last + [jnp.zeros((LANES - nchunks, GLA_QK), F32)], axis=0)

    q = q_ref[...].astype(F32) * (GLA_DK ** -0.5)
    k = k_ref[...].astype(F32)
    qb = q * jnp.exp(b)
    kd = k * jnp.exp(b_last_full - b)
    qb_bf, kd_bf = qb.astype(BF16), kd.astype(BF16)
    if anchored:
        _, b_mid_full = chunk_row(b, chunk // 2 - 1)
        qa_bf = (q * jnp.exp(b - b_mid_full)).astype(BF16)
        ka_bf = (k * jnp.exp(b_mid_full - b)).astype(BF16)
    else:
        qa_bf = qb_bf
        ka_bf = (k * jnp.exp(-b)).astype(BF16)

    r_idx = lax.broadcasted_iota(jnp.int32, (rows, rows), 0)
    c_idx = lax.broadcasted_iota(jnp.int32, (rows, rows), 1)
    diff = r_idx - c_idx
    mask = (diff >= 0) & (diff <= (r_idx & (chunk - 1)))

    def chunk_rows(x_f32, x_bf, sl, lanes):
        if chunk >= BF16_ROWS:
            return x_bf[sl, lanes]
        return _pad_rows(x_f32[sl, lanes], mm_rows).astype(BF16)

    if carry:
        @pl.when(pl.program_id(1) == 0)
        def _():
            s_ref[...] = jnp.zeros_like(s_ref)

    heads = range(GLA_H)
    lanes = [slice(h * GLA_DK, (h + 1) * GLA_DK) for h in heads]
    vl = [slice(h * GLA_DV, (h + 1) * GLA_DV) for h in heads]
    v_bf = v_ref[...]
    v_f32 = None if chunk >= BF16_ROWS else v_bf.astype(F32)
    o_intra = [_dot(jnp.where(mask, _dot_nt(qa_bf[:, lanes[h]], ka_bf[:, lanes[h]]),
                              0.0).astype(BF16), v_bf[:, vl[h]]) for h in heads]
    dec_cols = [jnp.exp(b_last_rows[:, lanes[h]].T) for h in heads]
    states = [s_ref[h] for h in heads] if carry else None

    o_inter = [[] for _ in heads]
    for j in range(nchunks):
        sl = slice(j * chunk, (j + 1) * chunk)
        for h in heads:
            state = states[h] if carry else s0_ref[j, h]
            o_inter[h].append(
                _dot(chunk_rows(qb, qb_bf, sl, lanes[h]), state.astype(BF16))[:chunk])
            v_j = (v_bf[sl, vl[h]] if chunk >= BF16_ROWS
                   else _pad_rows(v_f32[sl, vl[h]], mm_rows).astype(BF16))
            new_state = (dec_cols[h][:, j:j + 1] * state
                         + _dot_tn(chunk_rows(kd, kd_bf, sl, lanes[h]), v_j))
            if carry:
                states[h] = new_state
            else:
                s_ref[j, h] = new_state

    gz = gz_ref[...].astype(F32)
    gate = gz * _sigmoid(gz)
    for h in heads:
        if carry:
            s_ref[h] = states[h]
        o = o_intra[h] + jnp.concatenate(o_inter[h], axis=0)
        o_ref[:, vl[h]] = (_rmsnorm_rows(o, gout_ref[...]) * gate[:, vl[h]]).astype(BF16)


def _gla_call(z, wa, ba, gout, s0, prev, layer, depth, batch, rows, nblk, chunk, group):
    carry = s0 is None
    if carry:
        grid = (batch, nblk)
        row_blk = lambda b, n: b * nblk + n
        state_blk = (None, None, GLA_H, GLA_DK, GLA_DV)
        state_map = lambda b, n: (layer, b, 0, 0, 0)
        semantics = ("parallel", "arbitrary")
    else:
        grid = (batch // group,)
        row_blk = lambda i: i
        state_blk = (None, group, GLA_H, GLA_DK, GLA_DV)
        state_map = lambda i: (layer, i, 0, 0, 0)
        semantics = ("parallel",)
    zspec = lambda width, col0: pl.BlockSpec((rows, width),
                                             lambda *ids: (row_blk(*ids), col0 // width))
    const = lambda shape: pl.BlockSpec(shape, lambda *ids: (0,) * len(shape))
    in_specs = [zspec(GLA_QK, Z_GQ), zspec(GLA_QK, Z_GK), zspec(GLA_V, Z_GV), zspec(GLA_V, Z_GZ),
                zspec(LANES, Z_GA), const((LANES, GLA_QK)), const((1, GLA_QK)), const((1, GLA_DV))]
    args = [z, z, z, z, z, wa, ba, gout]
    if not carry:
        in_specs.append(pl.BlockSpec(state_blk, state_map))
        args.append(s0)
    aliases = {}
    if prev is not None:
        aliases = {len(args): 1}
        in_specs.append(pl.BlockSpec(memory_space=pl.ANY))
        args.append(prev)
    return pl.pallas_call(
        functools.partial(_gla_kernel, chunk=chunk, nchunks=rows // chunk, carry=carry,
                          aliased=prev is not None, anchored=carry),
        grid=grid,
        in_specs=in_specs,
        out_specs=[pl.BlockSpec((rows, GLA_V), lambda *ids: (row_blk(*ids), 0)),
                   pl.BlockSpec(state_blk, state_map)],
        out_shape=[jax.ShapeDtypeStruct((z.shape[0], GLA_V), BF16),
                   jax.ShapeDtypeStruct((depth, batch, GLA_H, GLA_DK, GLA_DV), F32)],
        input_output_aliases=aliases,
        compiler_params=_params(*semantics),
        name="gla_prompt" if carry else "gla_sample",
    )(*args)


def _swa_head_of_block(kv_head, blk):
    pairs = SWA_G // 2
    within = 2 * blk if blk < pairs else 2 * (blk - pairs) + 1
    return kv_head * SWA_G + within


def _swa_attend(sink_ref, q, kk, vv, valid, nq):
    nk = kk.shape[0]
    pairs = SWA_G // 2
    lo_q = lax.broadcasted_iota(jnp.int32, (nq, LANES), 1) < SWA_HD
    lo_k = lax.broadcasted_iota(jnp.int32, (nk, LANES), 1) < SWA_HD
    kk_sw = pltpu.roll(kk, SWA_HD, 1)
    vv_sw = pltpu.roll(vv, SWA_HD, 1)
    outs = []
    for kh in range(SWA_KVH):
        k_this, k_other = (kk, kk_sw) if kh == 0 else (kk_sw, kk)
        v_this, v_other = (vv, vv_sw) if kh == 0 else (vv_sw, vv)
        k_dup = jnp.where(lo_k, k_this, k_other).astype(BF16)
        v_even = jnp.where(lo_k, v_this, 0.0).astype(BF16)
        v_odd = jnp.where(lo_k, 0.0, v_other).astype(BF16)
        even, odd = [], []
        for p in range(pairs):
            c0 = kh * SWA_G * SWA_HD + p * LANES
            q_pair = q[:, c0:c0 + LANES]
            even.append(jnp.where(lo_q, q_pair, jnp.zeros_like(q_pair)))
            odd.append(jnp.where(lo_q, jnp.zeros_like(q_pair), q_pair))
        lhs = jnp.concatenate(even + odd, axis=0).astype(BF16)
        s = _dot_nt(lhs, k_dup) * (SWA_HD ** -0.5)
        probs, rinv = [], []
        for blk in range(SWA_G):
            sink = sink_ref[_swa_head_of_block(kh, blk)]
            s_h = jnp.where(valid, s[blk * nq:(blk + 1) * nq], -jnp.inf)
            m = jnp.maximum(jnp.max(s_h, axis=-1, keepdims=True), sink)
            p_h = jnp.exp(s_h - m)
            den = jnp.sum(p_h, axis=-1, keepdims=True) + jnp.exp(sink - m)
            probs.append(p_h)
            rinv.append(1.0 / den)
        p_even = jnp.concatenate(probs[:pairs], axis=0).astype(BF16)
        p_odd = jnp.concatenate(probs[pairs:], axis=0).astype(BF16)
        o = _dot(p_even, v_even) + _dot(p_odd, v_odd)
        for p in range(pairs):
            scale = jnp.where(lo_q, rinv[p], rinv[pairs + p])
            outs.append(o[p * nq:(p + 1) * nq] * scale)
    return jnp.concatenate(outs, axis=1)


def _swa_prompt_kernel(sink_ref, q_ref, sz_ref, kvc_ref, kvp_ref, o_ref):
    n = pl.program_id(1)
    kv = jnp.concatenate([kvp_ref[...], kvc_ref[...]], axis=0)
    i = lax.broadcasted_iota(jnp.int32, (WINDOW, 2 * WINDOW), 0)
    j = lax.broadcasted_iota(jnp.int32, (WINDOW, 2 * WINDOW), 1)
    diff = i + WINDOW - j
    first_key = jnp.where(n > 0, 0, WINDOW)
    valid = (diff >= 0) & (diff <= WINDOW) & (j >= first_key)
    o = _swa_attend(sink_ref, q_ref[...], kv[:, :SWA_KV], kv[:, SWA_KV:], valid, WINDOW)
    sz = sz_ref[...].astype(F32)
    o_ref[...] = (o * (sz * _sigmoid(sz))).astype(BF16)


def _swa_prompt(z, kv, sink, batch, seq):
    nblk = seq // WINDOW
    return pl.pallas_call(
        _swa_prompt_kernel,
        grid=(batch, nblk),
        in_specs=[
            pl.BlockSpec(memory_space=pltpu.SMEM),
            pl.BlockSpec((WINDOW, SWA_W), lambda b, n: (b * nblk + n, Z_SQ // SWA_W)),
            pl.BlockSpec((WINDOW, SWA_W), lambda b, n: (b * nblk + n, Z_SZ // SWA_W)),
            pl.BlockSpec((WINDOW, KV_COLS), lambda b, n: (b * nblk + n, 0)),
            pl.BlockSpec((WINDOW, KV_COLS), lambda b, n: (b * nblk + jnp.maximum(n - 1, 0), 0)),
        ],
        out_specs=pl.BlockSpec((WINDOW, SWA_W), lambda b, n: (b * nblk + n, 0)),
        out_shape=jax.ShapeDtypeStruct((batch * seq, SWA_W), BF16),
        compiler_params=_params("parallel", "parallel"),
        name="swa_prompt",
    )(sink, z, z, kv, kv)


def _swa_sample_kernel(sink_ref, q_ref, sz_ref, kvn_ref, ck_ref, cv_ref, *rest, group, seq):
    o_ref, nk_ref, nv_ref = rest[-3:]
    wb = ck_ref.shape[1]
    nk = 2 * wb
    i = lax.broadcasted_iota(jnp.int32, (seq, nk), 0)
    j = lax.broadcasted_iota(jnp.int32, (seq, nk), 1)
    diff = i + wb - j
    valid = (diff >= 0) & (diff <= WINDOW)
    q = q_ref[...].astype(F32)
    kvn = kvn_ref[...]
    pad = jnp.zeros((nk - wb - seq, SWA_KV), F32)
    outs = []
    for e in range(group):
        rows = slice(e * seq, (e + 1) * seq)
        k_old, v_old = ck_ref[e], cv_ref[e]
        k_new, v_new = kvn[rows, :SWA_KV], kvn[rows, SWA_KV:]
        nk_ref[e] = jnp.concatenate([k_old[seq:], k_new], axis=0)
        nv_ref[e] = jnp.concatenate([v_old[seq:], v_new], axis=0)
        kk = jnp.concatenate([k_old, k_new, pad], axis=0)
        vv = jnp.concatenate([v_old, v_new, pad], axis=0)
        outs.append(_swa_attend(sink_ref, q[rows], kk, vv, valid, seq))
    sz = sz_ref[...].astype(F32)
    o_ref[...] = (jnp.concatenate(outs, axis=0) * (sz * _sigmoid(sz))).astype(BF16)


def _swa_sample(z, kv, sink, ck, cv, prev, layer, batch, seq):
    group = 8
    rows = group * seq
    wb = ck.shape[2]
    cache_spec = pl.BlockSpec((None, group, wb, SWA_KV), lambda i: (layer, i, 0, 0))
    in_specs = [
        pl.BlockSpec(memory_space=pltpu.SMEM),
        pl.BlockSpec((rows, SWA_W), lambda i: (i, Z_SQ // SWA_W)),
        pl.BlockSpec((rows, SWA_W), lambda i: (i, Z_SZ // SWA_W)),
        pl.BlockSpec((rows, KV_COLS), lambda i: (i, 0)),
        cache_spec, cache_spec,
    ]
    args = [sink, z, z, kv, ck, cv]
    aliases = {}
    if prev is not None:
        aliases = {len(args): 1, len(args) + 1: 2}
        in_specs += [pl.BlockSpec(memory_space=pl.ANY)] * 2
        args += list(prev)
    return pl.pallas_call(
        functools.partial(_swa_sample_kernel, group=group, seq=seq),
        grid=(batch // group,),
        in_specs=in_specs,
        out_specs=[pl.BlockSpec((rows, SWA_W), lambda i: (i, 0)), cache_spec, cache_spec],
        out_shape=[
            jax.ShapeDtypeStruct((batch * seq, SWA_W), BF16),
            jax.ShapeDtypeStruct(ck.shape, F32),
            jax.ShapeDtypeStruct(cv.shape, F32),
        ],
        input_output_aliases=aliases,
        compiler_params=_params("parallel"),
        name="swa_sample",
    )(*args)


def _mem_attend_head(q_h, k_h, v_h):
    s = _dot_nt(q_h, k_h) * (X_HD ** -0.5)
    p = jnp.exp(s - jnp.max(s, axis=-1, keepdims=True))
    den = jnp.sum(p, axis=-1, keepdims=True)
    return _dot(p.astype(BF16), v_h) * (1.0 / den)


def _mem_prompt_kernel(q_ref, mz_ref, mk_ref, mv_ref, o_ref):
    outs = []
    for h in range(X_H):
        cols = slice(h * X_HD, (h + 1) * X_HD)
        outs.append(_mem_attend_head(q_ref[:, cols], mk_ref[:, cols].astype(BF16),
                                     mv_ref[:, cols].astype(BF16)))
    mz = mz_ref[...].astype(F32)
    o_ref[...] = (jnp.concatenate(outs, axis=1) * (mz * _sigmoid(mz))).astype(BF16)


def _mem_prompt(z, mk, mv, batch, seq):
    tm = 256
    nblk = seq // tm
    mem_spec = pl.BlockSpec((None, MEM_LEN, X_W), lambda b, n: (b, 0, 0))
    return pl.pallas_call(
        _mem_prompt_kernel,
        grid=(batch, nblk),
        in_specs=[
            pl.BlockSpec((tm, X_W), lambda b, n: (b * nblk + n, Z_MQ // X_W)),
            pl.BlockSpec((tm, X_W), lambda b, n: (b * nblk + n, Z_MZ // X_W)),
            mem_spec, mem_spec,
        ],
        out_specs=pl.BlockSpec((tm, X_W), lambda b, n: (b * nblk + n, 0)),
        out_shape=jax.ShapeDtypeStruct((batch * seq, X_W), BF16),
        compiler_params=_params("parallel", "parallel"),
        name="mem_prompt",
    )(z, z, mk, mv)


def _mem_sample_kernel(q_ref, mz_ref, mk_ref, mv_ref, o_ref, *, group, seq):
    q = q_ref[...].astype(F32)
    rows_out = []
    for e in range(group):
        q_e = _pad_rows(q[e * seq:(e + 1) * seq], BF16_ROWS).astype(BF16)
        outs = []
        for h in range(X_H):
            cols = slice(h * X_HD, (h + 1) * X_HD)
            o = _mem_attend_head(q_e[:, cols], mk_ref[e, :, cols].astype(BF16),
                                 mv_ref[e, :, cols].astype(BF16))
            outs.append(o[:seq])
        rows_out.append(jnp.concatenate(outs, axis=1))
    mz = mz_ref[...].astype(F32)
    o_ref[...] = (jnp.concatenate(rows_out, axis=0) * (mz * _sigmoid(mz))).astype(BF16)


def _mem_sample(z, mk, mv, layer, batch, seq):
    group = 4
    rows = group * seq
    mem_spec = pl.BlockSpec((None, group, MEM_LEN, X_W), lambda i: (layer, i, 0, 0))
    return pl.pallas_call(
        functools.partial(_mem_sample_kernel, group=group, seq=seq),
        grid=(batch // group,),
        in_specs=[
            pl.BlockSpec((rows, X_W), lambda i: (i, Z_MQ // X_W)),
            pl.BlockSpec((rows, X_W), lambda i: (i, Z_MZ // X_W)),
            mem_spec, mem_spec,
        ],
        out_specs=pl.BlockSpec((rows, X_W), lambda i: (i, 0)),
        out_shape=jax.ShapeDtypeStruct((batch * seq, X_W), BF16),
        compiler_params=_params("parallel"),
        name="mem_sample",
    )(z, z, mk, mv)


def _merge_kernel(x_ref, a_ref, s_ref, m_ref, ga_ref, gs_ref, gm_ref, wa_ref, ws_ref, wm_ref,
                  wo_ref, *rest, final):
    if final:
        gf_ref, y_ref = rest
    else:
        (y_ref,) = rest
    merged = (_sigmoid(ga_ref[...].astype(F32)) * _dot(a_ref[...], wa_ref[...])
              + _sigmoid(gs_ref[...].astype(F32)) * _dot(s_ref[...], ws_ref[...])
              + _sigmoid(gm_ref[...].astype(F32)) * _dot(m_ref[...], wm_ref[...]))
    y = x_ref[...] + _dot(merged.astype(BF16), wo_ref[...])
    if final:
        y = _rmsnorm_rows(y, gf_ref[...])
    y_ref[...] = y


def _merge(x2d, o_gla, o_swa, o_mem, z, w_gla, w_swa, w_mem, w_out, g_final):
    t, d = x2d.shape
    tm = _row_tile(t)
    final = g_final is not None
    row = lambda width: pl.BlockSpec((tm, width), lambda i: (i, 0))
    gate = lambda k: pl.BlockSpec((tm, d), lambda i: (i, Z_MG // d + k))
    weight = pl.BlockSpec((d, d), lambda i: (0, 0))
    in_specs = [row(d), row(d), row(d), row(d), gate(0), gate(1), gate(2),
                weight, weight, weight, weight]
    args = [x2d, o_gla, o_swa, o_mem, z, z, z, w_gla, w_swa, w_mem, w_out]
    if final:
        in_specs.append(pl.BlockSpec((1, d), lambda i: (0, 0)))
        args.append(g_final)
    return pl.pallas_call(
        functools.partial(_merge_kernel, final=final),
        grid=(t // tm,),
        in_specs=in_specs,
        out_specs=row(d),
        out_shape=jax.ShapeDtypeStruct((t, d), F32),
        compiler_params=_params("parallel"),
        name="merge_final" if final else "merge",
    )(*args)


def _regroup_w_in(w_in):
    cuts, acc = [], 0
    for width in IN_SPLITS[:-1]:
        acc += width
        cuts.append(acc)
    gq, gk, gv, gz, ga, sq, sk, sv, sz, mq, mz, mg = jnp.split(w_in, cuts, axis=-1)
    ga = jnp.pad(ga, ((0, 0), (0, 0), (0, LANES - GLA_RANK)))
    return jnp.concatenate([gq, gk, gv, gz, sq, sz, mq, mz, mg, ga, sk, sv], axis=-1).astype(BF16)


def kernel(x_prompt, x_sample, mem_prompt, state_gla, cache_swa_k, cache_swa_v, cache_mem_k,
           cache_mem_v, g_norm, w_in, w_gla_a, b_gla_a, g_gla_out, swa_sink, g_mem, w_mem_kv,
           w_br_gla, w_br_swa, w_br_mem, w_out, g_final):
    depth = w_in.shape[0]
    batch, seq, d = x_prompt.shape
    dec_batch, dec_seq, _ = x_sample.shape
    wb = cache_swa_k.shape[2]

    w_in_r = _regroup_w_in(w_in)
    w_a = jnp.pad(w_gla_a, ((0, 0), (0, LANES - GLA_RANK), (0, 0))).astype(BF16)
    w_mem_b = w_mem_kv.astype(BF16)
    w_g, w_s, w_m, w_o = (w.astype(BF16) for w in (w_br_gla, w_br_swa, w_br_mem, w_out))
    g_fin = g_final.reshape(1, d)

    yp = x_prompt.reshape(batch * seq, d)
    ys = x_sample.reshape(dec_batch * dec_seq, d)
    mem2d = mem_prompt.reshape(batch * MEM_LEN, d)
    ck_all = cache_swa_k.reshape(depth, dec_batch, wb, SWA_KV)
    cv_all = cache_swa_v.reshape(depth, dec_batch, wb, SWA_KV)
    cmk_all = cache_mem_k.reshape(depth, dec_batch, MEM_LEN, X_W)
    cmv_all = cache_mem_v.reshape(depth, dec_batch, MEM_LEN, X_W)
    gla_p = gla_s = swa_new = None
    prompt_chunk = 2 * GLA_CHUNK
    prompt_rows = 4 * prompt_chunk
    gla_group = 8
    kp_l, vp_l, mk_l, mv_l = [], [], [], []
    for l in range(depth):
        last = l == depth - 1
        gn = g_norm[l].reshape(1, d)
        ba = b_gla_a[l].reshape(1, GLA_QK)
        gout = g_gla_out[l].reshape(1, GLA_DV)
        sink = swa_sink[l]

        mk, mv = _mem_kv(mem2d, g_mem[l].reshape(1, d), w_mem_b[l])
        mk3 = mk.reshape(batch, MEM_LEN, X_W)
        mv3 = mv.reshape(batch, MEM_LEN, X_W)

        z, kv = _in_proj(yp, gn, w_in_r[l])
        o_gla, gla_p = _gla_call(z, w_a[l], ba, gout, None, gla_p, l, depth, batch, prompt_rows,
                                 seq // prompt_rows, prompt_chunk, None)
        o_swa = _swa_prompt(z, kv, sink, batch, seq)
        o_mem = _mem_prompt(z, mk3, mv3, batch, seq)
        yp = _merge(yp, o_gla, o_swa, o_mem, z, w_g[l], w_s[l], w_m[l], w_o[l],
                    g_fin if last else None)
        kv3 = kv.reshape(batch, seq, KV_COLS)[:, seq - WINDOW:]
        kp_l.append(kv3[..., :SWA_KV].reshape(batch, WINDOW, SWA_KVH, SWA_HD))
        vp_l.append(kv3[..., SWA_KV:].reshape(batch, WINDOW, SWA_KVH, SWA_HD))
        mk_l.append(mk.reshape(batch, MEM_LEN, X_H, X_HD))
        mv_l.append(mv.reshape(batch, MEM_LEN, X_H, X_HD))

        z, kv = _in_proj(ys, gn, w_in_r[l])
        o_gla, gla_s = _gla_call(z, w_a[l], ba, gout, state_gla, gla_s, l, depth, dec_batch,
                                 gla_group * dec_seq, None, dec_seq, gla_group)
        o_swa, *swa_new = _swa_sample(z, kv, sink, ck_all, cv_all, swa_new, l, dec_batch, dec_seq)
        o_mem = _mem_sample(z, cmk_all, cmv_all, l, dec_batch, dec_seq)
        ys = _merge(ys, o_gla, o_swa, o_mem, z, w_g[l], w_s[l], w_m[l], w_o[l],
                    g_fin if last else None)

    k_s, v_s = (c.reshape(depth, dec_batch, wb, SWA_KVH, SWA_HD) for c in swa_new)
    return (yp.reshape(batch, seq, d), ys.reshape(dec_batch, dec_seq, d), gla_p,
            jnp.stack(kp_l), jnp.stack(vp_l), jnp.stack(mk_l), jnp.stack(mv_l), gla_s, k_s, v_s)
```

```python
import functools

import jax
import jax.numpy as jnp
from jax import lax
from jax.experimental import pallas as pl
from jax.experimental.pallas import tpu as pltpu

F32 = jnp.float32
BF16 = jnp.bfloat16

D_MODEL = 1024
GLA_H = 4
GLA_DK = 128
GLA_DV = 256
GLA_QK = GLA_H * GLA_DK
GLA_V = GLA_H * GLA_DV
GLA_RANK = 16
GLA_TAU = 16.0
GLA_CHUNK = 32
SWA_HD = 64
SWA_H = 16
SWA_KVH = 2
SWA_G = SWA_H // SWA_KVH
SWA_W = SWA_H * SWA_HD
SWA_KV = SWA_KVH * SWA_HD
WINDOW = 128
MEM_LEN = 256
X_H = 4
X_HD = 256
X_W = X_H * X_HD
N_BRANCH = 3
IN_SPLITS = (GLA_QK, GLA_QK, GLA_V, GLA_V, GLA_RANK, SWA_W, SWA_KV, SWA_KV, SWA_W, X_W, X_W,
             N_BRANCH * D_MODEL)
EPS = 1e-6

LANES = 128
BF16_ROWS = 16
VMEM_LIMIT = 56 * 1024 * 1024

Z_GQ = 0
Z_GK = Z_GQ + GLA_QK
Z_GV = Z_GK + GLA_QK
Z_GZ = Z_GV + GLA_V
Z_SQ = Z_GZ + GLA_V
Z_SZ = Z_SQ + SWA_W
Z_MQ = Z_SZ + SWA_W
Z_MZ = Z_MQ + X_W
Z_MG = Z_MZ + X_W
Z_GA = Z_MG + N_BRANCH * D_MODEL
Z_COLS = Z_GA + LANES
KV_COLS = 2 * SWA_KV


def _row_tile(rows):
    return 512 if rows >= 4096 else 256


def _params(*semantics):
    return pltpu.CompilerParams(dimension_semantics=semantics, vmem_limit_bytes=VMEM_LIMIT)


def _sigmoid(x):
    return 1.0 / (1.0 + jnp.exp(-x))


def _log_sigmoid(x):
    return jnp.minimum(x, 0.0) - jnp.log(1.0 + jnp.exp(-jnp.abs(x)))


def _rmsnorm_rows(x, g):
    return x * lax.rsqrt(jnp.mean(x * x, axis=-1, keepdims=True) + EPS) * g


def _dot(a, b):
    return jnp.dot(a, b, preferred_element_type=F32)


def _dot_nt(a, b):
    return lax.dot_general(a, b, (((1,), (1,)), ((), ())), preferred_element_type=F32)


def _dot_tn(a, b):
    return lax.dot_general(a, b, (((0,), (0,)), ((), ())), preferred_element_type=F32)


def _norm_matmul_kernel(x_ref, g_ref, w_ref, *out_refs, plan):
    h = _rmsnorm_rows(x_ref[...], g_ref[...]).astype(BF16)
    for out_idx, out_c0, w_c0, width in plan:
        o_ref = out_refs[out_idx]
        o_ref[:, out_c0:out_c0 + width] = _dot(h, w_ref[:, w_c0:w_c0 + width]).astype(o_ref.dtype)


def _norm_matmul(x2d, g, w, outs, plan, tm, name):
    t, d = x2d.shape
    n = w.shape[1]
    return pl.pallas_call(
        functools.partial(_norm_matmul_kernel, plan=tuple(plan)),
        grid=(t // tm,),
        in_specs=[
            pl.BlockSpec((tm, d), lambda i: (i, 0)),
            pl.BlockSpec((1, d), lambda i: (0, 0)),
            pl.BlockSpec((d, n), lambda i: (0, 0), pipeline_mode=pl.Buffered(1)),
        ],
        out_specs=[pl.BlockSpec((tm, c), lambda i: (i, 0)) for c, _ in outs],
        out_shape=[jax.ShapeDtypeStruct((t, c), dt) for c, dt in outs],
        compiler_params=_params("parallel"),
        name=name,
    )(x2d, g, w)


def _chunk_plan(out_idx, out_c0, w_c0, width, chunk):
    plan = []
    off = 0
    while off < width:
        step = min(chunk, width - off)
        plan.append((out_idx, out_c0 + off, w_c0 + off, step))
        off += step
    return plan


def _in_proj(x2d, g, w):
    plan = _chunk_plan(0, 0, 0, Z_COLS, 512) + _chunk_plan(1, 0, Z_COLS, KV_COLS, KV_COLS)
    return _norm_matmul(x2d, g, w, [(Z_COLS, BF16), (KV_COLS, F32)], plan,
                        tm=_row_tile(x2d.shape[0]), name="in_proj")


def _chunk_cumsum(g, chunk):
    pos = lax.broadcasted_iota(jnp.int32, g.shape, 0) & (chunk - 1)
    b = g
    shift = 1
    while shift < chunk:
        b = b + jnp.where(pos >= shift, pltpu.roll(b, shift, 0), 0.0)
        shift *= 2
    return b


def _pad_rows(x, rows):
    if x.shape[0] >= rows:
        return x
    return jnp.concatenate([x, jnp.zeros((rows - x.shape[0], x.shape[1]), x.dtype)], axis=0)


def _gla_kernel(q_ref, k_ref, v_ref, gz_ref, ga_ref, wa_ref, ba_ref, gout_ref, *rest, chunk,
                nchunks, carry, aliased, anchored):
    refs = list(rest)
    s0_ref = None if carry else refs.pop(0)
    if aliased:
        refs.pop(0)
    o_ref, s_ref = refs
    rows = chunk * nchunks
    mm_rows = max(chunk, BF16_ROWS)

    def chunk_row(x, row):
        picked = [x[j * chunk + row:j * chunk + row + 1, :] for j in range(nchunks)]
        return picked, jnp.concatenate([jnp.broadcast_to(r, (chunk, x.shape[1])) for r in picked],
                                       axis=0)

    x = _dot(ga_ref[...], wa_ref[...]) + ba_ref[...]
    g = _log_sigmoid(x) * (1.0 / GLA_TAU)
    b = _chunk_cumsum(g, chunk)
    b_last, b_last_full = chunk_row(b, chunk - 1)
    b_last_rows = jnp.concatenate(b_last + [jnp.zeros((LANES - nchunks, GLA_QK), F32)], axis=0)

    q = q_ref[...].astype(F32) * (GLA_DK ** -0.5)
    k = k_ref[...].astype(F32)
    qb = q * jnp.exp(b)
    kd = k * jnp.exp(b_last_full - b)
    qb_bf, kd_bf = qb.astype(BF16), kd.astype(BF16)
    if anchored:
        _, b_mid_full = chunk_row(b, chunk // 2 - 1)
        qa_bf = (q * jnp.exp(b - b_mid_full)).astype(BF16)
        ka_bf = (k * jnp.exp(b_mid_full - b)).astype(BF16)
    else:
        qa_bf = qb_bf
        ka_bf = (k * jnp.exp(-b)).astype(BF16)

    r_idx = lax.broadcasted_iota(jnp.int32, (rows, rows), 0)
    c_idx = lax.broadcasted_iota(jnp.int32, (rows, rows), 1)
    diff = r_idx - c_idx
    mask = (diff >= 0) & (diff <= (r_idx & (chunk - 1)))

    def chunk_rows(x_f32, x_bf, sl, lanes):
        if chunk >= BF16_ROWS:
            return x_bf[sl, lanes]
        return _pad_rows(x_f32[sl, lanes], mm_rows).astype(BF16)

    if carry:
        @pl.when(pl.program_id(1) == 0)
        def _():
            s_ref[...] = jnp.zeros_like(s_ref)

    heads = range(GLA_H)
    lanes = [slice(h * GLA_DK, (h + 1) * GLA_DK) for h in heads]
    vl = [slice(h * GLA_DV, (h + 1) * GLA_DV) for h in heads]
    v_bf = v_ref[...]
    v_f32 = None if chunk >= BF16_ROWS else v_bf.astype(F32)
    o_intra = [_dot(jnp.where(mask, _dot_nt(qa_bf[:, lanes[h]], ka_bf[:, lanes[h]]),
                              0.0).astype(BF16), v_bf[:, vl[h]]) for h in heads]
    dec_cols = [jnp.exp(b_last_rows[:, lanes[h]].T) for h in heads]
    states = [s_ref[h] for h in heads] if carry else None

    o_inter = [[] for _ in heads]
    for j in range(nchunks):
        sl = slice(j * chunk, (j + 1) * chunk)
        for h in heads:
            state = states[h] if carry else s0_ref[j, h]
            o_inter[h].append(
                _dot(chunk_rows(qb, qb_bf, sl, lanes[h]), state.astype(BF16))[:chunk])
            v_j = (v_bf[sl, vl[h]] if chunk >= BF16_ROWS
                   else _pad_rows(v_f32[sl, vl[h]], mm_rows).astype(BF16))
            new_state = (dec_cols[h][:, j:j + 1] * state
                         + _dot_tn(chunk_rows(kd, kd_bf, sl, lanes[h]), v_j))
            if carry:
                states[h] = new_state
            else:
                s_ref[j, h] = new_state

    gz = gz_ref[...].astype(F32)
    gate = gz * _sigmoid(gz)
    for h in heads:
        if carry:
            s_ref[h] = states[h]
        o = o_intra[h] + jnp.concatenate(o_inter[h], axis=0)
        o_ref[:, vl[h]] = (_rmsnorm_rows(o, gout_ref[...]) * gate[:, vl[h]]).astype(BF16)


def _gla_call(z, wa, ba, gout, s0, prev, layer, depth, batch, rows, nblk, chunk, group):
    carry = s0 is None
    if carry:
        grid = (batch, nblk)
        row_blk = lambda b, n: b * nblk + n
        state_blk = (None, None, GLA_H, GLA_DK, GLA_DV)
        state_map = lambda b, n: (layer, b, 0, 0, 0)
        semantics = ("parallel", "arbitrary")
    else:
        grid = (batch // group,)
        row_blk = lambda i: i
        state_blk = (None, group, GLA_H, GLA_DK, GLA_DV)
        state_map = lambda i: (layer, i, 0, 0, 0)
        semantics = ("parallel",)
    zspec = lambda width, col0: pl.BlockSpec((rows, width),
                                             lambda *ids: (row_blk(*ids), col0 // width))
    const = lambda shape: pl.BlockSpec(shape, lambda *ids: (0,) * len(shape))
    in_specs = [zspec(GLA_QK, Z_GQ), zspec(GLA_QK, Z_GK), zspec(GLA_V, Z_GV), zspec(GLA_V, Z_GZ),
                zspec(LANES, Z_GA), const((LANES, GLA_QK)), const((1, GLA_QK)), const((1, GLA_DV))]
    args = [z, z, z, z, z, wa, ba, gout]
    if not carry:
        in_specs.append(pl.BlockSpec(state_blk, state_map))
        args.append(s0)
    aliases = {}
    if prev is not None:
        aliases = {len(args): 1}
        in_specs.append(pl.BlockSpec(memory_space=pl.ANY))
        args.append(prev)
    return pl.pallas_call(
        functools.partial(_gla_kernel, chunk=chunk, nchunks=rows // chunk, carry=carry,
                          aliased=prev is not None, anchored=carry),
        grid=grid,
        in_specs=in_specs,
        out_specs=[pl.BlockSpec((rows, GLA_V), lambda *ids: (row_blk(*ids), 0)),
                   pl.BlockSpec(state_blk, state_map)],
        out_shape=[jax.ShapeDtypeStruct((z.shape[0], GLA_V), BF16),
                   jax.ShapeDtypeStruct((depth, batch, GLA_H, GLA_DK, GLA_DV), F32)],
        input_output_aliases=aliases,
        compiler_params=_params(*semantics),
        name="gla_prompt" if carry else "gla_sample",
    )(*args)


def _swa_head_of_block(kv_head, blk):
    pairs = SWA_G // 2
    within = 2 * blk if blk < pairs else 2 * (blk - pairs) + 1
    return kv_head * SWA_G + within


def _swa_attend(sink_ref, q, kk, vv, valid, nq):
    nk = kk.shape[0]
    pairs = SWA_G // 2
    lo_q = lax.broadcasted_iota(jnp.int32, (nq, LANES), 1) < SWA_HD
    lo_k = lax.broadcasted_iota(jnp.int32, (nk, LANES), 1) < SWA_HD
    kk_sw = pltpu.roll(kk, SWA_HD, 1)
    vv_sw = pltpu.roll(vv, SWA_HD, 1)
    outs = []
    for kh in range(SWA_KVH):
        k_this, k_other = (kk, kk_sw) if kh == 0 else (kk_sw, kk)
        v_this, v_other = (vv, vv_sw) if kh == 0 else (vv_sw, vv)
        k_dup = jnp.where(lo_k, k_this, k_other).astype(BF16)
        v_even = jnp.where(lo_k, v_this, 0.0).astype(BF16)
        v_odd = jnp.where(lo_k, 0.0, v_other).astype(BF16)
        even, odd = [], []
        for p in range(pairs):
            c0 = kh * SWA_G * SWA_HD + p * LANES
            q_pair = q[:, c0:c0 + LANES]
            even.append(jnp.where(lo_q, q_pair, jnp.zeros_like(q_pair)))
            odd.append(jnp.where(lo_q, jnp.zeros_like(q_pair), q_pair))
        lhs = jnp.concatenate(even + odd, axis=0).astype(BF16)
        s = _dot_nt(lhs, k_dup) * (SWA_HD ** -0.5)
        probs, rinv = [], []
        for blk in range(SWA_G):
            sink = sink_ref[_swa_head_of_block(kh, blk)]
            s_h = jnp.where(valid, s[blk * nq:(blk + 1) * nq], -jnp.inf)
            m = jnp.maximum(jnp.max(s_h, axis=-1, keepdims=True), sink)
            p_h = jnp.exp(s_h - m)
            den = jnp.sum(p_h, axis=-1, keepdims=True) + jnp.exp(sink - m)
            probs.append(p_h)
            rinv.append(1.0 / den)
        p_even = jnp.concatenate(probs[:pairs], axis=0).astype(BF16)
        p_odd = jnp.concatenate(probs[pairs:], axis=0).astype(BF16)
        o = _dot(p_even, v_even) + _dot(p_odd, v_odd)
        for p in range(pairs):
            scale = jnp.where(lo_q, rinv[p], rinv[pairs + p])
            outs.append(o[p * nq:(p + 1) * nq] * scale)
    return jnp.concatenate(outs, axis=1)


def _swa_prompt_kernel(sink_ref, q_ref, sz_ref, kvc_ref, kvp_ref, o_ref):
    n = pl.program_id(1)
    kv = jnp.concatenate([kvp_ref[...], kvc_ref[...]], axis=0)
    i = lax.broadcasted_iota(jnp.int32, (WINDOW, 2 * WINDOW), 0)
    j = lax.broadcasted_iota(jnp.int32, (WINDOW, 2 * WINDOW), 1)
    diff = i + WINDOW - j
    first_key = jnp.where(n > 0, 0, WINDOW)
    valid = (diff >= 0) & (diff <= WINDOW) & (j >= first_key)
    o = _swa_attend(sink_ref, q_ref[...], kv[:, :SWA_KV], kv[:, SWA_KV:], valid, WINDOW)
    sz = sz_ref[...].astype(F32)
    o_ref[...] = (o * (sz * _sigmoid(sz))).astype(BF16)


def _swa_prompt(z, kv, sink, batch, seq):
    nblk = seq // WINDOW
    return pl.pallas_call(
        _swa_prompt_kernel,
        grid=(batch, nblk),
        in_specs=[
            pl.BlockSpec(memory_space=pltpu.SMEM),
            pl.BlockSpec((WINDOW, SWA_W), lambda b, n: (b * nblk + n, Z_SQ // SWA_W)),
            pl.BlockSpec((WINDOW, SWA_W), lambda b, n: (b * nblk + n, Z_SZ // SWA_W)),
            pl.BlockSpec((WINDOW, KV_COLS), lambda b, n: (b * nblk + n, 0)),
            pl.BlockSpec((WINDOW, KV_COLS), lambda b, n: (b * nblk + jnp.maximum(n - 1, 0), 0)),
        ],
        out_specs=pl.BlockSpec((WINDOW, SWA_W), lambda b, n: (b * nblk + n, 0)),
        out_shape=jax.ShapeDtypeStruct((batch * seq, SWA_W), BF16),
        compiler_params=_params("parallel", "parallel"),
        name="swa_prompt",
    )(sink, z, z, kv, kv)


def _swa_sample_kernel(sink_ref, q_ref, sz_ref, kvn_ref, ck_ref, cv_ref, *rest, group, seq):
    o_ref, nk_ref, nv_ref = rest[-3:]
    wb = ck_ref.shape[1]
    nk = 2 * wb
    i = lax.broadcasted_iota(jnp.int32, (seq, nk), 0)
    j = lax.broadcasted_iota(jnp.int32, (seq, nk), 1)
    diff = i + wb - j
    valid = (diff >= 0) & (diff <= WINDOW)
    q = q_ref[...].astype(F32)
    kvn = kvn_ref[...]
    pad = jnp.zeros((nk - wb - seq, SWA_KV), F32)
    outs = []
    for e in range(group):
        rows = slice(e * seq, (e + 1) * seq)
        k_old, v_old = ck_ref[e], cv_ref[e]
        k_new, v_new = kvn[rows, :SWA_KV], kvn[rows, SWA_KV:]
        nk_ref[e] = jnp.concatenate([k_old[seq:], k_new], axis=0)
        nv_ref[e] = jnp.concatenate([v_old[seq:], v_new], axis=0)
        kk = jnp.concatenate([k_old, k_new, pad], axis=0)
        vv = jnp.concatenate([v_old, v_new, pad], axis=0)
        outs.append(_swa_attend(sink_ref, q[rows], kk, vv, valid, seq))
    sz = sz_ref[...].astype(F32)
    o_ref[...] = (jnp.concatenate(outs, axis=0) * (sz * _sigmoid(sz))).astype(BF16)


def _swa_sample(z, kv, sink, ck, cv, prev, layer, batch, seq):
    group = 8
    rows = group * seq
    wb = ck.shape[2]
    cache_spec = pl.BlockSpec((None, group, wb, SWA_KV), lambda i: (layer, i, 0, 0))
    in_specs = [
        pl.BlockSpec(memory_space=pltpu.SMEM),
        pl.BlockSpec((rows, SWA_W), lambda i: (i, Z_SQ // SWA_W)),
        pl.BlockSpec((rows, SWA_W), lambda i: (i, Z_SZ // SWA_W)),
        pl.BlockSpec((rows, KV_COLS), lambda i: (i, 0)),
        cache_spec, cache_spec,
    ]
    args = [sink, z, z, kv, ck, cv]
    aliases = {}
    if prev is not None:
        aliases = {len(args): 1, len(args) + 1: 2}
        in_specs += [pl.BlockSpec(memory_space=pl.ANY)] * 2
        args += list(prev)
    return pl.pallas_call(
        functools.partial(_swa_sample_kernel, group=group, seq=seq),
        grid=(batch // group,),
        in_specs=in_specs,
        out_specs=[pl.BlockSpec((rows, SWA_W), lambda i: (i, 0)), cache_spec, cache_spec],
        out_shape=[
            jax.ShapeDtypeStruct((batch * seq, SWA_W), BF16),
            jax.ShapeDtypeStruct(ck.shape, F32),
            jax.ShapeDtypeStruct(cv.shape, F32),
        ],
        input_output_aliases=aliases,
        compiler_params=_params("parallel"),
        name="swa_sample",
    )(*args)


def _mem_attend_head(q_h, k_h, v_h):
    s = _dot_nt(q_h, k_h) * (X_HD ** -0.5)
    p = jnp.exp(s - jnp.max(s, axis=-1, keepdims=True))
    den = jnp.sum(p, axis=-1, keepdims=True)
    return _dot(p.astype(BF16), v_h) * (1.0 / den)


MEM_HALVES = X_HD // LANES
MEM_TILE_ROWS = X_H * MEM_HALVES


def _mem_rows_view(x):
    lead = x.shape[:-3]
    n = len(lead)
    x = x.reshape(*lead, MEM_LEN, X_H, MEM_HALVES, LANES)
    x = x.transpose(*range(n), n, n + 2, n + 1, n + 3)
    return x.reshape(*lead, MEM_LEN * MEM_TILE_ROWS, LANES)


def _mem_from_rows_view(x):
    lead = x.shape[:-2]
    n = len(lead)
    x = x.reshape(*lead, MEM_LEN, MEM_HALVES, X_H, LANES)
    x = x.transpose(*range(n), n, n + 2, n + 1, n + 3)
    return x.reshape(*lead, MEM_LEN, X_H, X_HD)


def _mem_head_rows(head, half):
    return pl.ds(half * X_H + head, MEM_LEN, stride=MEM_TILE_ROWS)


def _mem_kv_kernel(x_ref, g_ref, w_ref, *rest):
    kb_ref, vb_ref, kn_ref, vn_ref = rest[-4:]
    h = _rmsnorm_rows(x_ref[...], g_ref[...]).astype(BF16)
    for t, (b_ref, n_ref) in enumerate(((kb_ref, kn_ref), (vb_ref, vn_ref))):
        for head in range(X_H):
            c0 = head * X_HD
            res = _dot(h, w_ref[:, t * X_W + c0:t * X_W + c0 + X_HD])
            b_ref[:, c0:c0 + X_HD] = res.astype(BF16)
            for half in range(MEM_HALVES):
                n_ref[_mem_head_rows(head, half), :] = res[:, half * LANES:(half + 1) * LANES]


def _mem_kv(mem2d, g, w, prev, layer, depth, batch):
    d = mem2d.shape[1]
    rows_spec = pl.BlockSpec((None, None, MEM_LEN * MEM_TILE_ROWS, LANES), lambda i: (layer, i, 0, 0))
    rows_shape = jax.ShapeDtypeStruct((depth, batch, MEM_LEN * MEM_TILE_ROWS, LANES), F32)
    dense_spec = pl.BlockSpec((MEM_LEN, X_W), lambda i: (i, 0))
    dense_shape = jax.ShapeDtypeStruct((batch * MEM_LEN, X_W), BF16)
    in_specs = [pl.BlockSpec((MEM_LEN, d), lambda i: (i, 0)),
                pl.BlockSpec((1, d), lambda i: (0, 0)),
                pl.BlockSpec((d, 2 * X_W), lambda i: (0, 0))]
    args = [mem2d, g, w]
    aliases = {}
    if prev is not None:
        aliases = {len(args): 2, len(args) + 1: 3}
        in_specs += [pl.BlockSpec(memory_space=pl.ANY)] * 2
        args += list(prev)
    return pl.pallas_call(
        _mem_kv_kernel,
        grid=(batch,),
        in_specs=in_specs,
        out_specs=[dense_spec, dense_spec, rows_spec, rows_spec],
        out_shape=[dense_shape, dense_shape, rows_shape, rows_shape],
        input_output_aliases=aliases,
        compiler_params=_params("parallel"),
        name="mem_kv",
    )(*args)


def _mem_prompt_kernel(q_ref, mz_ref, mk_ref, mv_ref, o_ref):
    outs = []
    for h in range(X_H):
        cols = slice(h * X_HD, (h + 1) * X_HD)
        outs.append(_mem_attend_head(q_ref[:, cols], mk_ref[:, cols], mv_ref[:, cols]))
    mz = mz_ref[...].astype(F32)
    o_ref[...] = (jnp.concatenate(outs, axis=1) * (mz * _sigmoid(mz))).astype(BF16)


def _mem_prompt(z, mk, mv, batch, seq):
    tm = 256
    nblk = seq // tm
    mem_spec = pl.BlockSpec((MEM_LEN, X_W), lambda b, n: (b, 0))
    return pl.pallas_call(
        _mem_prompt_kernel,
        grid=(batch, nblk),
        in_specs=[
            pl.BlockSpec((tm, X_W), lambda b, n: (b * nblk + n, Z_MQ // X_W)),
            pl.BlockSpec((tm, X_W), lambda b, n: (b * nblk + n, Z_MZ // X_W)),
            mem_spec, mem_spec,
        ],
        out_specs=pl.BlockSpec((tm, X_W), lambda b, n: (b * nblk + n, 0)),
        out_shape=jax.ShapeDtypeStruct((batch * seq, X_W), BF16),
        compiler_params=_params("parallel", "parallel"),
        name="mem_prompt",
    )(z, z, mk, mv)


def _mem_sample_kernel(q_ref, mz_ref, mk_ref, mv_ref, o_ref, *, group, seq):
    def head_tile(ref, e, h):
        halves = [ref[e, _mem_head_rows(h, half), :] for half in range(MEM_HALVES)]
        return jnp.concatenate(halves, axis=1).astype(BF16)

    q = q_ref[...].astype(F32)
    pairs = [(e, h) for e in range(group) for h in range(X_H)]
    scores = []
    for e, h in pairs:
        q_eh = _pad_rows(q[e * seq:(e + 1) * seq, h * X_HD:(h + 1) * X_HD], BF16_ROWS).astype(BF16)
        scores.append(_dot_nt(q_eh, head_tile(mk_ref, e, h))[:seq])
    s = jnp.concatenate(scores, axis=0) * (X_HD ** -0.5)
    p = jnp.exp(s - jnp.max(s, axis=-1, keepdims=True))
    rinv = 1.0 / jnp.sum(p, axis=-1, keepdims=True)
    outs = [[] for _ in range(group)]
    for i, (e, h) in enumerate(pairs):
        p_eh = _pad_rows(p[i * seq:(i + 1) * seq], BF16_ROWS).astype(BF16)
        outs[e].append(_dot(p_eh, head_tile(mv_ref, e, h))[:seq] * rinv[i * seq:(i + 1) * seq])
    o = jnp.concatenate([jnp.concatenate(row, axis=1) for row in outs], axis=0)
    mz = mz_ref[...].astype(F32)
    o_ref[...] = (o * (mz * _sigmoid(mz))).astype(BF16)


def _mem_sample(z, mk, mv, layer, batch, seq):
    group = 4
    rows = group * seq
    mem_spec = pl.BlockSpec((None, group, MEM_LEN * MEM_TILE_ROWS, LANES),
                            lambda i: (layer, i, 0, 0))
    return pl.pallas_call(
        functools.partial(_mem_sample_kernel, group=group, seq=seq),
        grid=(batch // group,),
        in_specs=[
            pl.BlockSpec((rows, X_W), lambda i: (i, Z_MQ // X_W)),
            pl.BlockSpec((rows, X_W), lambda i: (i, Z_MZ // X_W)),
            mem_spec, mem_spec,
        ],
        out_specs=pl.BlockSpec((rows, X_W), lambda i: (i, 0)),
        out_shape=jax.ShapeDtypeStruct((batch * seq, X_W), BF16),
        compiler_params=_params("parallel"),
        name="mem_sample",
    )(z, z, mk, mv)


def _merge_kernel(x_ref, a_ref, s_ref, m_ref, ga_ref, gs_ref, gm_ref, wa_ref, ws_ref, wm_ref,
                  wo_ref, *rest, final):
    if final:
        gf_ref, y_ref = rest
    else:
        (y_ref,) = rest
    merged = (_sigmoid(ga_ref[...].astype(F32)) * _dot(a_ref[...], wa_ref[...])
              + _sigmoid(gs_ref[...].astype(F32)) * _dot(s_ref[...], ws_ref[...])
              + _sigmoid(gm_ref[...].astype(F32)) * _dot(m_ref[...], wm_ref[...]))
    y = x_ref[...] + _dot(merged.astype(BF16), wo_ref[...])
    if final:
        y = _rmsnorm_rows(y, gf_ref[...])
    y_ref[...] = y


def _merge(x2d, o_gla, o_swa, o_mem, z, w_gla, w_swa, w_mem, w_out, g_final):
    t, d = x2d.shape
    tm = _row_tile(t)
    final = g_final is not None
    row = lambda width: pl.BlockSpec((tm, width), lambda i: (i, 0))
    gate = lambda k: pl.BlockSpec((tm, d), lambda i: (i, Z_MG // d + k))
    weight = pl.BlockSpec((d, d), lambda i: (0, 0))
    in_specs = [row(d), row(d), row(d), row(d), gate(0), gate(1), gate(2),
                weight, weight, weight, weight]
    args = [x2d, o_gla, o_swa, o_mem, z, z, z, w_gla, w_swa, w_mem, w_out]
    if final:
        in_specs.append(pl.BlockSpec((1, d), lambda i: (0, 0)))
        args.append(g_final)
    return pl.pallas_call(
        functools.partial(_merge_kernel, final=final),
        grid=(t // tm,),
        in_specs=in_specs,
        out_specs=row(d),
        out_shape=jax.ShapeDtypeStruct((t, d), F32),
        compiler_params=_params("parallel"),
        name="merge_final" if final else "merge",
    )(*args)


def _regroup_w_in(w_in):
    cuts, acc = [], 0
    for width in IN_SPLITS[:-1]:
        acc += width
        cuts.append(acc)
    gq, gk, gv, gz, ga, sq, sk, sv, sz, mq, mz, mg = jnp.split(w_in, cuts, axis=-1)
    ga = jnp.pad(ga, ((0, 0), (0, 0), (0, LANES - GLA_RANK)))
    return jnp.concatenate([gq, gk, gv, gz, sq, sz, mq, mz, mg, ga, sk, sv], axis=-1).astype(BF16)


def kernel(x_prompt, x_sample, mem_prompt, state_gla, cache_swa_k, cache_swa_v, cache_mem_k,
           cache_mem_v, g_norm, w_in, w_gla_a, b_gla_a, g_gla_out, swa_sink, g_mem, w_mem_kv,
           w_br_gla, w_br_swa, w_br_mem, w_out, g_final):
    depth = w_in.shape[0]
    batch, seq, d = x_prompt.shape
    dec_batch, dec_seq, _ = x_sample.shape
    wb = cache_swa_k.shape[2]

    w_in_r = _regroup_w_in(w_in)
    w_a = jnp.pad(w_gla_a, ((0, 0), (0, LANES - GLA_RANK), (0, 0))).astype(BF16)
    w_mem_b = w_mem_kv.astype(BF16)
    w_g, w_s, w_m, w_o = (w.astype(BF16) for w in (w_br_gla, w_br_swa, w_br_mem, w_out))
    g_fin = g_final.reshape(1, d)

    yp = x_prompt.reshape(batch * seq, d)
    ys = x_sample.reshape(dec_batch * dec_seq, d)
    mem2d = mem_prompt.reshape(batch * MEM_LEN, d)
    ck_all = cache_swa_k.reshape(depth, dec_batch, wb, SWA_KV)
    cv_all = cache_swa_v.reshape(depth, dec_batch, wb, SWA_KV)
    cmk_all = _mem_rows_view(cache_mem_k)
    cmv_all = _mem_rows_view(cache_mem_v)
    gla_p = gla_s = swa_new = mem_new = None
    prompt_chunk = 2 * GLA_CHUNK
    prompt_rows = 4 * prompt_chunk
    gla_group = 8
    kp_l, vp_l = [], []
    for l in range(depth):
        last = l == depth - 1
        gn = g_norm[l].reshape(1, d)
        ba = b_gla_a[l].reshape(1, GLA_QK)
        gout = g_gla_out[l].reshape(1, GLA_DV)
        sink = swa_sink[l]

        mk, mv, *mem_new = _mem_kv(mem2d, g_mem[l].reshape(1, d), w_mem_b[l], mem_new, l, depth,
                                   batch)

        z, kv = _in_proj(yp, gn, w_in_r[l])
        o_gla, gla_p = _gla_call(z, w_a[l], ba, gout, None, gla_p, l, depth, batch, prompt_rows,
                                 seq // prompt_rows, prompt_chunk, None)
        o_swa = _swa_prompt(z, kv, sink, batch, seq)
        o_mem = _mem_prompt(z, mk, mv, batch, seq)
        yp = _merge(yp, o_gla, o_swa, o_mem, z, w_g[l], w_s[l], w_m[l], w_o[l],
                    g_fin if last else None)
        kv3 = kv.reshape(batch, seq, KV_COLS)[:, seq - WINDOW:]
        kp_l.append(kv3[..., :SWA_KV].reshape(batch, WINDOW, SWA_KVH, SWA_HD))
        vp_l.append(kv3[..., SWA_KV:].reshape(batch, WINDOW, SWA_KVH, SWA_HD))

        z, kv = _in_proj(ys, gn, w_in_r[l])
        o_gla, gla_s = _gla_call(z, w_a[l], ba, gout, state_gla, gla_s, l, depth, dec_batch,
                                 gla_group * dec_seq, None, dec_seq, gla_group)
        o_swa, *swa_new = _swa_sample(z, kv, sink, ck_all, cv_all, swa_new, l, dec_batch, dec_seq)
        o_mem = _mem_sample(z, cmk_all, cmv_all, l, dec_batch, dec_seq)
        ys = _merge(ys, o_gla, o_swa, o_mem, z, w_g[l], w_s[l], w_m[l], w_o[l],
                    g_fin if last else None)

    k_s, v_s = (c.reshape(depth, dec_batch, wb, SWA_KVH, SWA_HD) for c in swa_new)
    mk_p, mv_p = (_mem_from_rows_view(c) for c in mem_new)
    return (yp.reshape(batch, seq, d), ys.reshape(dec_batch, dec_seq, d), gla_p,
            jnp.stack(kp_l), jnp.stack(vp_l), mk_p, mv_p, gla_s, k_s, v_s)
```

```python
import functools

import jax
import jax.numpy as jnp
from jax import lax
from jax.experimental import pallas as pl
from jax.experimental.pallas import tpu as pltpu

F32 = jnp.float32
BF16 = jnp.bfloat16

D_MODEL = 1024
GLA_H = 4
GLA_DK = 128
GLA_DV = 256
GLA_QK = GLA_H * GLA_DK
GLA_V = GLA_H * GLA_DV
GLA_RANK = 16
GLA_TAU = 16.0
GLA_CHUNK = 32
SWA_HD = 64
SWA_H = 16
SWA_KVH = 2
SWA_G = SWA_H // SWA_KVH
SWA_W = SWA_H * SWA_HD
SWA_KV = SWA_KVH * SWA_HD
WINDOW = 128
MEM_LEN = 256
X_H = 4
X_HD = 256
X_W = X_H * X_HD
N_BRANCH = 3
IN_SPLITS = (GLA_QK, GLA_QK, GLA_V, GLA_V, GLA_RANK, SWA_W, SWA_KV, SWA_KV, SWA_W, X_W, X_W,
             N_BRANCH * D_MODEL)
EPS = 1e-6

LANES = 128
BF16_ROWS = 16
MXU_COLS = 256
VMEM_LIMIT = 56 * 1024 * 1024

Z_GQ = 0
Z_GK = Z_GQ + GLA_QK
Z_GV = Z_GK + GLA_QK
Z_GZ = Z_GV + GLA_V
Z_SQ = Z_GZ + GLA_V
Z_SZ = Z_SQ + SWA_W
Z_MQ = Z_SZ + SWA_W
Z_MZ = Z_MQ + X_W
Z_MG = Z_MZ + X_W
Z_GA = Z_MG + N_BRANCH * D_MODEL
Z_COLS = Z_GA + LANES
KV_COLS = 2 * SWA_KV


def _row_tile(rows):
    return 512 if rows >= 4096 else 256


def _params(*semantics):
    return pltpu.CompilerParams(dimension_semantics=semantics, vmem_limit_bytes=VMEM_LIMIT)


def _sigmoid(x):
    return 1.0 / (1.0 + jnp.exp(-x))


def _silu(x):
    return x * _sigmoid(x)


def _log_sigmoid(x):
    return jnp.minimum(x, 0.0) - jnp.log(1.0 + jnp.exp(-jnp.abs(x)))


def _rmsnorm_rows(x, g):
    return x * lax.rsqrt(jnp.mean(x * x, axis=-1, keepdims=True) + EPS) * g


def _dot(a, b):
    return jnp.dot(a, b, preferred_element_type=F32)


def _dot_nt(a, b):
    return lax.dot_general(a, b, (((1,), (1,)), ((), ())), preferred_element_type=F32)


def _dot_tn(a, b):
    return lax.dot_general(a, b, (((0,), (0,)), ((), ())), preferred_element_type=F32)


def _pad_rows(x, rows):
    if x.shape[0] >= rows:
        return x
    return jnp.concatenate([x, jnp.zeros((rows - x.shape[0], x.shape[1]), x.dtype)], axis=0)


def _interleave(steps, chunks, run_chunk):
    total = sum(w for _, w, _ in steps)
    done, acc = 0, 0
    for thunk, weight, needs in steps:
        acc += weight
        upto = max(needs, min(len(chunks), -(-acc * len(chunks) // total)))
        for chunk in chunks[done:upto]:
            run_chunk(chunk)
        done = max(done, upto)
        thunk()


def _in_proj_kernel(x_ref, g_ref, w_ref, z_ref, kv_ref):
    h = _rmsnorm_rows(x_ref[...], g_ref[...]).astype(BF16)
    for c0 in range(0, Z_COLS, 512):
        width = min(512, Z_COLS - c0)
        z_ref[:, c0:c0 + width] = _dot(h, w_ref[:, c0:c0 + width]).astype(BF16)
    kv_ref[...] = _dot(h, w_ref[:, Z_COLS:])


def _in_proj(x2d, g, w):
    t, d = x2d.shape
    tm = _row_tile(t)
    return pl.pallas_call(
        _in_proj_kernel,
        grid=(t // tm,),
        in_specs=[
            pl.BlockSpec((tm, d), lambda i: (i, 0)),
            pl.BlockSpec((1, d), lambda i: (0, 0)),
            pl.BlockSpec(w.shape, lambda i: (0, 0), pipeline_mode=pl.Buffered(1)),
        ],
        out_specs=[pl.BlockSpec((tm, Z_COLS), lambda i: (i, 0)),
                   pl.BlockSpec((tm, KV_COLS), lambda i: (i, 0))],
        out_shape=[jax.ShapeDtypeStruct((t, Z_COLS), BF16), jax.ShapeDtypeStruct((t, KV_COLS), F32)],
        compiler_params=_params("parallel"),
        name="in_proj",
    )(x2d, g, w)


def _chunk_cumsum(g, chunk):
    pos = lax.broadcasted_iota(jnp.int32, g.shape, 0) & (chunk - 1)
    b = g
    shift = 1
    while shift < chunk:
        b = b + jnp.where(pos >= shift, pltpu.roll(b, shift, 0), 0.0)
        shift *= 2
    return b


def _gla_thunks(q_ref, k_ref, v_ref, gz_ref, ga_ref, wa_ref, ba_ref, gout_ref, s0_ref, o_ref,
                s_ref, *, chunk, nchunks, carry, anchored):
    rows = chunk * nchunks
    mm_rows = max(chunk, BF16_ROWS)
    heads = range(GLA_H)
    lanes = [slice(h * GLA_DK, (h + 1) * GLA_DK) for h in heads]
    vl = [slice(h * GLA_DV, (h + 1) * GLA_DV) for h in heads]
    per_head = lambda: [None] * GLA_H
    st = {name: per_head() for name in ("b", "qb", "kd", "qb_bf", "kd_bf", "qa_bf", "ka_bf",
                                        "dec_cols", "o_intra")}
    st["o_inter"] = [[] for _ in heads]

    def chunk_row(x, row):
        picked = [x[j * chunk + row:j * chunk + row + 1, :] for j in range(nchunks)]
        return picked, jnp.concatenate([jnp.broadcast_to(r, (chunk, x.shape[1])) for r in picked],
                                       axis=0)

    small = chunk < BF16_ROWS

    def chunk_rows(x_f32, x_bf, sl):
        return _pad_rows(x_f32[sl], mm_rows).astype(BF16) if small else x_bf[sl]

    def prepare():
        r_idx = lax.broadcasted_iota(jnp.int32, (rows, rows), 0)
        c_idx = lax.broadcasted_iota(jnp.int32, (rows, rows), 1)
        diff = r_idx - c_idx
        st["mask"] = (diff >= 0) & (diff <= (r_idx & (chunk - 1)))
        st["v_f32"] = v_ref[...].astype(F32) if small else None
        if carry:
            st["states"] = [s_ref[h] for h in heads]

    def decay(h):
        x = _dot(ga_ref[...], wa_ref[:, lanes[h]]) + ba_ref[:, lanes[h]]
        st["b"][h] = _chunk_cumsum(_log_sigmoid(x) * (1.0 / GLA_TAU), chunk)

    def factors(h):
        b = st["b"][h]
        b_last, b_last_full = chunk_row(b, chunk - 1)
        b_last_rows = jnp.concatenate(b_last + [jnp.zeros((LANES - nchunks, GLA_DK), F32)], axis=0)
        st["dec_cols"][h] = jnp.exp(b_last_rows.T)
        q = q_ref[:, lanes[h]].astype(F32) * (GLA_DK ** -0.5)
        k = k_ref[:, lanes[h]].astype(F32)
        qb = q * jnp.exp(b)
        kd = k * jnp.exp(b_last_full - b)
        st["qb"][h], st["kd"][h] = (qb, kd) if small else (None, None)
        st["qb_bf"][h], st["kd_bf"][h] = qb.astype(BF16), kd.astype(BF16)
        if anchored:
            _, b_mid_full = chunk_row(b, chunk // 2 - 1)
            st["qa_bf"][h] = (q * jnp.exp(b - b_mid_full)).astype(BF16)
            st["ka_bf"][h] = (k * jnp.exp(b_mid_full - b)).astype(BF16)
        else:
            st["qa_bf"][h] = st["qb_bf"][h]
            st["ka_bf"][h] = (k * jnp.exp(-b)).astype(BF16)

    def intra(h):
        a = jnp.where(st["mask"], _dot_nt(st["qa_bf"][h], st["ka_bf"][h]), 0.0)
        st["o_intra"][h] = _dot(a.astype(BF16), v_ref[:, vl[h]])

    def step(j):
        sl = slice(j * chunk, (j + 1) * chunk)
        for h in heads:
            state = st["states"][h] if carry else s0_ref[j, h]
            qb_j = chunk_rows(st["qb"][h], st["qb_bf"][h], sl)
            st["o_inter"][h].append(_dot(qb_j, state.astype(BF16))[:chunk])
            if small:
                v_j = _pad_rows(st["v_f32"][sl, vl[h]], mm_rows).astype(BF16)
            else:
                v_j = v_ref[sl, vl[h]]
            kd_j = chunk_rows(st["kd"][h], st["kd_bf"][h], sl)
            new_state = st["dec_cols"][h][:, j:j + 1] * state + _dot_tn(kd_j, v_j)
            if carry:
                st["states"][h] = new_state
            else:
                s_ref[j, h] = new_state

    def epilogue(h):
        if carry:
            s_ref[h] = st["states"][h]
        o = st["o_intra"][h] + jnp.concatenate(st["o_inter"][h], axis=0)
        gate = _silu(gz_ref[:, vl[h]].astype(F32))
        o_ref[:, vl[h]] = (_rmsnorm_rows(o, gout_ref[...]) * gate).astype(BF16)

    steps = [(prepare, 100)]
    steps += [(functools.partial(decay, h), 230) for h in heads]
    steps += [(functools.partial(factors, h), 170) for h in heads]
    steps += [(functools.partial(intra, h), 60) for h in heads]
    steps += [(functools.partial(step, j), 130) for j in range(nchunks)]
    steps += [(functools.partial(epilogue, h), 190) for h in heads]
    return steps


def _gla_sample_kernel(q_ref, k_ref, v_ref, gz_ref, ga_ref, wa_ref, ba_ref, gout_ref, s0_ref, *rest,
                       seq, group):
    o_ref, s_ref = rest[-2:]
    for thunk, _ in _gla_thunks(q_ref, k_ref, v_ref, gz_ref, ga_ref, wa_ref, ba_ref, gout_ref,
                                s0_ref, o_ref, s_ref, chunk=seq, nchunks=group, carry=False,
                                anchored=False):
        thunk()


def _gla_sample(z, wa, ba, gout, s0, prev, layer, batch, seq):
    group = 8
    rows = group * seq
    state_spec = pl.BlockSpec((None, group, GLA_H, GLA_DK, GLA_DV), lambda i: (layer, i, 0, 0, 0))
    zspec = lambda width, col0: pl.BlockSpec((rows, width), lambda i: (i, col0 // width))
    const = lambda shape: pl.BlockSpec(shape, lambda i: (0,) * len(shape))
    in_specs = [zspec(GLA_QK, Z_GQ), zspec(GLA_QK, Z_GK), zspec(GLA_V, Z_GV), zspec(GLA_V, Z_GZ),
                zspec(LANES, Z_GA), const((LANES, GLA_QK)), const((1, GLA_QK)), const((1, GLA_DV)),
                state_spec]
    args = [z, z, z, z, z, wa, ba, gout, s0]
    aliases = {}
    if prev is not None:
        aliases = {len(args): 1}
        in_specs.append(pl.BlockSpec(memory_space=pl.ANY))
        args.append(prev)
    return pl.pallas_call(
        functools.partial(_gla_sample_kernel, seq=seq, group=group),
        grid=(batch // group,),
        in_specs=in_specs,
        out_specs=[pl.BlockSpec((rows, GLA_V), lambda i: (i, 0)), state_spec],
        out_shape=[jax.ShapeDtypeStruct((batch * seq, GLA_V), BF16),
                   jax.ShapeDtypeStruct(s0.shape, F32)],
        input_output_aliases=aliases,
        compiler_params=_params("parallel"),
        name="gla_sample",
    )(*args)


def _swa_head_of_block(kv_head, blk):
    pairs = SWA_G // 2
    within = 2 * blk if blk < pairs else 2 * (blk - pairs) + 1
    return kv_head * SWA_G + within


def _swa_operands(kk, vv):
    lo = lax.broadcasted_iota(jnp.int32, kk.shape, 1) < SWA_HD
    kk_sw = pltpu.roll(kk, SWA_HD, 1)
    vv_sw = pltpu.roll(vv, SWA_HD, 1)
    ops = []
    for kh in range(SWA_KVH):
        k_this, k_other = (kk, kk_sw) if kh == 0 else (kk_sw, kk)
        v_this, v_other = (vv, vv_sw) if kh == 0 else (vv_sw, vv)
        ops.append((jnp.where(lo, k_this, k_other).astype(BF16),
                    jnp.where(lo, v_this, 0.0).astype(BF16),
                    jnp.where(lo, 0.0, v_other).astype(BF16)))
    return ops


def _swa_scores(q, kh, k_dup):
    nq = q.shape[0]
    lo = lax.broadcasted_iota(jnp.int32, (nq, LANES), 1) < SWA_HD
    even, odd = [], []
    for p in range(SWA_G // 2):
        c0 = kh * SWA_G * SWA_HD + p * LANES
        q_pair = q[:, c0:c0 + LANES]
        even.append(jnp.where(lo, q_pair, jnp.zeros_like(q_pair)))
        odd.append(jnp.where(lo, jnp.zeros_like(q_pair), q_pair))
    lhs = jnp.concatenate(even + odd, axis=0).astype(BF16)
    return _dot_nt(lhs, k_dup) * (SWA_HD ** -0.5)


def _swa_soft_head(s_h, valid, sink):
    s_h = jnp.where(valid, s_h, -jnp.inf)
    m = jnp.maximum(jnp.max(s_h, axis=-1, keepdims=True), sink)
    p_h = jnp.exp(s_h - m)
    den = jnp.sum(p_h, axis=-1, keepdims=True) + jnp.exp(sink - m)
    if p_h.shape[0] % BF16_ROWS == 0:
        p_h = p_h.astype(BF16)
    return p_h, 1.0 / den


def _swa_out(probs, rinv, v_even, v_odd):
    pairs = SWA_G // 2
    nq = probs[0].shape[0]
    lo = lax.broadcasted_iota(jnp.int32, (nq, LANES), 1) < SWA_HD
    p_even = jnp.concatenate(probs[:pairs], axis=0).astype(BF16)
    p_odd = jnp.concatenate(probs[pairs:], axis=0).astype(BF16)
    o = _dot(p_even, v_even) + _dot(p_odd, v_odd)
    outs = [o[p * nq:(p + 1) * nq] * jnp.where(lo, rinv[p], rinv[pairs + p]) for p in range(pairs)]
    return jnp.concatenate(outs, axis=1)


def _swa_sample_kernel(sink_ref, q_ref, sz_ref, kvn_ref, ck_ref, cv_ref, *rest, group, seq):
    o_ref, nk_ref, nv_ref = rest[-3:]
    wb = ck_ref.shape[1]
    nk = 2 * wb
    i = lax.broadcasted_iota(jnp.int32, (seq, nk), 0)
    j = lax.broadcasted_iota(jnp.int32, (seq, nk), 1)
    diff = i + wb - j
    valid = (diff >= 0) & (diff <= WINDOW)
    q = q_ref[...].astype(F32)
    kvn = kvn_ref[...]
    pad = jnp.zeros((nk - wb - seq, SWA_KV), F32)
    outs = []
    for e in range(group):
        rows = slice(e * seq, (e + 1) * seq)
        k_old, v_old = ck_ref[e], cv_ref[e]
        k_new, v_new = kvn[rows, :SWA_KV], kvn[rows, SWA_KV:]
        nk_ref[e] = jnp.concatenate([k_old[seq:], k_new], axis=0)
        nv_ref[e] = jnp.concatenate([v_old[seq:], v_new], axis=0)
        ops = _swa_operands(jnp.concatenate([k_old, k_new, pad], axis=0),
                            jnp.concatenate([v_old, v_new, pad], axis=0))
        heads_out = []
        for kh in range(SWA_KVH):
            k_dup, v_even, v_odd = ops[kh]
            s = _swa_scores(q[rows], kh, k_dup)
            soft = [_swa_soft_head(s[blk * seq:(blk + 1) * seq], valid,
                                   sink_ref[_swa_head_of_block(kh, blk)]) for blk in range(SWA_G)]
            heads_out.append(_swa_out([p for p, _ in soft], [r for _, r in soft], v_even, v_odd))
        outs.append(jnp.concatenate(heads_out, axis=1))
    o_ref[...] = (jnp.concatenate(outs, axis=0) * _silu(sz_ref[...].astype(F32))).astype(BF16)


def _swa_sample(z, kv, sink, ck, cv, prev, layer, batch, seq):
    group = 8
    rows = group * seq
    wb = ck.shape[2]
    cache_spec = pl.BlockSpec((None, group, wb, SWA_KV), lambda i: (layer, i, 0, 0))
    in_specs = [
        pl.BlockSpec(memory_space=pltpu.SMEM),
        pl.BlockSpec((rows, SWA_W), lambda i: (i, Z_SQ // SWA_W)),
        pl.BlockSpec((rows, SWA_W), lambda i: (i, Z_SZ // SWA_W)),
        pl.BlockSpec((rows, KV_COLS), lambda i: (i, 0)),
        cache_spec, cache_spec,
    ]
    args = [sink, z, z, kv, ck, cv]
    aliases = {}
    if prev is not None:
        aliases = {len(args): 1, len(args) + 1: 2}
        in_specs += [pl.BlockSpec(memory_space=pl.ANY)] * 2
        args += list(prev)
    return pl.pallas_call(
        functools.partial(_swa_sample_kernel, group=group, seq=seq),
        grid=(batch // group,),
        in_specs=in_specs,
        out_specs=[pl.BlockSpec((rows, SWA_W), lambda i: (i, 0)), cache_spec, cache_spec],
        out_shape=[
            jax.ShapeDtypeStruct((batch * seq, SWA_W), BF16),
            jax.ShapeDtypeStruct(ck.shape, F32),
            jax.ShapeDtypeStruct(cv.shape, F32),
        ],
        input_output_aliases=aliases,
        compiler_params=_params("parallel"),
        name="swa_sample",
    )(*args)


MEM_HALVES = X_HD // LANES
MEM_TILE_ROWS = X_H * MEM_HALVES


def _mem_rows_view(x):
    lead = x.shape[:-3]
    n = len(lead)
    x = x.reshape(*lead, MEM_LEN, X_H, MEM_HALVES, LANES)
    x = x.transpose(*range(n), n, n + 2, n + 1, n + 3)
    return x.reshape(*lead, MEM_LEN * MEM_TILE_ROWS, LANES)


def _mem_from_rows_view(x):
    lead = x.shape[:-2]
    n = len(lead)
    x = x.reshape(*lead, MEM_LEN, MEM_HALVES, X_H, LANES)
    x = x.transpose(*range(n), n, n + 2, n + 1, n + 3)
    return x.reshape(*lead, MEM_LEN, X_H, X_HD)


def _mem_head_rows(head, half):
    return pl.ds(half * X_H + head, MEM_LEN, stride=MEM_TILE_ROWS)


def _mem_kv_kernel(x_ref, g_ref, w_ref, *rest):
    kb_ref, vb_ref, kn_ref, vn_ref = rest[-4:]
    h = _rmsnorm_rows(x_ref[...], g_ref[...]).astype(BF16)
    for t, (b_ref, n_ref) in enumerate(((kb_ref, kn_ref), (vb_ref, vn_ref))):
        for head in range(X_H):
            c0 = head * X_HD
            res = _dot(h, w_ref[:, t * X_W + c0:t * X_W + c0 + X_HD])
            b_ref[:, c0:c0 + X_HD] = res.astype(BF16)
            for half in range(MEM_HALVES):
                n_ref[_mem_head_rows(head, half), :] = res[:, half * LANES:(half + 1) * LANES]


def _mem_kv(mem2d, g, w, prev, layer, depth, batch):
    d = mem2d.shape[1]
    rows_spec = pl.BlockSpec((None, None, MEM_LEN * MEM_TILE_ROWS, LANES), lambda i: (layer, i, 0, 0))
    rows_shape = jax.ShapeDtypeStruct((depth, batch, MEM_LEN * MEM_TILE_ROWS, LANES), F32)
    dense_spec = pl.BlockSpec((MEM_LEN, X_W), lambda i: (i, 0))
    dense_shape = jax.ShapeDtypeStruct((batch * MEM_LEN, X_W), BF16)
    in_specs = [pl.BlockSpec((MEM_LEN, d), lambda i: (i, 0)),
                pl.BlockSpec((1, d), lambda i: (0, 0)),
                pl.BlockSpec((d, 2 * X_W), lambda i: (0, 0))]
    args = [mem2d, g, w]
    aliases = {}
    if prev is not None:
        aliases = {len(args): 2, len(args) + 1: 3}
        in_specs += [pl.BlockSpec(memory_space=pl.ANY)] * 2
        args += list(prev)
    return pl.pallas_call(
        _mem_kv_kernel,
        grid=(batch,),
        in_specs=in_specs,
        out_specs=[dense_spec, dense_spec, rows_spec, rows_spec],
        out_shape=[dense_shape, dense_shape, rows_shape, rows_shape],
        input_output_aliases=aliases,
        compiler_params=_params("parallel"),
        name="mem_kv",
    )(*args)


def _mem_softmax(s):
    p = jnp.exp(s - jnp.max(s, axis=-1, keepdims=True))
    return p, 1.0 / jnp.sum(p, axis=-1, keepdims=True)


def _mem_sample_kernel(q_ref, mz_ref, mk_ref, mv_ref, o_ref, *, group, seq):
    def head_tile(ref, e, h):
        halves = [ref[e, _mem_head_rows(h, half), :] for half in range(MEM_HALVES)]
        return jnp.concatenate(halves, axis=1).astype(BF16)

    q = q_ref[...].astype(F32)
    pairs = [(e, h) for e in range(group) for h in range(X_H)]
    scores = []
    for e, h in pairs:
        q_eh = _pad_rows(q[e * seq:(e + 1) * seq, h * X_HD:(h + 1) * X_HD], BF16_ROWS).astype(BF16)
        scores.append(_dot_nt(q_eh, head_tile(mk_ref, e, h))[:seq])
    p, rinv = _mem_softmax(jnp.concatenate(scores, axis=0) * (X_HD ** -0.5))
    outs = [[] for _ in range(group)]
    for i, (e, h) in enumerate(pairs):
        p_eh = _pad_rows(p[i * seq:(i + 1) * seq], BF16_ROWS).astype(BF16)
        outs[e].append(_dot(p_eh, head_tile(mv_ref, e, h))[:seq] * rinv[i * seq:(i + 1) * seq])
    o = jnp.concatenate([jnp.concatenate(row, axis=1) for row in outs], axis=0)
    o_ref[...] = (o * _silu(mz_ref[...].astype(F32))).astype(BF16)


def _mem_sample(z, mk, mv, layer, batch, seq):
    group = 4
    rows = group * seq
    mem_spec = pl.BlockSpec((None, group, MEM_LEN * MEM_TILE_ROWS, LANES),
                            lambda i: (layer, i, 0, 0))
    return pl.pallas_call(
        functools.partial(_mem_sample_kernel, group=group, seq=seq),
        grid=(batch // group,),
        in_specs=[
            pl.BlockSpec((rows, X_W), lambda i: (i, Z_MQ // X_W)),
            pl.BlockSpec((rows, X_W), lambda i: (i, Z_MZ // X_W)),
            mem_spec, mem_spec,
        ],
        out_specs=pl.BlockSpec((rows, X_W), lambda i: (i, 0)),
        out_shape=jax.ShapeDtypeStruct((batch * seq, X_W), BF16),
        compiler_params=_params("parallel"),
        name="mem_sample",
    )(z, z, mk, mv)


FRONT_ROWS = 256
FRONT_GLA_CHUNK = 2 * GLA_CHUNK


def _front_kernel(sink_ref, x_ref, g_ref, w_ref, wa_ref, ba_ref, gout_ref, mk_ref, mv_ref, *rest):
    (ogla_ref, oswa_ref, omem_ref, mg_ref, kvl_ref, s_ref,
     gq_scr, gk_scr, gv_scr, gz_scr, ga_scr, sq_scr, sz_scr, mq_scr, mz_scr, kv_scr, kvp_scr) = rest[-17:]
    n = pl.program_id(1)
    tm = FRONT_ROWS

    @pl.when(n == 0)
    def _():
        s_ref[...] = jnp.zeros_like(s_ref)
        kvp_scr[...] = jnp.zeros_like(kvp_scr)

    h = _rmsnorm_rows(x_ref[...], g_ref[...]).astype(BF16)

    def project(chunk):
        dst, dst_c0, w_c0, width = chunk
        dst[:, dst_c0:dst_c0 + width] = _dot(h, w_ref[:, w_c0:w_c0 + width]).astype(dst.dtype)

    def chunks_of(dst, w_c0, width):
        return [(dst, c, w_c0 + c, min(MXU_COLS, width - c)) for c in range(0, width, MXU_COLS)]

    swa_cols = chunks_of(kv_scr, Z_COLS, KV_COLS) + chunks_of(sq_scr, Z_SQ, SWA_W) \
        + chunks_of(sz_scr, Z_SZ, SWA_W)
    gla_cols = (chunks_of(ga_scr, Z_GA, LANES) + chunks_of(gq_scr, Z_GQ, GLA_QK)
                + chunks_of(gk_scr, Z_GK, GLA_QK) + chunks_of(gv_scr, Z_GV, GLA_V)
                + chunks_of(gz_scr, Z_GZ, GLA_V))
    mem_cols = chunks_of(mq_scr, Z_MQ, X_W) + chunks_of(mz_scr, Z_MZ, X_W)
    merge_cols = chunks_of(mg_ref, Z_MG, N_BRANCH * D_MODEL)

    i = lax.broadcasted_iota(jnp.int32, (WINDOW, 2 * WINDOW), 0)
    j = lax.broadcasted_iota(jnp.int32, (WINDOW, 2 * WINDOW), 1)
    diff = i + WINDOW - j
    band = (diff >= 0) & (diff <= WINDOW)
    swa = {}

    def swa_start(blk):
        prev = kvp_scr[...] if blk == 0 else kv_scr[(blk - 1) * WINDOW:blk * WINDOW, :]
        kvb = jnp.concatenate([prev, kv_scr[blk * WINDOW:(blk + 1) * WINDOW, :]], axis=0)
        swa["ops"] = _swa_operands(kvb[:, :SWA_KV], kvb[:, SWA_KV:])
        swa["valid"] = band & (j >= jnp.where(n > 0, 0, WINDOW)) if blk == 0 else band

    def swa_scores(blk, kh):
        swa["s"] = _swa_scores(sq_scr[blk * WINDOW:(blk + 1) * WINDOW, :], kh, swa["ops"][kh][0])
        swa["soft"] = []

    def swa_soft(kh, hb):
        swa["soft"].append(_swa_soft_head(swa["s"][hb * WINDOW:(hb + 1) * WINDOW], swa["valid"],
                                          sink_ref[_swa_head_of_block(kh, hb)]))

    def swa_out(blk, kh):
        _, v_even, v_odd = swa["ops"][kh]
        o = _swa_out([p for p, _ in swa["soft"]], [r for _, r in swa["soft"]], v_even, v_odd)
        rows = slice(blk * WINDOW, (blk + 1) * WINDOW)
        cols = slice(kh * SWA_G * SWA_HD, (kh + 1) * SWA_G * SWA_HD)
        oswa_ref[rows, cols] = (o * _silu(sz_scr[rows, cols].astype(F32))).astype(BF16)

    swa_steps = []
    for blk in range(tm // WINDOW):
        swa_steps.append((functools.partial(swa_start, blk), 100))
        for kh in range(SWA_KVH):
            swa_steps.append((functools.partial(swa_scores, blk, kh), 60))
            swa_steps += [(functools.partial(swa_soft, kh, hb), 80) for hb in range(SWA_G)]
            swa_steps.append((functools.partial(swa_out, blk, kh), 150))

    mem = {}

    def mem_scores(hd):
        cols = slice(hd * X_HD, (hd + 1) * X_HD)
        p, rinv = _mem_softmax(_dot_nt(mq_scr[:, cols], mk_ref[:, cols]) * (X_HD ** -0.5))
        mem["p"], mem["rinv"] = p.astype(BF16), rinv

    def mem_out(hd):
        cols = slice(hd * X_HD, (hd + 1) * X_HD)
        o = _dot(mem["p"], mv_ref[:, cols]) * mem["rinv"]
        omem_ref[:, cols] = (o * _silu(mz_scr[:, cols].astype(F32))).astype(BF16)

    mem_steps = []
    for hd in range(X_H):
        mem_steps += [(functools.partial(mem_scores, hd), 130), (functools.partial(mem_out, hd), 160)]

    gla_steps = _gla_thunks(gq_scr, gk_scr, gv_scr, gz_scr, ga_scr, wa_ref, ba_ref, gout_ref, None,
                            ogla_ref, s_ref, chunk=FRONT_GLA_CHUNK, nchunks=tm // FRONT_GLA_CHUNK,
                            carry=True, anchored=True)

    for chunk in swa_cols:
        project(chunk)
    chunks = gla_cols + mem_cols + merge_cols
    steps = [(t, w, 0) for t, w in swa_steps]
    steps += [(t, w, len(gla_cols)) for t, w in gla_steps]
    steps += [(t, w, len(gla_cols) + len(mem_cols)) for t, w in mem_steps]
    _interleave(steps, chunks, project)

    kv_tail = kv_scr[tm - WINDOW:, :]
    kvp_scr[...] = kv_tail
    kvl_ref[...] = kv_tail


def _front(x2d, g, w, sink, wa, ba, gout, mk, mv, prev_state, layer, depth, batch, seq):
    t, d = x2d.shape
    tm = FRONT_ROWS
    nblk = seq // tm
    row = lambda width: pl.BlockSpec((tm, width), lambda b, n: (b * nblk + n, 0))
    const = lambda shape: pl.BlockSpec(shape, lambda b, n: (0,) * len(shape))
    mem_spec = pl.BlockSpec((MEM_LEN, X_W), lambda b, n: (b, 0))
    state_spec = pl.BlockSpec((None, None, GLA_H, GLA_DK, GLA_DV), lambda b, n: (layer, b, 0, 0, 0))
    in_specs = [pl.BlockSpec(memory_space=pltpu.SMEM), row(d), const((1, d)),
                pl.BlockSpec(w.shape, lambda b, n: (0, 0), pipeline_mode=pl.Buffered(1)),
                const((LANES, GLA_QK)), const((1, GLA_QK)), const((1, GLA_DV)), mem_spec, mem_spec]
    args = [sink, x2d, g, w, wa, ba, gout, mk, mv]
    aliases = {}
    if prev_state is not None:
        aliases = {len(args): 5}
        in_specs.append(pl.BlockSpec(memory_space=pl.ANY))
        args.append(prev_state)
    scratch = [pltpu.VMEM((tm, width), BF16) for width in
               (GLA_QK, GLA_QK, GLA_V, GLA_V, LANES, SWA_W, SWA_W, X_W, X_W)]
    scratch += [pltpu.VMEM((tm, KV_COLS), F32), pltpu.VMEM((WINDOW, KV_COLS), F32)]
    return pl.pallas_call(
        _front_kernel,
        grid=(batch, nblk),
        in_specs=in_specs,
        out_specs=[row(GLA_V), row(SWA_W), row(X_W), row(N_BRANCH * D_MODEL),
                   pl.BlockSpec((None, WINDOW, KV_COLS), lambda b, n: (b, 0, 0)), state_spec],
        out_shape=[jax.ShapeDtypeStruct((t, GLA_V), BF16), jax.ShapeDtypeStruct((t, SWA_W), BF16),
                   jax.ShapeDtypeStruct((t, X_W), BF16),
                   jax.ShapeDtypeStruct((t, N_BRANCH * D_MODEL), BF16),
                   jax.ShapeDtypeStruct((batch, WINDOW, KV_COLS), F32),
                   jax.ShapeDtypeStruct((depth, batch, GLA_H, GLA_DK, GLA_DV), F32)],
        scratch_shapes=scratch,
        input_output_aliases=aliases,
        compiler_params=_params("parallel", "arbitrary"),
        name="front",
    )(*args)


def _merge_kernel(x_ref, a_ref, s_ref, m_ref, ga_ref, gs_ref, gm_ref, wa_ref, ws_ref, wm_ref,
                  wo_ref, *rest, final):
    if final:
        gf_ref, y_ref = rest
    else:
        (y_ref,) = rest
    merged = (_sigmoid(ga_ref[...].astype(F32)) * _dot(a_ref[...], wa_ref[...])
              + _sigmoid(gs_ref[...].astype(F32)) * _dot(s_ref[...], ws_ref[...])
              + _sigmoid(gm_ref[...].astype(F32)) * _dot(m_ref[...], wm_ref[...]))
    y = x_ref[...] + _dot(merged.astype(BF16), wo_ref[...])
    if final:
        y = _rmsnorm_rows(y, gf_ref[...])
    y_ref[...] = y


def _merge(x2d, o_gla, o_swa, o_mem, gates, gate_col0, w_gla, w_swa, w_mem, w_out, g_final):
    t, d = x2d.shape
    tm = _row_tile(t)
    final = g_final is not None
    row = lambda width: pl.BlockSpec((tm, width), lambda i: (i, 0))
    gate = lambda k: pl.BlockSpec((tm, d), lambda i: (i, gate_col0 // d + k))
    weight = pl.BlockSpec((d, d), lambda i: (0, 0))
    in_specs = [row(d), row(d), row(d), row(d), gate(0), gate(1), gate(2),
                weight, weight, weight, weight]
    args = [x2d, o_gla, o_swa, o_mem, gates, gates, gates, w_gla, w_swa, w_mem, w_out]
    if final:
        in_specs.append(pl.BlockSpec((1, d), lambda i: (0, 0)))
        args.append(g_final)
    return pl.pallas_call(
        functools.partial(_merge_kernel, final=final),
        grid=(t // tm,),
        in_specs=in_specs,
        out_specs=row(d),
        out_shape=jax.ShapeDtypeStruct((t, d), F32),
        compiler_params=_params("parallel"),
        name="merge_final" if final else "merge",
    )(*args)


def _regroup_w_in(w_in):
    cuts, acc = [], 0
    for width in IN_SPLITS[:-1]:
        acc += width
        cuts.append(acc)
    gq, gk, gv, gz, ga, sq, sk, sv, sz, mq, mz, mg = jnp.split(w_in, cuts, axis=-1)
    ga = jnp.pad(ga, ((0, 0), (0, 0), (0, LANES - GLA_RANK)))
    return jnp.concatenate([gq, gk, gv, gz, sq, sz, mq, mz, mg, ga, sk, sv], axis=-1).astype(BF16)


def kernel(x_prompt, x_sample, mem_prompt, state_gla, cache_swa_k, cache_swa_v, cache_mem_k,
           cache_mem_v, g_norm, w_in, w_gla_a, b_gla_a, g_gla_out, swa_sink, g_mem, w_mem_kv,
           w_br_gla, w_br_swa, w_br_mem, w_out, g_final):
    depth = w_in.shape[0]
    batch, seq, d = x_prompt.shape
    dec_batch, dec_seq, _ = x_sample.shape
    wb = cache_swa_k.shape[2]

    w_in_r = _regroup_w_in(w_in)
    w_a = jnp.pad(w_gla_a, ((0, 0), (0, LANES - GLA_RANK), (0, 0))).astype(BF16)
    w_mem_b = w_mem_kv.astype(BF16)
    w_g, w_s, w_m, w_o = (w.astype(BF16) for w in (w_br_gla, w_br_swa, w_br_mem, w_out))
    g_fin = g_final.reshape(1, d)

    yp = x_prompt.reshape(batch * seq, d)
    ys = x_sample.reshape(dec_batch * dec_seq, d)
    mem2d = mem_prompt.reshape(batch * MEM_LEN, d)
    ck_all = cache_swa_k.reshape(depth, dec_batch, wb, SWA_KV)
    cv_all = cache_swa_v.reshape(depth, dec_batch, wb, SWA_KV)
    cmk_all = _mem_rows_view(cache_mem_k)
    cmv_all = _mem_rows_view(cache_mem_v)
    gla_p = gla_s = swa_new = mem_new = None
    kp_l, vp_l = [], []
    for l in range(depth):
        last = l == depth - 1
        gn = g_norm[l].reshape(1, d)
        ba = b_gla_a[l].reshape(1, GLA_QK)
        gout = g_gla_out[l].reshape(1, GLA_DV)
        sink = swa_sink[l]
        branch_w = (w_g[l], w_s[l], w_m[l], w_o[l], g_fin if last else None)

        mk, mv, *mem_new = _mem_kv(mem2d, g_mem[l].reshape(1, d), w_mem_b[l], mem_new, l, depth,
                                   batch)
        o_gla, o_swa, o_mem, gates, kv_tail, gla_p = _front(
            yp, gn, w_in_r[l], sink, w_a[l], ba, gout, mk, mv, gla_p, l, depth, batch, seq)
        yp = _merge(yp, o_gla, o_swa, o_mem, gates, 0, *branch_w)
        kp_l.append(kv_tail[..., :SWA_KV].reshape(batch, WINDOW, SWA_KVH, SWA_HD))
        vp_l.append(kv_tail[..., SWA_KV:].reshape(batch, WINDOW, SWA_KVH, SWA_HD))

        z, kv = _in_proj(ys, gn, w_in_r[l])
        o_gla, gla_s = _gla_sample(z, w_a[l], ba, gout, state_gla, gla_s, l, dec_batch, dec_seq)
        o_swa, *swa_new = _swa_sample(z, kv, sink, ck_all, cv_all, swa_new, l, dec_batch, dec_seq)
        o_mem = _mem_sample(z, cmk_all, cmv_all, l, dec_batch, dec_seq)
        ys = _merge(ys, o_gla, o_swa, o_mem, z, Z_MG, *branch_w)

    k_s, v_s = (c.reshape(depth, dec_batch, wb, SWA_KVH, SWA_HD) for c in swa_new)
    mk_p, mv_p = (_mem_from_rows_view(c) for c in mem_new)
    return (yp.reshape(batch, seq, d), ys.reshape(dec_batch, dec_seq, d), gla_p,
            jnp.stack(kp_l), jnp.stack(vp_l), mk_p, mv_p, gla_s, k_s, v_s)
```

```python
import functools

import jax
import jax.numpy as jnp
from jax import lax
from jax.experimental import pallas as pl
from jax.experimental.pallas import tpu as pltpu

F32 = jnp.float32
BF16 = jnp.bfloat16

D_MODEL = 1024
GLA_H = 4
GLA_DK = 128
GLA_DV = 256
GLA_QK = GLA_H * GLA_DK
GLA_V = GLA_H * GLA_DV
GLA_RANK = 16
GLA_TAU = 16.0
GLA_CHUNK = 32
SWA_HD = 64
SWA_H = 16
SWA_KVH = 2
SWA_G = SWA_H // SWA_KVH
SWA_W = SWA_H * SWA_HD
SWA_KV = SWA_KVH * SWA_HD
WINDOW = 128
MEM_LEN = 256
X_H = 4
X_HD = 256
X_W = X_H * X_HD
N_BRANCH = 3
IN_SPLITS = (GLA_QK, GLA_QK, GLA_V, GLA_V, GLA_RANK, SWA_W, SWA_KV, SWA_KV, SWA_W, X_W, X_W,
             N_BRANCH * D_MODEL)
EPS = 1e-6

LANES = 128
BF16_ROWS = 16
MXU_COLS = 256
VMEM_LIMIT = 56 * 1024 * 1024

Z_GQ = 0
Z_GK = Z_GQ + GLA_QK
Z_GV = Z_GK + GLA_QK
Z_GZ = Z_GV + GLA_V
Z_SQ = Z_GZ + GLA_V
Z_SZ = Z_SQ + SWA_W
Z_MQ = Z_SZ + SWA_W
Z_MZ = Z_MQ + X_W
Z_MG = Z_MZ + X_W
Z_GA = Z_MG + N_BRANCH * D_MODEL
Z_COLS = Z_GA + LANES
KV_COLS = 2 * SWA_KV

(W_GQ, W_GK, W_GV, W_GZ, W_GA, W_SQ, W_SK, W_SV, W_SZ, W_MQ, W_MZ, W_MG) = (
    sum(IN_SPLITS[:i]) for i in range(len(IN_SPLITS)))
Z_PIECES = ((Z_GQ, W_GQ, GLA_QK), (Z_GK, W_GK, GLA_QK), (Z_GV, W_GV, GLA_V), (Z_GZ, W_GZ, GLA_V),
            (Z_SQ, W_SQ, SWA_W), (Z_SZ, W_SZ, SWA_W), (Z_MQ, W_MQ, X_W), (Z_MZ, W_MZ, X_W),
            (Z_MG, W_MG, N_BRANCH * D_MODEL))


def _project(h, w_ref, row0, width):
    return _dot_nt(h, w_ref[row0:row0 + width, :])


def _project_decay(h, w_ref):
    res = _project(h, w_ref, W_GA, LANES)
    lane = lax.broadcasted_iota(jnp.int32, res.shape, 1)
    return jnp.where(lane < GLA_RANK, res, 0.0)


def _row_tile(rows):
    return 512 if rows >= 4096 else 256


def _params(*semantics):
    return pltpu.CompilerParams(dimension_semantics=semantics, vmem_limit_bytes=VMEM_LIMIT)


def _sigmoid(x):
    return 1.0 / (1.0 + jnp.exp(-x))


def _silu(x):
    return x * _sigmoid(x)


def _log_sigmoid(x):
    return jnp.minimum(x, 0.0) - jnp.log(1.0 + jnp.exp(-jnp.abs(x)))


def _rmsnorm_rows(x, g):
    return x * lax.rsqrt(jnp.mean(x * x, axis=-1, keepdims=True) + EPS) * g


def _dot(a, b):
    return jnp.dot(a, b, preferred_element_type=F32)


def _dot_nt(a, b):
    return lax.dot_general(a, b, (((1,), (1,)), ((), ())), preferred_element_type=F32)


def _dot_tn(a, b):
    return lax.dot_general(a, b, (((0,), (0,)), ((), ())), preferred_element_type=F32)


def _pad_rows(x, rows):
    if x.shape[0] >= rows:
        return x
    return jnp.concatenate([x, jnp.zeros((rows - x.shape[0], x.shape[1]), x.dtype)], axis=0)


def _interleave(steps, chunks, run_chunk):
    total = sum(w for _, w, _ in steps)
    done, acc = 0, 0
    for thunk, weight, needs in steps:
        acc += weight
        upto = max(needs, min(len(chunks), -(-acc * len(chunks) // total)))
        for chunk in chunks[done:upto]:
            run_chunk(chunk)
        done = max(done, upto)
        thunk()


def _in_proj_kernel(x_ref, g_ref, w_ref, z_ref, kv_ref):
    h = _rmsnorm_rows(x_ref[...], g_ref[...]).astype(BF16)
    for z0, w0, width in Z_PIECES:
        for c in range(0, width, 512):
            z_ref[:, z0 + c:z0 + c + 512] = _project(h, w_ref, w0 + c, 512).astype(BF16)
    z_ref[:, Z_GA:Z_COLS] = _project_decay(h, w_ref).astype(BF16)
    kv_ref[...] = _project(h, w_ref, W_SK, KV_COLS)


def _in_proj(x2d, g, w):
    t, d = x2d.shape
    tm = _row_tile(t)
    return pl.pallas_call(
        _in_proj_kernel,
        grid=(t // tm,),
        in_specs=[
            pl.BlockSpec((tm, d), lambda i: (i, 0)),
            pl.BlockSpec((1, d), lambda i: (0, 0)),
            pl.BlockSpec(w.shape, lambda i: (0, 0), pipeline_mode=pl.Buffered(1)),
        ],
        out_specs=[pl.BlockSpec((tm, Z_COLS), lambda i: (i, 0)),
                   pl.BlockSpec((tm, KV_COLS), lambda i: (i, 0))],
        out_shape=[jax.ShapeDtypeStruct((t, Z_COLS), BF16), jax.ShapeDtypeStruct((t, KV_COLS), F32)],
        compiler_params=_params("parallel"),
        name="in_proj",
    )(x2d, g, w)


def _chunk_cumsum(g, chunk):
    pos = lax.broadcasted_iota(jnp.int32, g.shape, 0) & (chunk - 1)
    b = g
    shift = 1
    while shift < chunk:
        b = b + jnp.where(pos >= shift, pltpu.roll(b, shift, 0), 0.0)
        shift *= 2
    return b


def _gla_thunks(q_ref, k_ref, v_ref, gz_ref, ga_ref, wa_ref, ba_ref, gout_ref, s0_ref, o_ref,
                s_ref, *, chunk, nchunks, carry, anchored):
    rows = chunk * nchunks
    mm_rows = max(chunk, BF16_ROWS)
    heads = range(GLA_H)
    lanes = [slice(h * GLA_DK, (h + 1) * GLA_DK) for h in heads]
    vl = [slice(h * GLA_DV, (h + 1) * GLA_DV) for h in heads]
    per_head = lambda: [None] * GLA_H
    st = {name: per_head() for name in ("b", "qb", "kd", "qb_bf", "kd_bf", "qa_bf", "ka_bf",
                                        "dec_cols", "o_intra")}
    st["o_inter"] = [[] for _ in heads]

    def chunk_row(x, row):
        picked = [x[j * chunk + row:j * chunk + row + 1, :] for j in range(nchunks)]
        return picked, jnp.concatenate([jnp.broadcast_to(r, (chunk, x.shape[1])) for r in picked],
                                       axis=0)

    small = chunk < BF16_ROWS

    def chunk_rows(x_f32, x_bf, sl):
        return _pad_rows(x_f32[sl], mm_rows).astype(BF16) if small else x_bf[sl]

    def prepare():
        r_idx = lax.broadcasted_iota(jnp.int32, (rows, rows), 0)
        c_idx = lax.broadcasted_iota(jnp.int32, (rows, rows), 1)
        diff = r_idx - c_idx
        st["mask"] = (diff >= 0) & (diff <= (r_idx & (chunk - 1)))
        st["v_f32"] = v_ref[...].astype(F32) if small else None
        if carry:
            st["states"] = [s_ref[h] for h in heads]

    def decay(h):
        x = _dot(ga_ref[...], wa_ref[:, lanes[h]]) + ba_ref[:, lanes[h]]
        st["b"][h] = _chunk_cumsum(_log_sigmoid(x) * (1.0 / GLA_TAU), chunk)

    def factors(h):
        b = st["b"][h]
        b_last, b_last_full = chunk_row(b, chunk - 1)
        b_last_rows = jnp.concatenate(b_last + [jnp.zeros((LANES - nchunks, GLA_DK), F32)], axis=0)
        st["dec_cols"][h] = jnp.exp(b_last_rows.T)
        q = q_ref[:, lanes[h]].astype(F32) * (GLA_DK ** -0.5)
        k = k_ref[:, lanes[h]].astype(F32)
        qb = q * jnp.exp(b)
        kd = k * jnp.exp(b_last_full - b)
        st["qb"][h], st["kd"][h] = (qb, kd) if small else (None, None)
        st["qb_bf"][h], st["kd_bf"][h] = qb.astype(BF16), kd.astype(BF16)
        if anchored:
            _, b_mid_full = chunk_row(b, chunk // 2 - 1)
            st["qa_bf"][h] = (q * jnp.exp(b - b_mid_full)).astype(BF16)
            st["ka_bf"][h] = (k * jnp.exp(b_mid_full - b)).astype(BF16)
        else:
            st["qa_bf"][h] = st["qb_bf"][h]
            st["ka_bf"][h] = (k * jnp.exp(-b)).astype(BF16)

    def intra(h):
        a = jnp.where(st["mask"], _dot_nt(st["qa_bf"][h], st["ka_bf"][h]), 0.0)
        st["o_intra"][h] = _dot(a.astype(BF16), v_ref[:, vl[h]])

    def step(j):
        sl = slice(j * chunk, (j + 1) * chunk)
        for h in heads:
            state = st["states"][h] if carry else s0_ref[j, h]
            qb_j = chunk_rows(st["qb"][h], st["qb_bf"][h], sl)
            st["o_inter"][h].append(_dot(qb_j, state.astype(BF16))[:chunk])
            if small:
                v_j = _pad_rows(st["v_f32"][sl, vl[h]], mm_rows).astype(BF16)
            else:
                v_j = v_ref[sl, vl[h]]
            kd_j = chunk_rows(st["kd"][h], st["kd_bf"][h], sl)
            new_state = st["dec_cols"][h][:, j:j + 1] * state + _dot_tn(kd_j, v_j)
            if carry:
                st["states"][h] = new_state
            else:
                s_ref[j, h] = new_state

    def epilogue(h):
        if carry:
            s_ref[h] = st["states"][h]
        o = st["o_intra"][h] + jnp.concatenate(st["o_inter"][h], axis=0)
        gate = _silu(gz_ref[:, vl[h]].astype(F32))
        o_ref[:, vl[h]] = (_rmsnorm_rows(o, gout_ref[...]) * gate).astype(BF16)

    steps = [(prepare, 100)]
    steps += [(functools.partial(decay, h), 230) for h in heads]
    steps += [(functools.partial(factors, h), 170) for h in heads]
    steps += [(functools.partial(intra, h), 60) for h in heads]
    steps += [(functools.partial(step, j), 130) for j in range(nchunks)]
    steps += [(functools.partial(epilogue, h), 190) for h in heads]
    return steps


def _gla_sample_kernel(q_ref, k_ref, v_ref, gz_ref, ga_ref, wa_ref, ba_ref, gout_ref, s0_ref, *rest,
                       seq, group):
    o_ref, s_ref = rest[-2:]
    for thunk, _ in _gla_thunks(q_ref, k_ref, v_ref, gz_ref, ga_ref, wa_ref, ba_ref, gout_ref,
                                s0_ref, o_ref, s_ref, chunk=seq, nchunks=group, carry=False,
                                anchored=False):
        thunk()


def _gla_sample(z, wa, ba, gout, s0, prev, layer, batch, seq):
    group = 8
    rows = group * seq
    state_spec = pl.BlockSpec((None, group, GLA_H, GLA_DK, GLA_DV), lambda i: (layer, i, 0, 0, 0))
    zspec = lambda width, col0: pl.BlockSpec((rows, width), lambda i: (i, col0 // width))
    const = lambda shape: pl.BlockSpec(shape, lambda i: (0,) * len(shape))
    in_specs = [zspec(GLA_QK, Z_GQ), zspec(GLA_QK, Z_GK), zspec(GLA_V, Z_GV), zspec(GLA_V, Z_GZ),
                zspec(LANES, Z_GA), const((LANES, GLA_QK)), const((1, GLA_QK)), const((1, GLA_DV)),
                state_spec]
    args = [z, z, z, z, z, wa, ba, gout, s0]
    aliases = {}
    if prev is not None:
        aliases = {len(args): 1}
        in_specs.append(pl.BlockSpec(memory_space=pl.ANY))
        args.append(prev)
    return pl.pallas_call(
        functools.partial(_gla_sample_kernel, seq=seq, group=group),
        grid=(batch // group,),
        in_specs=in_specs,
        out_specs=[pl.BlockSpec((rows, GLA_V), lambda i: (i, 0)), state_spec],
        out_shape=[jax.ShapeDtypeStruct((batch * seq, GLA_V), BF16),
                   jax.ShapeDtypeStruct(s0.shape, F32)],
        input_output_aliases=aliases,
        compiler_params=_params("parallel"),
        name="gla_sample",
    )(*args)


def _swa_head_of_block(kv_head, blk):
    pairs = SWA_G // 2
    within = 2 * blk if blk < pairs else 2 * (blk - pairs) + 1
    return kv_head * SWA_G + within


def _swa_operands(kk, vv):
    lo = lax.broadcasted_iota(jnp.int32, kk.shape, 1) < SWA_HD
    kk_sw = pltpu.roll(kk, SWA_HD, 1)
    vv_sw = pltpu.roll(vv, SWA_HD, 1)
    ops = []
    for kh in range(SWA_KVH):
        k_this, k_other = (kk, kk_sw) if kh == 0 else (kk_sw, kk)
        v_this, v_other = (vv, vv_sw) if kh == 0 else (vv_sw, vv)
        ops.append((jnp.where(lo, k_this, k_other).astype(BF16),
                    jnp.where(lo, v_this, 0.0).astype(BF16),
                    jnp.where(lo, 0.0, v_other).astype(BF16)))
    return ops


def _swa_scores(q, kh, k_dup):
    nq = q.shape[0]
    lo = lax.broadcasted_iota(jnp.int32, (nq, LANES), 1) < SWA_HD
    even, odd = [], []
    for p in range(SWA_G // 2):
        c0 = kh * SWA_G * SWA_HD + p * LANES
        q_pair = q[:, c0:c0 + LANES]
        even.append(jnp.where(lo, q_pair, jnp.zeros_like(q_pair)))
        odd.append(jnp.where(lo, jnp.zeros_like(q_pair), q_pair))
    lhs = jnp.concatenate(even + odd, axis=0).astype(BF16)
    return _dot_nt(lhs, k_dup) * (SWA_HD ** -0.5)


def _swa_soft_head(s_h, valid, sink):
    s_h = jnp.where(valid, s_h, -jnp.inf)
    m = jnp.maximum(jnp.max(s_h, axis=-1, keepdims=True), sink)
    p_h = jnp.exp(s_h - m)
    den = jnp.sum(p_h, axis=-1, keepdims=True) + jnp.exp(sink - m)
    if p_h.shape[0] % BF16_ROWS == 0:
        p_h = p_h.astype(BF16)
    return p_h, 1.0 / den


def _swa_out(probs, rinv, v_even, v_odd):
    pairs = SWA_G // 2
    nq = probs[0].shape[0]
    lo = lax.broadcasted_iota(jnp.int32, (nq, LANES), 1) < SWA_HD
    p_even = jnp.concatenate(probs[:pairs], axis=0).astype(BF16)
    p_odd = jnp.concatenate(probs[pairs:], axis=0).astype(BF16)
    o = _dot(p_even, v_even) + _dot(p_odd, v_odd)
    outs = [o[p * nq:(p + 1) * nq] * jnp.where(lo, rinv[p], rinv[pairs + p]) for p in range(pairs)]
    return jnp.concatenate(outs, axis=1)


def _swa_sample_kernel(sink_ref, q_ref, sz_ref, kvn_ref, ck_ref, cv_ref, *rest, group, seq):
    o_ref, nk_ref, nv_ref = rest[-3:]
    wb = ck_ref.shape[1]
    nk = 2 * wb
    i = lax.broadcasted_iota(jnp.int32, (seq, nk), 0)
    j = lax.broadcasted_iota(jnp.int32, (seq, nk), 1)
    diff = i + wb - j
    valid = (diff >= 0) & (diff <= WINDOW)
    q = q_ref[...].astype(F32)
    kvn = kvn_ref[...]
    pad = jnp.zeros((nk - wb - seq, SWA_KV), F32)
    outs = []
    for e in range(group):
        rows = slice(e * seq, (e + 1) * seq)
        k_old, v_old = ck_ref[e], cv_ref[e]
        k_new, v_new = kvn[rows, :SWA_KV], kvn[rows, SWA_KV:]
        nk_ref[e] = jnp.concatenate([k_old[seq:], k_new], axis=0)
        nv_ref[e] = jnp.concatenate([v_old[seq:], v_new], axis=0)
        ops = _swa_operands(jnp.concatenate([k_old, k_new, pad], axis=0),
                            jnp.concatenate([v_old, v_new, pad], axis=0))
        heads_out = []
        for kh in range(SWA_KVH):
            k_dup, v_even, v_odd = ops[kh]
            s = _swa_scores(q[rows], kh, k_dup)
            soft = [_swa_soft_head(s[blk * seq:(blk + 1) * seq], valid,
                                   sink_ref[_swa_head_of_block(kh, blk)]) for blk in range(SWA_G)]
            heads_out.append(_swa_out([p for p, _ in soft], [r for _, r in soft], v_even, v_odd))
        outs.append(jnp.concatenate(heads_out, axis=1))
    o_ref[...] = (jnp.concatenate(outs, axis=0) * _silu(sz_ref[...].astype(F32))).astype(BF16)


def _swa_sample(z, kv, sink, ck, cv, prev, layer, batch, seq):
    group = 8
    rows = group * seq
    wb = ck.shape[2]
    cache_spec = pl.BlockSpec((None, group, wb, SWA_KV), lambda i: (layer, i, 0, 0))
    in_specs = [
        pl.BlockSpec(memory_space=pltpu.SMEM),
        pl.BlockSpec((rows, SWA_W), lambda i: (i, Z_SQ // SWA_W)),
        pl.BlockSpec((rows, SWA_W), lambda i: (i, Z_SZ // SWA_W)),
        pl.BlockSpec((rows, KV_COLS), lambda i: (i, 0)),
        cache_spec, cache_spec,
    ]
    args = [sink, z, z, kv, ck, cv]
    aliases = {}
    if prev is not None:
        aliases = {len(args): 1, len(args) + 1: 2}
        in_specs += [pl.BlockSpec(memory_space=pl.ANY)] * 2
        args += list(prev)
    return pl.pallas_call(
        functools.partial(_swa_sample_kernel, group=group, seq=seq),
        grid=(batch // group,),
        in_specs=in_specs,
        out_specs=[pl.BlockSpec((rows, SWA_W), lambda i: (i, 0)), cache_spec, cache_spec],
        out_shape=[
            jax.ShapeDtypeStruct((batch * seq, SWA_W), BF16),
            jax.ShapeDtypeStruct(ck.shape, F32),
            jax.ShapeDtypeStruct(cv.shape, F32),
        ],
        input_output_aliases=aliases,
        compiler_params=_params("parallel"),
        name="swa_sample",
    )(*args)


MEM_HALVES = X_HD // LANES
MEM_TILE_ROWS = X_H * MEM_HALVES


def _mem_rows_view(x):
    lead = x.shape[:-3]
    n = len(lead)
    x = x.reshape(*lead, MEM_LEN, X_H, MEM_HALVES, LANES)
    x = x.transpose(*range(n), n, n + 2, n + 1, n + 3)
    return x.reshape(*lead, MEM_LEN * MEM_TILE_ROWS, LANES)


def _mem_from_rows_view(x):
    lead = x.shape[:-2]
    n = len(lead)
    x = x.reshape(*lead, MEM_LEN, MEM_HALVES, X_H, LANES)
    x = x.transpose(*range(n), n, n + 2, n + 1, n + 3)
    return x.reshape(*lead, MEM_LEN, X_H, X_HD)


def _mem_head_rows(head, half):
    return pl.ds(half * X_H + head, MEM_LEN, stride=MEM_TILE_ROWS)


def _mem_kv_kernel(x_ref, g_ref, w_ref, *rest):
    kb_ref, vb_ref, kn_ref, vn_ref = rest[-4:]
    h = _rmsnorm_rows(x_ref[...], g_ref[...]).astype(BF16)
    for t, (b_ref, n_ref) in enumerate(((kb_ref, kn_ref), (vb_ref, vn_ref))):
        for head in range(X_H):
            c0 = head * X_HD
            res = _dot(h, w_ref[:, t * X_W + c0:t * X_W + c0 + X_HD])
            b_ref[:, c0:c0 + X_HD] = res.astype(BF16)
            for half in range(MEM_HALVES):
                n_ref[_mem_head_rows(head, half), :] = res[:, half * LANES:(half + 1) * LANES]


def _mem_kv(mem2d, g, w, prev, layer, depth, batch):
    d = mem2d.shape[1]
    rows_spec = pl.BlockSpec((None, None, MEM_LEN * MEM_TILE_ROWS, LANES), lambda i: (layer, i, 0, 0))
    rows_shape = jax.ShapeDtypeStruct((depth, batch, MEM_LEN * MEM_TILE_ROWS, LANES), F32)
    dense_spec = pl.BlockSpec((MEM_LEN, X_W), lambda i: (i, 0))
    dense_shape = jax.ShapeDtypeStruct((batch * MEM_LEN, X_W), BF16)
    in_specs = [pl.BlockSpec((MEM_LEN, d), lambda i: (i, 0)),
                pl.BlockSpec((1, d), lambda i: (0, 0)),
                pl.BlockSpec((d, 2 * X_W), lambda i: (0, 0))]
    args = [mem2d, g, w]
    aliases = {}
    if prev is not None:
        aliases = {len(args): 2, len(args) + 1: 3}
        in_specs += [pl.BlockSpec(memory_space=pl.ANY)] * 2
        args += list(prev)
    return pl.pallas_call(
        _mem_kv_kernel,
        grid=(batch,),
        in_specs=in_specs,
        out_specs=[dense_spec, dense_spec, rows_spec, rows_spec],
        out_shape=[dense_shape, dense_shape, rows_shape, rows_shape],
        input_output_aliases=aliases,
        compiler_params=_params("parallel"),
        name="mem_kv",
    )(*args)


def _mem_softmax(s):
    p = jnp.exp(s - jnp.max(s, axis=-1, keepdims=True))
    return p, 1.0 / jnp.sum(p, axis=-1, keepdims=True)


def _mem_sample_kernel(q_ref, mz_ref, mk_ref, mv_ref, o_ref, *, group, seq):
    def head_tile(ref, e, h):
        halves = [ref[e, _mem_head_rows(h, half), :] for half in range(MEM_HALVES)]
        return jnp.concatenate(halves, axis=1).astype(BF16)

    q = q_ref[...].astype(F32)
    pairs = [(e, h) for e in range(group) for h in range(X_H)]
    scores = []
    for e, h in pairs:
        q_eh = _pad_rows(q[e * seq:(e + 1) * seq, h * X_HD:(h + 1) * X_HD], BF16_ROWS).astype(BF16)
        scores.append(_dot_nt(q_eh, head_tile(mk_ref, e, h))[:seq])
    p, rinv = _mem_softmax(jnp.concatenate(scores, axis=0) * (X_HD ** -0.5))
    outs = [[] for _ in range(group)]
    for i, (e, h) in enumerate(pairs):
        p_eh = _pad_rows(p[i * seq:(i + 1) * seq], BF16_ROWS).astype(BF16)
        outs[e].append(_dot(p_eh, head_tile(mv_ref, e, h))[:seq] * rinv[i * seq:(i + 1) * seq])
    o = jnp.concatenate([jnp.concatenate(row, axis=1) for row in outs], axis=0)
    o_ref[...] = (o * _silu(mz_ref[...].astype(F32))).astype(BF16)


def _mem_sample(z, mk, mv, layer, batch, seq):
    group = 4
    rows = group * seq
    mem_spec = pl.BlockSpec((None, group, MEM_LEN * MEM_TILE_ROWS, LANES),
                            lambda i: (layer, i, 0, 0))
    return pl.pallas_call(
        functools.partial(_mem_sample_kernel, group=group, seq=seq),
        grid=(batch // group,),
        in_specs=[
            pl.BlockSpec((rows, X_W), lambda i: (i, Z_MQ // X_W)),
            pl.BlockSpec((rows, X_W), lambda i: (i, Z_MZ // X_W)),
            mem_spec, mem_spec,
        ],
        out_specs=pl.BlockSpec((rows, X_W), lambda i: (i, 0)),
        out_shape=jax.ShapeDtypeStruct((batch * seq, X_W), BF16),
        compiler_params=_params("parallel"),
        name="mem_sample",
    )(z, z, mk, mv)


FRONT_ROWS = 256
FRONT_GLA_CHUNK = 2 * GLA_CHUNK


def _front_kernel(sink_ref, x_ref, g_ref, w_ref, wa_ref, ba_ref, gout_ref, mk_ref, mv_ref, *rest):
    (ogla_ref, oswa_ref, omem_ref, mg_ref, kvl_ref, s_ref,
     gq_scr, gk_scr, gv_scr, gz_scr, ga_scr, sq_scr, sz_scr, mq_scr, mz_scr, kv_scr, kvp_scr) = rest[-17:]
    n = pl.program_id(1)
    tm = FRONT_ROWS

    @pl.when(n == 0)
    def _():
        s_ref[...] = jnp.zeros_like(s_ref)
        kvp_scr[...] = jnp.zeros_like(kvp_scr)

    h = _rmsnorm_rows(x_ref[...], g_ref[...]).astype(BF16)

    def project(chunk):
        dst, dst_c0, w_r0, width = chunk
        if dst is ga_scr:
            dst[...] = _project_decay(h, w_ref).astype(dst.dtype)
        else:
            dst[:, dst_c0:dst_c0 + width] = _project(h, w_ref, w_r0, width).astype(dst.dtype)

    def chunks_of(dst, w_r0, width):
        return [(dst, c, w_r0 + c, min(MXU_COLS, width - c)) for c in range(0, width, MXU_COLS)]

    swa_cols = chunks_of(kv_scr, W_SK, KV_COLS) + chunks_of(sq_scr, W_SQ, SWA_W) \
        + chunks_of(sz_scr, W_SZ, SWA_W)
    gla_cols = (chunks_of(ga_scr, W_GA, LANES) + chunks_of(gq_scr, W_GQ, GLA_QK)
                + chunks_of(gk_scr, W_GK, GLA_QK) + chunks_of(gv_scr, W_GV, GLA_V)
                + chunks_of(gz_scr, W_GZ, GLA_V))
    mem_cols = chunks_of(mq_scr, W_MQ, X_W) + chunks_of(mz_scr, W_MZ, X_W)
    merge_cols = chunks_of(mg_ref, W_MG, N_BRANCH * D_MODEL)

    i = lax.broadcasted_iota(jnp.int32, (WINDOW, 2 * WINDOW), 0)
    j = lax.broadcasted_iota(jnp.int32, (WINDOW, 2 * WINDOW), 1)
    diff = i + WINDOW - j
    band = (diff >= 0) & (diff <= WINDOW)
    swa = {}

    def swa_start(blk):
        prev = kvp_scr[...] if blk == 0 else kv_scr[(blk - 1) * WINDOW:blk * WINDOW, :]
        kvb = jnp.concatenate([prev, kv_scr[blk * WINDOW:(blk + 1) * WINDOW, :]], axis=0)
        swa["ops"] = _swa_operands(kvb[:, :SWA_KV], kvb[:, SWA_KV:])
        swa["valid"] = band & (j >= jnp.where(n > 0, 0, WINDOW)) if blk == 0 else band

    def swa_scores(blk, kh):
        swa["s"] = _swa_scores(sq_scr[blk * WINDOW:(blk + 1) * WINDOW, :], kh, swa["ops"][kh][0])
        swa["soft"] = []

    def swa_soft(kh, hb):
        swa["soft"].append(_swa_soft_head(swa["s"][hb * WINDOW:(hb + 1) * WINDOW], swa["valid"],
                                          sink_ref[_swa_head_of_block(kh, hb)]))

    def swa_out(blk, kh):
        _, v_even, v_odd = swa["ops"][kh]
        o = _swa_out([p for p, _ in swa["soft"]], [r for _, r in swa["soft"]], v_even, v_odd)
        rows = slice(blk * WINDOW, (blk + 1) * WINDOW)
        cols = slice(kh * SWA_G * SWA_HD, (kh + 1) * SWA_G * SWA_HD)
        oswa_ref[rows, cols] = (o * _silu(sz_scr[rows, cols].astype(F32))).astype(BF16)

    swa_steps = []
    for blk in range(tm // WINDOW):
        swa_steps.append((functools.partial(swa_start, blk), 50))
        for kh in range(SWA_KVH):
            swa_steps.append((functools.partial(swa_scores, blk, kh), 30))
            swa_steps += [(functools.partial(swa_soft, kh, hb), 40) for hb in range(SWA_G)]
            swa_steps.append((functools.partial(swa_out, blk, kh), 75))

    mem = {}

    def mem_scores(hd):
        cols = slice(hd * X_HD, (hd + 1) * X_HD)
        p, rinv = _mem_softmax(_dot_nt(mq_scr[:, cols], mk_ref[:, cols]) * (X_HD ** -0.5))
        mem["p"], mem["rinv"] = p.astype(BF16), rinv

    def mem_out(hd):
        cols = slice(hd * X_HD, (hd + 1) * X_HD)
        o = _dot(mem["p"], mv_ref[:, cols]) * mem["rinv"]
        omem_ref[:, cols] = (o * _silu(mz_scr[:, cols].astype(F32))).astype(BF16)

    mem_steps = []
    for hd in range(X_H):
        mem_steps += [(functools.partial(mem_scores, hd), 130), (functools.partial(mem_out, hd), 160)]

    gla_steps = _gla_thunks(gq_scr, gk_scr, gv_scr, gz_scr, ga_scr, wa_ref, ba_ref, gout_ref, None,
                            ogla_ref, s_ref, chunk=FRONT_GLA_CHUNK, nchunks=tm // FRONT_GLA_CHUNK,
                            carry=True, anchored=True)

    for chunk in swa_cols:
        project(chunk)
    chunks = gla_cols + mem_cols + merge_cols
    steps = [(t, w, 0) for t, w in swa_steps]
    steps += [(t, w, len(gla_cols)) for t, w in gla_steps]
    steps += [(t, w, len(gla_cols) + len(mem_cols)) for t, w in mem_steps]
    _interleave(steps, chunks, project)

    kv_tail = kv_scr[tm - WINDOW:, :]
    kvp_scr[...] = kv_tail
    kvl_ref[...] = kv_tail.T


def _front(x2d, g, w, sink, wa, ba, gout, mk, mv, prev_state, layer, depth, batch, seq):
    t, d = x2d.shape
    tm = FRONT_ROWS
    nblk = seq // tm
    row = lambda width: pl.BlockSpec((tm, width), lambda b, n: (b * nblk + n, 0))
    const = lambda shape: pl.BlockSpec(shape, lambda b, n: (0,) * len(shape))
    mem_spec = pl.BlockSpec((MEM_LEN, X_W), lambda b, n: (b, 0))
    state_spec = pl.BlockSpec((None, None, GLA_H, GLA_DK, GLA_DV), lambda b, n: (layer, b, 0, 0, 0))
    in_specs = [pl.BlockSpec(memory_space=pltpu.SMEM), row(d), const((1, d)),
                pl.BlockSpec(w.shape, lambda b, n: (0, 0), pipeline_mode=pl.Buffered(1)),
                const((LANES, GLA_QK)), const((1, GLA_QK)), const((1, GLA_DV)), mem_spec, mem_spec]
    args = [sink, x2d, g, w, wa, ba, gout, mk, mv]
    aliases = {}
    if prev_state is not None:
        aliases = {len(args): 5}
        in_specs.append(pl.BlockSpec(memory_space=pl.ANY))
        args.append(prev_state)
    scratch = [pltpu.VMEM((tm, width), BF16) for width in
               (GLA_QK, GLA_QK, GLA_V, GLA_V, LANES, SWA_W, SWA_W, X_W, X_W)]
    scratch += [pltpu.VMEM((tm, KV_COLS), F32), pltpu.VMEM((WINDOW, KV_COLS), F32)]
    return pl.pallas_call(
        _front_kernel,
        grid=(batch, nblk),
        in_specs=in_specs,
        out_specs=[row(GLA_V), row(SWA_W), row(X_W), row(N_BRANCH * D_MODEL),
                   pl.BlockSpec((None, KV_COLS, WINDOW), lambda b, n: (b, 0, 0)), state_spec],
        out_shape=[jax.ShapeDtypeStruct((t, GLA_V), BF16), jax.ShapeDtypeStruct((t, SWA_W), BF16),
                   jax.ShapeDtypeStruct((t, X_W), BF16),
                   jax.ShapeDtypeStruct((t, N_BRANCH * D_MODEL), BF16),
                   jax.ShapeDtypeStruct((batch, KV_COLS, WINDOW), F32),
                   jax.ShapeDtypeStruct((depth, batch, GLA_H, GLA_DK, GLA_DV), F32)],
        scratch_shapes=scratch,
        input_output_aliases=aliases,
        compiler_params=_params("parallel", "arbitrary"),
        name="front",
    )(*args)


def _merge_kernel(x_ref, a_ref, s_ref, m_ref, ga_ref, gs_ref, gm_ref, wa_ref, ws_ref, wm_ref,
                  wo_ref, *rest, final):
    if final:
        gf_ref, y_ref = rest
    else:
        (y_ref,) = rest
    merged = (_sigmoid(ga_ref[...].astype(F32)) * _dot(a_ref[...], wa_ref[...])
              + _sigmoid(gs_ref[...].astype(F32)) * _dot(s_ref[...], ws_ref[...])
              + _sigmoid(gm_ref[...].astype(F32)) * _dot(m_ref[...], wm_ref[...]))
    y = x_ref[...] + _dot(merged.astype(BF16), wo_ref[...])
    if final:
        y = _rmsnorm_rows(y, gf_ref[...])
    y_ref[...] = y


def _merge(x2d, o_gla, o_swa, o_mem, gates, gate_col0, w_gla, w_swa, w_mem, w_out, g_final):
    t, d = x2d.shape
    tm = _row_tile(t)
    final = g_final is not None
    row = lambda width: pl.BlockSpec((tm, width), lambda i: (i, 0))
    gate = lambda k: pl.BlockSpec((tm, d), lambda i: (i, gate_col0 // d + k))
    weight = pl.BlockSpec((d, d), lambda i: (0, 0))
    in_specs = [row(d), row(d), row(d), row(d), gate(0), gate(1), gate(2),
                weight, weight, weight, weight]
    args = [x2d, o_gla, o_swa, o_mem, gates, gates, gates, w_gla, w_swa, w_mem, w_out]
    if final:
        in_specs.append(pl.BlockSpec((1, d), lambda i: (0, 0)))
        args.append(g_final)
    return pl.pallas_call(
        functools.partial(_merge_kernel, final=final),
        grid=(t // tm,),
        in_specs=in_specs,
        out_specs=row(d),
        out_shape=jax.ShapeDtypeStruct((t, d), F32),
        compiler_params=_params("parallel"),
        name="merge_final" if final else "merge",
    )(*args)


def kernel(x_prompt, x_sample, mem_prompt, state_gla, cache_swa_k, cache_swa_v, cache_mem_k,
           cache_mem_v, g_norm, w_in, w_gla_a, b_gla_a, g_gla_out, swa_sink, g_mem, w_mem_kv,
           w_br_gla, w_br_swa, w_br_mem, w_out, g_final):
    depth = w_in.shape[0]
    batch, seq, d = x_prompt.shape
    dec_batch, dec_seq, _ = x_sample.shape
    wb = cache_swa_k.shape[2]

    w_in_r = w_in.transpose(0, 2, 1).astype(BF16)
    w_a = jnp.pad(w_gla_a, ((0, 0), (0, LANES - GLA_RANK), (0, 0))).astype(BF16)
    w_mem_b = w_mem_kv.astype(BF16)
    w_g, w_s, w_m, w_o = (w.astype(BF16) for w in (w_br_gla, w_br_swa, w_br_mem, w_out))
    g_fin = g_final.reshape(1, d)

    yp = x_prompt.reshape(batch * seq, d)
    ys = x_sample.reshape(dec_batch * dec_seq, d)
    mem2d = mem_prompt.reshape(batch * MEM_LEN, d)
    ck_all = cache_swa_k.reshape(depth, dec_batch, wb, SWA_KV)
    cv_all = cache_swa_v.reshape(depth, dec_batch, wb, SWA_KV)
    cmk_all = _mem_rows_view(cache_mem_k)
    cmv_all = _mem_rows_view(cache_mem_v)
    gla_p = gla_s = swa_new = mem_new = None
    kp_l, vp_l = [], []
    for l in range(depth):
        last = l == depth - 1
        gn = g_norm[l].reshape(1, d)
        ba = b_gla_a[l].reshape(1, GLA_QK)
        gout = g_gla_out[l].reshape(1, GLA_DV)
        sink = swa_sink[l]
        branch_w = (w_g[l], w_s[l], w_m[l], w_o[l], g_fin if last else None)

        mk, mv, *mem_new = _mem_kv(mem2d, g_mem[l].reshape(1, d), w_mem_b[l], mem_new, l, depth,
                                   batch)
        o_gla, o_swa, o_mem, gates, kv_tail, gla_p = _front(
            yp, gn, w_in_r[l], sink, w_a[l], ba, gout, mk, mv, gla_p, l, depth, batch, seq)
        yp = _merge(yp, o_gla, o_swa, o_mem, gates, 0, *branch_w)
        kv_heads = kv_tail.reshape(batch, 2, SWA_KVH, SWA_HD, WINDOW).transpose(0, 1, 4, 2, 3)
        kp_l.append(kv_heads[:, 0])
        vp_l.append(kv_heads[:, 1])

        z, kv = _in_proj(ys, gn, w_in_r[l])
        o_gla, gla_s = _gla_sample(z, w_a[l], ba, gout, state_gla, gla_s, l, dec_batch, dec_seq)
        o_swa, *swa_new = _swa_sample(z, kv, sink, ck_all, cv_all, swa_new, l, dec_batch, dec_seq)
        o_mem = _mem_sample(z, cmk_all, cmv_all, l, dec_batch, dec_seq)
        ys = _merge(ys, o_gla, o_swa, o_mem, z, Z_MG, *branch_w)

    k_s, v_s = (c.reshape(depth, dec_batch, wb, SWA_KVH, SWA_HD) for c in swa_new)
    mk_p, mv_p = (_mem_from_rows_view(c) for c in mem_new)
    return (yp.reshape(batch, seq, d), ys.reshape(dec_batch, dec_seq, d), gla_p,
            jnp.stack(kp_l), jnp.stack(vp_l), mk_p, mv_p, gla_s, k_s, v_s)
```

```python
import functools

import jax
import jax.numpy as jnp
from jax import lax
from jax.experimental import pallas as pl
from jax.experimental.pallas import tpu as pltpu

F32 = jnp.float32
BF16 = jnp.bfloat16

D_MODEL = 1024
GLA_H = 4
GLA_DK = 128
GLA_DV = 256
GLA_QK = GLA_H * GLA_DK
GLA_V = GLA_H * GLA_DV
GLA_RANK = 16
GLA_TAU = 16.0
GLA_CHUNK = 32
SWA_HD = 64
SWA_H = 16
SWA_KVH = 2
SWA_G = SWA_H // SWA_KVH
SWA_W = SWA_H * SWA_HD
SWA_KV = SWA_KVH * SWA_HD
WINDOW = 128
MEM_LEN = 256
X_H = 4
X_HD = 256
X_W = X_H * X_HD
N_BRANCH = 3
IN_SPLITS = (GLA_QK, GLA_QK, GLA_V, GLA_V, GLA_RANK, SWA_W, SWA_KV, SWA_KV, SWA_W, X_W, X_W,
             N_BRANCH * D_MODEL)
EPS = 1e-6

LANES = 128
BF16_ROWS = 16
MXU_COLS = 256
VMEM_LIMIT = 56 * 1024 * 1024

Z_GQ = 0
Z_GK = Z_GQ + GLA_QK
Z_GV = Z_GK + GLA_QK
Z_GZ = Z_GV + GLA_V
Z_SQ = Z_GZ + GLA_V
Z_SZ = Z_SQ + SWA_W
Z_MQ = Z_SZ + SWA_W
Z_MZ = Z_MQ + X_W
Z_MG = Z_MZ + X_W
Z_GA = Z_MG + N_BRANCH * D_MODEL
Z_COLS = Z_GA + LANES
KV_COLS = 2 * SWA_KV

(W_GQ, W_GK, W_GV, W_GZ, W_GA, W_SQ, W_SK, W_SV, W_SZ, W_MQ, W_MZ, W_MG) = (
    sum(IN_SPLITS[:i]) for i in range(len(IN_SPLITS)))
Z_PIECES = ((Z_GQ, W_GQ, GLA_QK), (Z_GK, W_GK, GLA_QK), (Z_GV, W_GV, GLA_V), (Z_GZ, W_GZ, GLA_V),
            (Z_SQ, W_SQ, SWA_W), (Z_SZ, W_SZ, SWA_W), (Z_MQ, W_MQ, X_W), (Z_MZ, W_MZ, X_W),
            (Z_MG, W_MG, N_BRANCH * D_MODEL))


def _project(h, w_ref, row0, width):
    return _dot_nt(h, w_ref[row0:row0 + width, :])


def _project_decay(h, w_ref):
    res = _project(h, w_ref, W_GA, LANES)
    lane = lax.broadcasted_iota(jnp.int32, res.shape, 1)
    return jnp.where(lane < GLA_RANK, res, 0.0)


def _row_tile(rows):
    return 512 if rows >= 4096 else 256


def _params(*semantics):
    return pltpu.CompilerParams(dimension_semantics=semantics, vmem_limit_bytes=VMEM_LIMIT)


def _sigmoid(x):
    return 1.0 / (1.0 + jnp.exp(-x))


def _silu(x):
    return x * _sigmoid(x)


def _log_sigmoid(x):
    return jnp.minimum(x, 0.0) - jnp.log(1.0 + jnp.exp(-jnp.abs(x)))


def _rmsnorm_rows(x, g):
    return x * lax.rsqrt(jnp.mean(x * x, axis=-1, keepdims=True) + EPS) * g


def _dot(a, b):
    return jnp.dot(a, b, preferred_element_type=F32)


def _dot_nt(a, b):
    return lax.dot_general(a, b, (((1,), (1,)), ((), ())), preferred_element_type=F32)


def _dot_tn(a, b):
    return lax.dot_general(a, b, (((0,), (0,)), ((), ())), preferred_element_type=F32)


def _pad_rows(x, rows):
    if x.shape[0] >= rows:
        return x
    return jnp.concatenate([x, jnp.zeros((rows - x.shape[0], x.shape[1]), x.dtype)], axis=0)


def _interleave(steps, chunks, run_chunk):
    total = sum(w for _, w, _ in steps)
    done, acc = 0, 0
    for thunk, weight, needs in steps:
        acc += weight
        upto = max(needs, min(len(chunks), -(-acc * len(chunks) // total)))
        for chunk in chunks[done:upto]:
            run_chunk(chunk)
        done = max(done, upto)
        thunk()


def _in_proj_kernel(x_ref, g_ref, w_ref, z_ref, kv_ref):
    h = _rmsnorm_rows(x_ref[...], g_ref[...]).astype(BF16)
    for z0, w0, width in Z_PIECES:
        for c in range(0, width, 512):
            z_ref[:, z0 + c:z0 + c + 512] = _project(h, w_ref, w0 + c, 512).astype(BF16)
    z_ref[:, Z_GA:Z_COLS] = _project_decay(h, w_ref).astype(BF16)
    kv_ref[...] = _project(h, w_ref, W_SK, KV_COLS)


def _in_proj(x2d, g, w):
    t, d = x2d.shape
    tm = _row_tile(t)
    return pl.pallas_call(
        _in_proj_kernel,
        grid=(t // tm,),
        in_specs=[
            pl.BlockSpec((tm, d), lambda i: (i, 0)),
            pl.BlockSpec((1, d), lambda i: (0, 0)),
            pl.BlockSpec(w.shape, lambda i: (0, 0), pipeline_mode=pl.Buffered(1)),
        ],
        out_specs=[pl.BlockSpec((tm, Z_COLS), lambda i: (i, 0)),
                   pl.BlockSpec((tm, KV_COLS), lambda i: (i, 0))],
        out_shape=[jax.ShapeDtypeStruct((t, Z_COLS), BF16), jax.ShapeDtypeStruct((t, KV_COLS), F32)],
        compiler_params=_params("parallel"),
        name="in_proj",
    )(x2d, g, w)


def _chunk_cumsum(g, chunk):
    pos = lax.broadcasted_iota(jnp.int32, g.shape, 0) & (chunk - 1)
    b = g
    shift = 1
    while shift < chunk:
        b = b + jnp.where(pos >= shift, pltpu.roll(b, shift, 0), 0.0)
        shift *= 2
    return b


def _gla_thunks(q_ref, k_ref, v_ref, gz_ref, ga_ref, wa_ref, ba_ref, gout_ref, s0_ref, o_ref,
                s_ref, *, chunk, nchunks, carry, anchored):
    rows = chunk * nchunks
    mm_rows = max(chunk, BF16_ROWS)
    heads = range(GLA_H)
    lanes = [slice(h * GLA_DK, (h + 1) * GLA_DK) for h in heads]
    vl = [slice(h * GLA_DV, (h + 1) * GLA_DV) for h in heads]
    per_head = lambda: [None] * GLA_H
    st = {name: per_head() for name in ("b", "qb", "kd", "qb_bf", "kd_bf", "qa_bf", "ka_bf",
                                        "dec_cols", "o_intra")}
    st["o_inter"] = [[] for _ in heads]

    def chunk_row(x, row):
        picked = [x[j * chunk + row:j * chunk + row + 1, :] for j in range(nchunks)]
        return picked, jnp.concatenate([jnp.broadcast_to(r, (chunk, x.shape[1])) for r in picked],
                                       axis=0)

    small = chunk < BF16_ROWS

    def chunk_rows(x_f32, x_bf, sl):
        return _pad_rows(x_f32[sl], mm_rows).astype(BF16) if small else x_bf[sl]

    def prepare():
        r_idx = lax.broadcasted_iota(jnp.int32, (rows, rows), 0)
        c_idx = lax.broadcasted_iota(jnp.int32, (rows, rows), 1)
        diff = r_idx - c_idx
        st["mask"] = (diff >= 0) & (diff <= (r_idx & (chunk - 1)))
        st["v_f32"] = v_ref[...].astype(F32) if small else None
        if carry:
            st["states"] = [s_ref[h] for h in heads]

    def decay(h):
        x = _dot(ga_ref[...], wa_ref[:, lanes[h]]) + ba_ref[:, lanes[h]]
        st["b"][h] = _chunk_cumsum(_log_sigmoid(x) * (1.0 / GLA_TAU), chunk)

    def factors(h):
        b = st["b"][h]
        b_last, b_last_full = chunk_row(b, chunk - 1)
        b_last_rows = jnp.concatenate(b_last + [jnp.zeros((LANES - nchunks, GLA_DK), F32)], axis=0)
        st["dec_cols"][h] = jnp.exp(b_last_rows.T)
        q = q_ref[:, lanes[h]].astype(F32) * (GLA_DK ** -0.5)
        k = k_ref[:, lanes[h]].astype(F32)
        qb = q * jnp.exp(b)
        kd = k * jnp.exp(b_last_full - b)
        st["qb"][h], st["kd"][h] = (qb, kd) if small else (None, None)
        st["qb_bf"][h], st["kd_bf"][h] = qb.astype(BF16), kd.astype(BF16)
        if anchored:
            _, b_mid_full = chunk_row(b, chunk // 2 - 1)
            st["qa_bf"][h] = (q * jnp.exp(b - b_mid_full)).astype(BF16)
            st["ka_bf"][h] = (k * jnp.exp(b_mid_full - b)).astype(BF16)
        else:
            st["qa_bf"][h] = st["qb_bf"][h]
            st["ka_bf"][h] = (k * jnp.exp(-b)).astype(BF16)

    def intra(h):
        a = jnp.where(st["mask"], _dot_nt(st["qa_bf"][h], st["ka_bf"][h]), 0.0)
        st["o_intra"][h] = _dot(a.astype(BF16), v_ref[:, vl[h]])

    def step(j):
        sl = slice(j * chunk, (j + 1) * chunk)
        for h in heads:
            state = st["states"][h] if carry else s0_ref[j, h]
            qb_j = chunk_rows(st["qb"][h], st["qb_bf"][h], sl)
            st["o_inter"][h].append(_dot(qb_j, state.astype(BF16))[:chunk])
            if small:
                v_j = _pad_rows(st["v_f32"][sl, vl[h]], mm_rows).astype(BF16)
            else:
                v_j = v_ref[sl, vl[h]]
            kd_j = chunk_rows(st["kd"][h], st["kd_bf"][h], sl)
            new_state = st["dec_cols"][h][:, j:j + 1] * state + _dot_tn(kd_j, v_j)
            if carry:
                st["states"][h] = new_state
            else:
                s_ref[j, h] = new_state

    def epilogue(h):
        if carry:
            s_ref[h] = st["states"][h]
        o = st["o_intra"][h] + jnp.concatenate(st["o_inter"][h], axis=0)
        gate = _silu(gz_ref[:, vl[h]].astype(F32))
        o_ref[:, vl[h]] = (_rmsnorm_rows(o, gout_ref[...]) * gate).astype(BF16)

    steps = [(prepare, 100)]
    steps += [(functools.partial(decay, h), 230) for h in heads]
    steps += [(functools.partial(factors, h), 170) for h in heads]
    steps += [(functools.partial(intra, h), 60) for h in heads]
    steps += [(functools.partial(step, j), 130) for j in range(nchunks)]
    steps += [(functools.partial(epilogue, h), 190) for h in heads]
    return steps


def _gla_sample_kernel(q_ref, k_ref, v_ref, gz_ref, ga_ref, wa_ref, ba_ref, gout_ref, s0_ref, *rest,
                       seq, group):
    o_ref, s_ref = rest[-2:]
    for thunk, _ in _gla_thunks(q_ref, k_ref, v_ref, gz_ref, ga_ref, wa_ref, ba_ref, gout_ref,
                                s0_ref, o_ref, s_ref, chunk=seq, nchunks=group, carry=False,
                                anchored=False):
        thunk()


def _gla_sample(z, wa, ba, gout, s0, prev, layer, batch, seq):
    group = 8
    rows = group * seq
    state_spec = pl.BlockSpec((None, group, GLA_H, GLA_DK, GLA_DV), lambda i: (layer, i, 0, 0, 0))
    zspec = lambda width, col0: pl.BlockSpec((rows, width), lambda i: (i, col0 // width))
    const = lambda shape: pl.BlockSpec(shape, lambda i: (0,) * len(shape))
    in_specs = [zspec(GLA_QK, Z_GQ), zspec(GLA_QK, Z_GK), zspec(GLA_V, Z_GV), zspec(GLA_V, Z_GZ),
                zspec(LANES, Z_GA), const((LANES, GLA_QK)), const((1, GLA_QK)), const((1, GLA_DV)),
                state_spec]
    args = [z, z, z, z, z, wa, ba, gout, s0]
    aliases = {}
    if prev is not None:
        aliases = {len(args): 1}
        in_specs.append(pl.BlockSpec(memory_space=pl.ANY))
        args.append(prev)
    return pl.pallas_call(
        functools.partial(_gla_sample_kernel, seq=seq, group=group),
        grid=(batch // group,),
        in_specs=in_specs,
        out_specs=[pl.BlockSpec((rows, GLA_V), lambda i: (i, 0)), state_spec],
        out_shape=[jax.ShapeDtypeStruct((batch * seq, GLA_V), BF16),
                   jax.ShapeDtypeStruct(s0.shape, F32)],
        input_output_aliases=aliases,
        compiler_params=_params("parallel"),
        name="gla_sample",
    )(*args)


def _swa_head_of_block(kv_head, blk):
    pairs = SWA_G // 2
    within = 2 * blk if blk < pairs else 2 * (blk - pairs) + 1
    return kv_head * SWA_G + within


def _swa_operands(kk, vv):
    lo = lax.broadcasted_iota(jnp.int32, kk.shape, 1) < SWA_HD
    kk_sw = pltpu.roll(kk, SWA_HD, 1)
    vv_sw = pltpu.roll(vv, SWA_HD, 1)
    ops = []
    for kh in range(SWA_KVH):
        k_this, k_other = (kk, kk_sw) if kh == 0 else (kk_sw, kk)
        v_this, v_other = (vv, vv_sw) if kh == 0 else (vv_sw, vv)
        ops.append((jnp.where(lo, k_this, k_other).astype(BF16),
                    jnp.where(lo, v_this, 0.0).astype(BF16),
                    jnp.where(lo, 0.0, v_other).astype(BF16)))
    return ops


def _swa_operands_t(kk_t, vv_t):
    zeros = jnp.zeros((SWA_HD, kk_t.shape[1]), F32)
    ops = []
    for kh in range(SWA_KVH):
        k_h = kk_t[kh * SWA_HD:(kh + 1) * SWA_HD]
        v_h = vv_t[kh * SWA_HD:(kh + 1) * SWA_HD]
        ops.append((jnp.concatenate([k_h, k_h], axis=0).astype(BF16),
                    jnp.concatenate([v_h, zeros], axis=0).astype(BF16),
                    jnp.concatenate([zeros, v_h], axis=0).astype(BF16)))
    return ops


def _swa_scores(q, kh, k_dup, keys_on_lanes=False):
    nq = q.shape[0]
    lo = lax.broadcasted_iota(jnp.int32, (nq, LANES), 1) < SWA_HD
    even, odd = [], []
    for p in range(SWA_G // 2):
        c0 = kh * SWA_G * SWA_HD + p * LANES
        q_pair = q[:, c0:c0 + LANES]
        even.append(jnp.where(lo, q_pair, jnp.zeros_like(q_pair)))
        odd.append(jnp.where(lo, jnp.zeros_like(q_pair), q_pair))
    lhs = jnp.concatenate(even + odd, axis=0).astype(BF16)
    s = _dot(lhs, k_dup) if keys_on_lanes else _dot_nt(lhs, k_dup)
    return s * (SWA_HD ** -0.5)


def _swa_soft_head(s_h, valid, sink):
    s_h = jnp.where(valid, s_h, -jnp.inf)
    m = jnp.maximum(jnp.max(s_h, axis=-1, keepdims=True), sink)
    p_h = jnp.exp(s_h - m)
    den = jnp.sum(p_h, axis=-1, keepdims=True) + jnp.exp(sink - m)
    if p_h.shape[0] % BF16_ROWS == 0:
        p_h = p_h.astype(BF16)
    return p_h, 1.0 / den


def _swa_out(probs, rinv, v_even, v_odd):
    pairs = SWA_G // 2
    nq = probs[0].shape[0]
    lo = lax.broadcasted_iota(jnp.int32, (nq, LANES), 1) < SWA_HD
    p_even = jnp.concatenate(probs[:pairs], axis=0).astype(BF16)
    p_odd = jnp.concatenate(probs[pairs:], axis=0).astype(BF16)
    o = _dot(p_even, v_even) + _dot(p_odd, v_odd)
    outs = [o[p * nq:(p + 1) * nq] * jnp.where(lo, rinv[p], rinv[pairs + p]) for p in range(pairs)]
    return jnp.concatenate(outs, axis=1)


def _swa_sample_kernel(sink_ref, q_ref, sz_ref, kvn_ref, ck_ref, cv_ref, *rest, group, seq):
    o_ref, nk_ref, nv_ref = rest[-3:]
    wb = ck_ref.shape[2]
    nk = 2 * wb
    pairs = SWA_G // 2
    blk_rows = SWA_G * seq
    first = lax.broadcasted_iota(jnp.int32, (SWA_KV, wb), 1) < seq
    q = q_ref[...].astype(F32)
    kvn = jnp.concatenate([kvn_ref[...], jnp.zeros((LANES - group * seq, KV_COLS), F32)], axis=0)
    kn_all, vn_all = kvn[:, :SWA_KV].T, kvn[:, SWA_KV:].T

    scores, values = [], []
    for e in range(group):
        rows = slice(e * seq, (e + 1) * seq)
        k_old, v_old = ck_ref[e], cv_ref[e]
        shift = (LANES - e * seq) % LANES
        k_new = pltpu.roll(kn_all, shift, 1) if shift else kn_all
        v_new = pltpu.roll(vn_all, shift, 1) if shift else vn_all
        nk_ref[e] = pltpu.roll(jnp.where(first, k_new, k_old), wb - seq, 1)
        nv_ref[e] = pltpu.roll(jnp.where(first, v_new, v_old), wb - seq, 1)
        ops = _swa_operands_t(jnp.concatenate([k_old, k_new], axis=1),
                              jnp.concatenate([v_old, v_new], axis=1))
        for kh in range(SWA_KVH):
            k_dup, v_even, v_odd = ops[kh]
            scores.append(_swa_scores(q[rows], kh, k_dup, keys_on_lanes=True))
            values.append((v_even, v_odd))
    s_all = jnp.concatenate(scores, axis=0)
    i = lax.broadcasted_iota(jnp.int32, s_all.shape, 0) & (seq - 1)
    j = lax.broadcasted_iota(jnp.int32, s_all.shape, 1)
    diff = i + wb - j
    sink_kh = [jnp.concatenate([jnp.full((seq, 1), sink_ref[_swa_head_of_block(kh, blk)], F32)
                                for blk in range(SWA_G)], axis=0) for kh in range(SWA_KVH)]
    p_all, rinv_all = _swa_soft_head(s_all, (diff >= 0) & (diff <= WINDOW),
                                     jnp.concatenate(sink_kh * group, axis=0))

    lo = lax.broadcasted_iota(jnp.int32, (seq, LANES), 1) < SWA_HD
    outs = [[] for _ in range(group)]
    for idx, (v_even, v_odd) in enumerate(values):
        base = idx * blk_rows
        half = pairs * seq
        o = (_dot_nt(p_all[base:base + half], v_even)
             + _dot_nt(p_all[base + half:base + blk_rows], v_odd))
        for p in range(pairs):
            r_even = rinv_all[base + p * seq:base + (p + 1) * seq]
            r_odd = rinv_all[base + half + p * seq:base + half + (p + 1) * seq]
            outs[idx // SWA_KVH].append(o[p * seq:(p + 1) * seq] * jnp.where(lo, r_even, r_odd))
    o = jnp.concatenate([jnp.concatenate(row, axis=1) for row in outs], axis=0)
    o_ref[...] = (o * _silu(sz_ref[...].astype(F32))).astype(BF16)


def _swa_sample(z, kv, sink, ck, cv, prev, layer, batch, seq):
    group = 8
    rows = group * seq
    wb = ck.shape[3]
    assert wb == LANES, "the sliding window is one lane tile wide"
    cache_spec = pl.BlockSpec((None, group, SWA_KV, wb), lambda i: (layer, i, 0, 0))
    in_specs = [
        pl.BlockSpec(memory_space=pltpu.SMEM),
        pl.BlockSpec((rows, SWA_W), lambda i: (i, Z_SQ // SWA_W)),
        pl.BlockSpec((rows, SWA_W), lambda i: (i, Z_SZ // SWA_W)),
        pl.BlockSpec((rows, KV_COLS), lambda i: (i, 0)),
        cache_spec, cache_spec,
    ]
    args = [sink, z, z, kv, ck, cv]
    aliases = {}
    if prev is not None:
        aliases = {len(args): 1, len(args) + 1: 2}
        in_specs += [pl.BlockSpec(memory_space=pl.ANY)] * 2
        args += list(prev)
    return pl.pallas_call(
        functools.partial(_swa_sample_kernel, group=group, seq=seq),
        grid=(batch // group,),
        in_specs=in_specs,
        out_specs=[pl.BlockSpec((rows, SWA_W), lambda i: (i, 0)), cache_spec, cache_spec],
        out_shape=[
            jax.ShapeDtypeStruct((batch * seq, SWA_W), BF16),
            jax.ShapeDtypeStruct(ck.shape, F32),
            jax.ShapeDtypeStruct(cv.shape, F32),
        ],
        input_output_aliases=aliases,
        compiler_params=_params("parallel"),
        name="swa_sample",
    )(*args)


MEM_HALVES = X_HD // LANES
MEM_TILE_ROWS = X_H * MEM_HALVES


def _mem_rows_view(x):
    lead = x.shape[:-3]
    n = len(lead)
    x = x.reshape(*lead, MEM_LEN, X_H, MEM_HALVES, LANES)
    x = x.transpose(*range(n), n, n + 2, n + 1, n + 3)
    return x.reshape(*lead, MEM_LEN * MEM_TILE_ROWS, LANES)


def _mem_from_rows_view(x):
    lead = x.shape[:-2]
    n = len(lead)
    x = x.reshape(*lead, MEM_LEN, MEM_HALVES, X_H, LANES)
    x = x.transpose(*range(n), n, n + 2, n + 1, n + 3)
    return x.reshape(*lead, MEM_LEN, X_H, X_HD)


def _mem_head_rows(head, half):
    return pl.ds(half * X_H + head, MEM_LEN, stride=MEM_TILE_ROWS)


def _mem_kv_kernel(x_ref, g_ref, w_ref, *rest):
    kb_ref, vb_ref, kn_ref, vn_ref = rest[-4:]
    h = _rmsnorm_rows(x_ref[...], g_ref[...]).astype(BF16)
    for t, (b_ref, n_ref) in enumerate(((kb_ref, kn_ref), (vb_ref, vn_ref))):
        for head in range(X_H):
            c0 = head * X_HD
            res = _dot(h, w_ref[:, t * X_W + c0:t * X_W + c0 + X_HD])
            b_ref[:, c0:c0 + X_HD] = res.astype(BF16)
            for half in range(MEM_HALVES):
                n_ref[_mem_head_rows(head, half), :] = res[:, half * LANES:(half + 1) * LANES]


def _mem_kv(mem2d, g, w, prev, layer, depth, batch):
    d = mem2d.shape[1]
    rows_spec = pl.BlockSpec((None, None, MEM_LEN * MEM_TILE_ROWS, LANES), lambda i: (layer, i, 0, 0))
    rows_shape = jax.ShapeDtypeStruct((depth, batch, MEM_LEN * MEM_TILE_ROWS, LANES), F32)
    dense_spec = pl.BlockSpec((MEM_LEN, X_W), lambda i: (i, 0))
    dense_shape = jax.ShapeDtypeStruct((batch * MEM_LEN, X_W), BF16)
    in_specs = [pl.BlockSpec((MEM_LEN, d), lambda i: (i, 0)),
                pl.BlockSpec((1, d), lambda i: (0, 0)),
                pl.BlockSpec((d, 2 * X_W), lambda i: (0, 0))]
    args = [mem2d, g, w]
    aliases = {}
    if prev is not None:
        aliases = {len(args): 2, len(args) + 1: 3}
        in_specs += [pl.BlockSpec(memory_space=pl.ANY)] * 2
        args += list(prev)
    return pl.pallas_call(
        _mem_kv_kernel,
        grid=(batch,),
        in_specs=in_specs,
        out_specs=[dense_spec, dense_spec, rows_spec, rows_spec],
        out_shape=[dense_shape, dense_shape, rows_shape, rows_shape],
        input_output_aliases=aliases,
        compiler_params=_params("parallel"),
        name="mem_kv",
    )(*args)


def _mem_softmax(s):
    p = jnp.exp(s - jnp.max(s, axis=-1, keepdims=True))
    return p, 1.0 / jnp.sum(p, axis=-1, keepdims=True)


def _mem_sample_kernel(q_ref, mz_ref, mk_ref, mv_ref, o_ref, *, group, seq):
    def head_tile(ref, e, h):
        halves = [ref[e, _mem_head_rows(h, half), :] for half in range(MEM_HALVES)]
        return jnp.concatenate(halves, axis=1).astype(BF16)

    q = q_ref[...].astype(F32)
    pairs = [(e, h) for e in range(group) for h in range(X_H)]
    scores = []
    for e, h in pairs:
        q_eh = _pad_rows(q[e * seq:(e + 1) * seq, h * X_HD:(h + 1) * X_HD], BF16_ROWS).astype(BF16)
        scores.append(_dot_nt(q_eh, head_tile(mk_ref, e, h))[:seq])
    p, rinv = _mem_softmax(jnp.concatenate(scores, axis=0) * (X_HD ** -0.5))
    outs = [[] for _ in range(group)]
    for i, (e, h) in enumerate(pairs):
        p_eh = _pad_rows(p[i * seq:(i + 1) * seq], BF16_ROWS).astype(BF16)
        outs[e].append(_dot(p_eh, head_tile(mv_ref, e, h))[:seq] * rinv[i * seq:(i + 1) * seq])
    o = jnp.concatenate([jnp.concatenate(row, axis=1) for row in outs], axis=0)
    o_ref[...] = (o * _silu(mz_ref[...].astype(F32))).astype(BF16)


def _mem_sample(z, mk, mv, layer, batch, seq):
    group = 4
    rows = group * seq
    mem_spec = pl.BlockSpec((None, group, MEM_LEN * MEM_TILE_ROWS, LANES),
                            lambda i: (layer, i, 0, 0))
    return pl.pallas_call(
        functools.partial(_mem_sample_kernel, group=group, seq=seq),
        grid=(batch // group,),
        in_specs=[
            pl.BlockSpec((rows, X_W), lambda i: (i, Z_MQ // X_W)),
            pl.BlockSpec((rows, X_W), lambda i: (i, Z_MZ // X_W)),
            mem_spec, mem_spec,
        ],
        out_specs=pl.BlockSpec((rows, X_W), lambda i: (i, 0)),
        out_shape=jax.ShapeDtypeStruct((batch * seq, X_W), BF16),
        compiler_params=_params("parallel"),
        name="mem_sample",
    )(z, z, mk, mv)


FRONT_ROWS = 256
FRONT_GLA_CHUNK = 2 * GLA_CHUNK

def _front_kernel(sink_ref, x_ref, g_ref, w_ref, wa_ref, ba_ref, gout_ref, mk_ref, mv_ref, *rest):
    (ogla_ref, oswa_ref, omem_ref, mg_ref, kvl_ref, s_ref,
     gq_scr, gk_scr, gv_scr, gz_scr, ga_scr, sq_scr, sz_scr, mq_scr, mz_scr, kv_scr, kvp_scr) = rest[-17:]
    n = pl.program_id(1)
    tm = FRONT_ROWS

    @pl.when(n == 0)
    def _():
        s_ref[...] = jnp.zeros_like(s_ref)
        kvp_scr[...] = jnp.zeros_like(kvp_scr)

    h = _rmsnorm_rows(x_ref[...], g_ref[...]).astype(BF16)

    def project(chunk):
        dst, dst_c0, w_r0, width = chunk
        if dst is ga_scr:
            dst[...] = _project_decay(h, w_ref).astype(dst.dtype)
        else:
            dst[:, dst_c0:dst_c0 + width] = _project(h, w_ref, w_r0, width).astype(dst.dtype)

    def chunks_of(dst, w_r0, width):
        return [(dst, c, w_r0 + c, min(MXU_COLS, width - c)) for c in range(0, width, MXU_COLS)]

    swa_cols = chunks_of(kv_scr, W_SK, KV_COLS) + chunks_of(sq_scr, W_SQ, SWA_W) \
        + chunks_of(sz_scr, W_SZ, SWA_W)
    gla_cols = (chunks_of(ga_scr, W_GA, LANES) + chunks_of(gq_scr, W_GQ, GLA_QK)
                + chunks_of(gk_scr, W_GK, GLA_QK) + chunks_of(gv_scr, W_GV, GLA_V)
                + chunks_of(gz_scr, W_GZ, GLA_V))
    mem_cols = chunks_of(mq_scr, W_MQ, X_W) + chunks_of(mz_scr, W_MZ, X_W)
    merge_cols = chunks_of(mg_ref, W_MG, N_BRANCH * D_MODEL)

    i = lax.broadcasted_iota(jnp.int32, (WINDOW, 2 * WINDOW), 0)
    j = lax.broadcasted_iota(jnp.int32, (WINDOW, 2 * WINDOW), 1)
    diff = i + WINDOW - j
    band = (diff >= 0) & (diff <= WINDOW)
    swa = {}

    def swa_start(blk):
        prev = kvp_scr[...] if blk == 0 else kv_scr[(blk - 1) * WINDOW:blk * WINDOW, :]
        kvb = jnp.concatenate([prev, kv_scr[blk * WINDOW:(blk + 1) * WINDOW, :]], axis=0)
        swa["ops"] = _swa_operands(kvb[:, :SWA_KV], kvb[:, SWA_KV:])
        swa["valid"] = band & (j >= jnp.where(n > 0, 0, WINDOW)) if blk == 0 else band

    def swa_scores(blk, kh):
        swa["s"] = _swa_scores(sq_scr[blk * WINDOW:(blk + 1) * WINDOW, :], kh, swa["ops"][kh][0])
        swa["soft"] = []

    def swa_soft(kh, hb):
        swa["soft"].append(_swa_soft_head(swa["s"][hb * WINDOW:(hb + 1) * WINDOW], swa["valid"],
                                          sink_ref[_swa_head_of_block(kh, hb)]))

    def swa_out(blk, kh):
        _, v_even, v_odd = swa["ops"][kh]
        o = _swa_out([p for p, _ in swa["soft"]], [r for _, r in swa["soft"]], v_even, v_odd)
        rows = slice(blk * WINDOW, (blk + 1) * WINDOW)
        cols = slice(kh * SWA_G * SWA_HD, (kh + 1) * SWA_G * SWA_HD)
        oswa_ref[rows, cols] = (o * _silu(sz_scr[rows, cols].astype(F32))).astype(BF16)

    swa_steps = []
    for blk in range(tm // WINDOW):
        swa_steps.append((functools.partial(swa_start, blk), 50))
        for kh in range(SWA_KVH):
            swa_steps.append((functools.partial(swa_scores, blk, kh), 30))
            swa_steps += [(functools.partial(swa_soft, kh, hb), 40) for hb in range(SWA_G)]
            swa_steps.append((functools.partial(swa_out, blk, kh), 75))

    mem = {}

    def mem_scores(hd):
        cols = slice(hd * X_HD, (hd + 1) * X_HD)
        p, rinv = _mem_softmax(_dot_nt(mq_scr[:, cols], mk_ref[:, cols]) * (X_HD ** -0.5))
        mem["p"], mem["rinv"] = p.astype(BF16), rinv

    def mem_out(hd):
        cols = slice(hd * X_HD, (hd + 1) * X_HD)
        o = _dot(mem["p"], mv_ref[:, cols]) * mem["rinv"]
        omem_ref[:, cols] = (o * _silu(mz_scr[:, cols].astype(F32))).astype(BF16)

    mem_steps = []
    for hd in range(X_H):
        mem_steps += [(functools.partial(mem_scores, hd), 130), (functools.partial(mem_out, hd), 160)]

    gla_steps = _gla_thunks(gq_scr, gk_scr, gv_scr, gz_scr, ga_scr, wa_ref, ba_ref, gout_ref, None,
                            ogla_ref, s_ref, chunk=FRONT_GLA_CHUNK, nchunks=tm // FRONT_GLA_CHUNK,
                            carry=True, anchored=True)

    for chunk in swa_cols:
        project(chunk)
    chunks = gla_cols + mem_cols + merge_cols
    steps = [(t, w, 0) for t, w in swa_steps]
    steps += [(t, w, len(gla_cols)) for t, w in gla_steps]
    steps += [(t, w, len(gla_cols) + len(mem_cols)) for t, w in mem_steps]
    _interleave(steps, chunks, project)

    kv_tail = kv_scr[tm - WINDOW:, :]
    kvp_scr[...] = kv_tail
    kvl_ref[...] = kv_tail.T


def _front(x2d, g, w, sink, wa, ba, gout, mk, mv, prev_state, layer, depth, batch, seq):
    t, d = x2d.shape
    tm = FRONT_ROWS
    nblk = seq // tm
    row = lambda width: pl.BlockSpec((tm, width), lambda b, n: (b * nblk + n, 0))
    const = lambda shape: pl.BlockSpec(shape, lambda b, n: (0,) * len(shape))
    mem_spec = pl.BlockSpec((MEM_LEN, X_W), lambda b, n: (b, 0))
    state_spec = pl.BlockSpec((None, None, GLA_H, GLA_DK, GLA_DV), lambda b, n: (layer, b, 0, 0, 0))
    in_specs = [pl.BlockSpec(memory_space=pltpu.SMEM), row(d), const((1, d)),
                pl.BlockSpec(w.shape, lambda b, n: (0, 0), pipeline_mode=pl.Buffered(1)),
                const((LANES, GLA_QK)), const((1, GLA_QK)), const((1, GLA_DV)), mem_spec, mem_spec]
    args = [sink, x2d, g, w, wa, ba, gout, mk, mv]
    aliases = {}
    if prev_state is not None:
        aliases = {len(args): 5}
        in_specs.append(pl.BlockSpec(memory_space=pl.ANY))
        args.append(prev_state)
    scratch = [pltpu.VMEM((tm, width), BF16) for width in
               (GLA_QK, GLA_QK, GLA_V, GLA_V, LANES, SWA_W, SWA_W, X_W, X_W)]
    scratch += [pltpu.VMEM((tm, KV_COLS), F32), pltpu.VMEM((WINDOW, KV_COLS), F32)]
    return pl.pallas_call(
        _front_kernel,
        grid=(batch, nblk),
        in_specs=in_specs,
        out_specs=[row(GLA_V), row(SWA_W), row(X_W), row(N_BRANCH * D_MODEL),
                   pl.BlockSpec((None, KV_COLS, WINDOW), lambda b, n: (b, 0, 0)), state_spec],
        out_shape=[jax.ShapeDtypeStruct((t, GLA_V), BF16), jax.ShapeDtypeStruct((t, SWA_W), BF16),
                   jax.ShapeDtypeStruct((t, X_W), BF16),
                   jax.ShapeDtypeStruct((t, N_BRANCH * D_MODEL), BF16),
                   jax.ShapeDtypeStruct((batch, KV_COLS, WINDOW), F32),
                   jax.ShapeDtypeStruct((depth, batch, GLA_H, GLA_DK, GLA_DV), F32)],
        scratch_shapes=scratch,
        input_output_aliases=aliases,
        compiler_params=_params("parallel", "arbitrary"),
        name="front",
    )(*args)


def _merge_kernel(x_ref, a_ref, s_ref, m_ref, ga_ref, gs_ref, gm_ref, wa_ref, ws_ref, wm_ref,
                  wo_ref, *rest, final):
    if final:
        gf_ref, y_ref = rest
    else:
        (y_ref,) = rest
    merged = (_sigmoid(ga_ref[...].astype(F32)) * _dot(a_ref[...], wa_ref[...])
              + _sigmoid(gs_ref[...].astype(F32)) * _dot(s_ref[...], ws_ref[...])
              + _sigmoid(gm_ref[...].astype(F32)) * _dot(m_ref[...], wm_ref[...]))
    y = x_ref[...] + _dot(merged.astype(BF16), wo_ref[...])
    if final:
        y = _rmsnorm_rows(y, gf_ref[...])
    y_ref[...] = y


def _merge(x2d, o_gla, o_swa, o_mem, gates, gate_col0, w_gla, w_swa, w_mem, w_out, g_final):
    t, d = x2d.shape
    tm = _row_tile(t)
    final = g_final is not None
    row = lambda width: pl.BlockSpec((tm, width), lambda i: (i, 0))
    gate = lambda k: pl.BlockSpec((tm, d), lambda i: (i, gate_col0 // d + k))
    weight = pl.BlockSpec((d, d), lambda i: (0, 0))
    in_specs = [row(d), row(d), row(d), row(d), gate(0), gate(1), gate(2),
                weight, weight, weight, weight]
    args = [x2d, o_gla, o_swa, o_mem, gates, gates, gates, w_gla, w_swa, w_mem, w_out]
    if final:
        in_specs.append(pl.BlockSpec((1, d), lambda i: (0, 0)))
        args.append(g_final)
    return pl.pallas_call(
        functools.partial(_merge_kernel, final=final),
        grid=(t // tm,),
        in_specs=in_specs,
        out_specs=row(d),
        out_shape=jax.ShapeDtypeStruct((t, d), F32),
        compiler_params=_params("parallel"),
        name="merge_final" if final else "merge",
    )(*args)


def kernel(x_prompt, x_sample, mem_prompt, state_gla, cache_swa_k, cache_swa_v, cache_mem_k,
           cache_mem_v, g_norm, w_in, w_gla_a, b_gla_a, g_gla_out, swa_sink, g_mem, w_mem_kv,
           w_br_gla, w_br_swa, w_br_mem, w_out, g_final):
    depth = w_in.shape[0]
    batch, seq, d = x_prompt.shape
    dec_batch, dec_seq, _ = x_sample.shape
    wb = cache_swa_k.shape[2]

    w_in_r = w_in.transpose(0, 2, 1).astype(BF16)
    w_a = jnp.pad(w_gla_a, ((0, 0), (0, LANES - GLA_RANK), (0, 0))).astype(BF16)
    w_mem_b = w_mem_kv.astype(BF16)
    w_g, w_s, w_m, w_o = (w.astype(BF16) for w in (w_br_gla, w_br_swa, w_br_mem, w_out))
    g_fin = g_final.reshape(1, d)

    yp = x_prompt.reshape(batch * seq, d)
    ys = x_sample.reshape(dec_batch * dec_seq, d)
    mem2d = mem_prompt.reshape(batch * MEM_LEN, d)
    swa_t = lambda c: c.transpose(0, 1, 3, 4, 2).reshape(depth, dec_batch, SWA_KV, wb)
    ck_all, cv_all = swa_t(cache_swa_k), swa_t(cache_swa_v)
    cmk_all = _mem_rows_view(cache_mem_k)
    cmv_all = _mem_rows_view(cache_mem_v)
    gla_p = gla_s = swa_new = mem_new = None
    kp_l, vp_l = [], []
    for l in range(depth):
        last = l == depth - 1
        gn = g_norm[l].reshape(1, d)
        ba = b_gla_a[l].reshape(1, GLA_QK)
        gout = g_gla_out[l].reshape(1, GLA_DV)
        sink = swa_sink[l]
        branch_w = (w_g[l], w_s[l], w_m[l], w_o[l], g_fin if last else None)

        mk, mv, *mem_new = _mem_kv(mem2d, g_mem[l].reshape(1, d), w_mem_b[l], mem_new, l, depth,
                                   batch)
        o_gla, o_swa, o_mem, gates, kv_tail, gla_p = _front(
            yp, gn, w_in_r[l], sink, w_a[l], ba, gout, mk, mv, gla_p, l, depth, batch, seq)
        yp = _merge(yp, o_gla, o_swa, o_mem, gates, 0, *branch_w)
        kv_heads = kv_tail.reshape(batch, 2, SWA_KVH, SWA_HD, WINDOW).transpose(0, 1, 4, 2, 3)
        kp_l.append(kv_heads[:, 0])
        vp_l.append(kv_heads[:, 1])

        z, kv = _in_proj(ys, gn, w_in_r[l])
        o_gla, gla_s = _gla_sample(z, w_a[l], ba, gout, state_gla, gla_s, l, dec_batch, dec_seq)
        o_swa, *swa_new = _swa_sample(z, kv, sink, ck_all, cv_all, swa_new, l, dec_batch, dec_seq)
        o_mem = _mem_sample(z, cmk_all, cmv_all, l, dec_batch, dec_seq)
        ys = _merge(ys, o_gla, o_swa, o_mem, z, Z_MG, *branch_w)

    k_s, v_s = (c.reshape(depth, dec_batch, SWA_KVH, SWA_HD, wb).transpose(0, 1, 4, 2, 3)
                for c in swa_new)
    mk_p, mv_p = (_mem_from_rows_view(c) for c in mem_new)
    return (yp.reshape(batch, seq, d), ys.reshape(dec_batch, dec_seq, d), gla_p,
            jnp.stack(kp_l), jnp.stack(vp_l), mk_p, mv_p, gla_s, k_s, v_s)
```

```python
import functools

import jax
import jax.numpy as jnp
from jax import lax
from jax.experimental import pallas as pl
from jax.experimental.pallas import tpu as pltpu

F32 = jnp.float32
BF16 = jnp.bfloat16

D_MODEL = 1024
GLA_H = 4
GLA_DK = 128
GLA_DV = 256
GLA_QK = GLA_H * GLA_DK
GLA_V = GLA_H * GLA_DV
GLA_RANK = 16
GLA_TAU = 16.0
GLA_CHUNK = 32
SWA_HD = 64
SWA_H = 16
SWA_KVH = 2
SWA_G = SWA_H // SWA_KVH
SWA_W = SWA_H * SWA_HD
SWA_KV = SWA_KVH * SWA_HD
WINDOW = 128
MEM_LEN = 256
X_H = 4
X_HD = 256
X_W = X_H * X_HD
N_BRANCH = 3
IN_SPLITS = (GLA_QK, GLA_QK, GLA_V, GLA_V, GLA_RANK, SWA_W, SWA_KV, SWA_KV, SWA_W, X_W, X_W,
             N_BRANCH * D_MODEL)
EPS = 1e-6

LANES = 128
BF16_ROWS = 16
MXU_COLS = 256
VMEM_LIMIT = 56 * 1024 * 1024

Z_GQ = 0
Z_GK = Z_GQ + GLA_QK
Z_GV = Z_GK + GLA_QK
Z_GZ = Z_GV + GLA_V
Z_SQ = Z_GZ + GLA_V
Z_SZ = Z_SQ + SWA_W
Z_MQ = Z_SZ + SWA_W
Z_MZ = Z_MQ + X_W
Z_MG = Z_MZ + X_W
Z_GA = Z_MG + N_BRANCH * D_MODEL
Z_COLS = Z_GA + LANES
KV_COLS = 2 * SWA_KV

(W_GQ, W_GK, W_GV, W_GZ, W_GA, W_SQ, W_SK, W_SV, W_SZ, W_MQ, W_MZ, W_MG) = (
    sum(IN_SPLITS[:i]) for i in range(len(IN_SPLITS)))
Z_PIECES = ((Z_GQ, W_GQ, GLA_QK), (Z_GK, W_GK, GLA_QK), (Z_GV, W_GV, GLA_V), (Z_GZ, W_GZ, GLA_V),
            (Z_SQ, W_SQ, SWA_W), (Z_SZ, W_SZ, SWA_W), (Z_MQ, W_MQ, X_W), (Z_MZ, W_MZ, X_W),
            (Z_MG, W_MG, N_BRANCH * D_MODEL))


def _project(h, w_ref, row0, width):
    return _dot_nt(h, w_ref[row0:row0 + width, :])


def _project_decay(h, w_ref):
    res = _project(h, w_ref, W_GA, LANES)
    lane = lax.broadcasted_iota(jnp.int32, res.shape, 1)
    return jnp.where(lane < GLA_RANK, res, 0.0)


def _row_tile(rows):
    return 512 if rows >= 4096 else 256


def _params(*semantics):
    return pltpu.CompilerParams(dimension_semantics=semantics, vmem_limit_bytes=VMEM_LIMIT)


def _sigmoid(x):
    return 0.5 * jnp.tanh(0.5 * x) + 0.5


def _silu(x):
    return x * _sigmoid(x)


def _log_sigmoid(x):
    return jnp.minimum(x, 0.0) - jnp.log(1.0 + jnp.exp(-jnp.abs(x)))


def _rmsnorm_rows(x, g):
    return x * lax.rsqrt(jnp.mean(x * x, axis=-1, keepdims=True) + EPS) * g


def _dot(a, b):
    return jnp.dot(a, b, preferred_element_type=F32)


def _dot_nt(a, b):
    return lax.dot_general(a, b, (((1,), (1,)), ((), ())), preferred_element_type=F32)


def _dot_tn(a, b):
    return lax.dot_general(a, b, (((0,), (0,)), ((), ())), preferred_element_type=F32)


def _pad_rows(x, rows):
    if x.shape[0] >= rows:
        return x
    return jnp.concatenate([x, jnp.zeros((rows - x.shape[0], x.shape[1]), x.dtype)], axis=0)


def _interleave(steps, chunks, run_chunk):
    total = sum(w for _, w, _ in steps)
    done, acc = 0, 0
    for thunk, weight, needs in steps:
        acc += weight
        upto = max(needs, min(len(chunks), -(-acc * len(chunks) // total)))
        for chunk in chunks[done:upto]:
            run_chunk(chunk)
        done = max(done, upto)
        thunk()


def _in_proj_kernel(x_ref, g_ref, w_ref, z_ref, kv_ref):
    h = _rmsnorm_rows(x_ref[...], g_ref[...]).astype(BF16)
    for z0, w0, width in Z_PIECES:
        for c in range(0, width, 512):
            z_ref[:, z0 + c:z0 + c + 512] = _project(h, w_ref, w0 + c, 512).astype(BF16)
    z_ref[:, Z_GA:Z_COLS] = _project_decay(h, w_ref).astype(BF16)
    kv_ref[...] = _project(h, w_ref, W_SK, KV_COLS)


def _layer_weight_spec(w, layer, resident=False):
    mode = dict(pipeline_mode=pl.Buffered(1)) if resident else {}
    return pl.BlockSpec((None,) + w.shape[1:], lambda *ids: (layer, 0, 0), **mode)


def _in_proj(x2d, g, w, layer):
    t, d = x2d.shape
    tm = _row_tile(t)
    return pl.pallas_call(
        _in_proj_kernel,
        grid=(t // tm,),
        in_specs=[
            pl.BlockSpec((tm, d), lambda i: (i, 0)),
            pl.BlockSpec((1, d), lambda i: (0, 0)),
            _layer_weight_spec(w, layer, resident=True),
        ],
        out_specs=[pl.BlockSpec((tm, Z_COLS), lambda i: (i, 0)),
                   pl.BlockSpec((tm, KV_COLS), lambda i: (i, 0))],
        out_shape=[jax.ShapeDtypeStruct((t, Z_COLS), BF16), jax.ShapeDtypeStruct((t, KV_COLS), F32)],
        compiler_params=_params("parallel"),
        name="in_proj",
    )(x2d, g, w)


def _chunk_cumsum(g, chunk):
    pos = lax.broadcasted_iota(jnp.int32, g.shape, 0) & (chunk - 1)
    b = g
    shift = 1
    while shift < chunk:
        b = b + jnp.where(pos >= shift, pltpu.roll(b, shift, 0), 0.0)
        shift *= 2
    return b


def _gla_thunks(q_ref, k_ref, v_ref, gz_ref, ga_ref, wa_ref, ba_ref, gout_ref, s0_ref, o_ref,
                s_ref, *, chunk, nchunks, carry, anchored):
    rows = chunk * nchunks
    mm_rows = max(chunk, BF16_ROWS)
    heads = range(GLA_H)
    lanes = [slice(h * GLA_DK, (h + 1) * GLA_DK) for h in heads]
    vl = [slice(h * GLA_DV, (h + 1) * GLA_DV) for h in heads]
    per_head = lambda: [None] * GLA_H
    st = {name: per_head() for name in ("b", "qb", "kd", "qb_bf", "kd_bf", "qa_bf", "ka_bf",
                                        "dec_cols", "o_intra")}
    st["o_inter"] = [[] for _ in heads]

    def chunk_row(x, row):
        picked = [x[j * chunk + row:j * chunk + row + 1, :] for j in range(nchunks)]
        return picked, jnp.concatenate([jnp.broadcast_to(r, (chunk, x.shape[1])) for r in picked],
                                       axis=0)

    small = chunk < BF16_ROWS

    def chunk_rows(x_f32, x_bf, sl):
        return _pad_rows(x_f32[sl], mm_rows).astype(BF16) if small else x_bf[sl]

    def prepare():
        r_idx = lax.broadcasted_iota(jnp.int32, (rows, rows), 0)
        c_idx = lax.broadcasted_iota(jnp.int32, (rows, rows), 1)
        diff = r_idx - c_idx
        st["mask"] = (diff >= 0) & (diff <= (r_idx & (chunk - 1)))
        st["v_f32"] = v_ref[...].astype(F32) if small else None
        if carry:
            st["states"] = [s_ref[h] for h in heads]

    def decay(h):
        x = _dot(ga_ref[...], wa_ref[:, lanes[h]]) + ba_ref[:, lanes[h]]
        st["b"][h] = _chunk_cumsum(_log_sigmoid(x) * (1.0 / GLA_TAU), chunk)

    def factors(h):
        b = st["b"][h]
        b_last, b_last_full = chunk_row(b, chunk - 1)
        b_last_rows = jnp.concatenate(b_last + [jnp.zeros((LANES - nchunks, GLA_DK), F32)], axis=0)
        st["dec_cols"][h] = jnp.exp(b_last_rows.T)
        q = q_ref[:, lanes[h]].astype(F32) * (GLA_DK ** -0.5)
        k = k_ref[:, lanes[h]].astype(F32)
        qb = q * jnp.exp(b)
        kd = k * jnp.exp(b_last_full - b)
        st["qb"][h], st["kd"][h] = (qb, kd) if small else (None, None)
        st["qb_bf"][h], st["kd_bf"][h] = qb.astype(BF16), kd.astype(BF16)
        if anchored:
            _, b_mid_full = chunk_row(b, chunk // 2 - 1)
            st["qa_bf"][h] = (q * jnp.exp(b - b_mid_full)).astype(BF16)
            st["ka_bf"][h] = (k * jnp.exp(b_mid_full - b)).astype(BF16)
        else:
            st["qa_bf"][h] = st["qb_bf"][h]
            st["ka_bf"][h] = (k * jnp.exp(-b)).astype(BF16)

    def intra(h):
        a = jnp.where(st["mask"], _dot_nt(st["qa_bf"][h], st["ka_bf"][h]), 0.0)
        st["o_intra"][h] = _dot(a.astype(BF16), v_ref[:, vl[h]])

    def step(j):
        sl = slice(j * chunk, (j + 1) * chunk)
        for h in heads:
            state = st["states"][h] if carry else s0_ref[j, h]
            qb_j = chunk_rows(st["qb"][h], st["qb_bf"][h], sl)
            st["o_inter"][h].append(_dot(qb_j, state.astype(BF16))[:chunk])
            if small:
                v_j = _pad_rows(st["v_f32"][sl, vl[h]], mm_rows).astype(BF16)
            else:
                v_j = v_ref[sl, vl[h]]
            kd_j = chunk_rows(st["kd"][h], st["kd_bf"][h], sl)
            new_state = st["dec_cols"][h][:, j:j + 1] * state + _dot_tn(kd_j, v_j)
            if carry:
                st["states"][h] = new_state
            else:
                s_ref[j, h] = new_state

    def epilogue(h):
        if carry:
            s_ref[h] = st["states"][h]
        o = st["o_intra"][h] + jnp.concatenate(st["o_inter"][h], axis=0)
        gate = _silu(gz_ref[:, vl[h]].astype(F32))
        o_ref[:, vl[h]] = (_rmsnorm_rows(o, gout_ref[...]) * gate).astype(BF16)

    steps = [(prepare, 100)]
    steps += [(functools.partial(decay, h), 230) for h in heads]
    steps += [(functools.partial(factors, h), 170) for h in heads]
    steps += [(functools.partial(intra, h), 60) for h in heads]
    steps += [(functools.partial(step, j), 130) for j in range(nchunks)]
    steps += [(functools.partial(epilogue, h), 190) for h in heads]
    return steps


def _gla_sample_kernel(q_ref, k_ref, v_ref, gz_ref, ga_ref, wa_ref, ba_ref, gout_ref, s0_ref, *rest,
                       seq, group):
    o_ref, s_ref = rest[-2:]
    for thunk, _ in _gla_thunks(q_ref, k_ref, v_ref, gz_ref, ga_ref, wa_ref, ba_ref, gout_ref,
                                s0_ref, o_ref, s_ref, chunk=seq, nchunks=group, carry=False,
                                anchored=False):
        thunk()


def _gla_sample(z, wa, ba, gout, s0, prev, layer, batch, seq):
    group = 8
    rows = group * seq
    state_spec = pl.BlockSpec((None, group, GLA_H, GLA_DK, GLA_DV), lambda i: (layer, i, 0, 0, 0))
    zspec = lambda width, col0: pl.BlockSpec((rows, width), lambda i: (i, col0 // width))
    const = lambda shape: pl.BlockSpec(shape, lambda i: (0,) * len(shape))
    in_specs = [zspec(GLA_QK, Z_GQ), zspec(GLA_QK, Z_GK), zspec(GLA_V, Z_GV), zspec(GLA_V, Z_GZ),
                zspec(LANES, Z_GA), _layer_weight_spec(wa, layer), const((1, GLA_QK)),
                const((1, GLA_DV)), state_spec]
    args = [z, z, z, z, z, wa, ba, gout, s0]
    aliases = {}
    if prev is not None:
        aliases = {len(args): 1}
        in_specs.append(pl.BlockSpec(memory_space=pl.ANY))
        args.append(prev)
    return pl.pallas_call(
        functools.partial(_gla_sample_kernel, seq=seq, group=group),
        grid=(batch // group,),
        in_specs=in_specs,
        out_specs=[pl.BlockSpec((rows, GLA_V), lambda i: (i, 0)), state_spec],
        out_shape=[jax.ShapeDtypeStruct((batch * seq, GLA_V), BF16),
                   jax.ShapeDtypeStruct(s0.shape, F32)],
        input_output_aliases=aliases,
        compiler_params=_params("parallel"),
        name="gla_sample",
    )(*args)


def _swa_head_of_block(kv_head, blk):
    pairs = SWA_G // 2
    within = 2 * blk if blk < pairs else 2 * (blk - pairs) + 1
    return kv_head * SWA_G + within


def _swa_operands(kk, vv):
    lo = lax.broadcasted_iota(jnp.int32, kk.shape, 1) < SWA_HD
    kk_sw = pltpu.roll(kk, SWA_HD, 1)
    vv_sw = pltpu.roll(vv, SWA_HD, 1)
    ops = []
    for kh in range(SWA_KVH):
        k_this, k_other = (kk, kk_sw) if kh == 0 else (kk_sw, kk)
        v_this, v_other = (vv, vv_sw) if kh == 0 else (vv_sw, vv)
        ops.append((jnp.where(lo, k_this, k_other).astype(BF16),
                    jnp.where(lo, v_this, 0.0).astype(BF16),
                    jnp.where(lo, 0.0, v_other).astype(BF16)))
    return ops


def _swa_operands_t(kk_t, vv_t):
    zeros = jnp.zeros((SWA_HD, kk_t.shape[1]), F32)
    ops = []
    for kh in range(SWA_KVH):
        k_h = kk_t[kh * SWA_HD:(kh + 1) * SWA_HD]
        v_h = vv_t[kh * SWA_HD:(kh + 1) * SWA_HD]
        ops.append((jnp.concatenate([k_h, k_h], axis=0).astype(BF16),
                    jnp.concatenate([v_h, zeros], axis=0).astype(BF16),
                    jnp.concatenate([zeros, v_h], axis=0).astype(BF16)))
    return ops


def _swa_scores(q, kh, k_dup, keys_on_lanes=False):
    nq = q.shape[0]
    lo = lax.broadcasted_iota(jnp.int32, (nq, LANES), 1) < SWA_HD
    even, odd = [], []
    for p in range(SWA_G // 2):
        c0 = kh * SWA_G * SWA_HD + p * LANES
        q_pair = q[:, c0:c0 + LANES]
        even.append(jnp.where(lo, q_pair, jnp.zeros_like(q_pair)))
        odd.append(jnp.where(lo, jnp.zeros_like(q_pair), q_pair))
    lhs = jnp.concatenate(even + odd, axis=0).astype(BF16)
    s = _dot(lhs, k_dup) if keys_on_lanes else _dot_nt(lhs, k_dup)
    return s * (SWA_HD ** -0.5)


def _swa_soft_head(s_h, valid, sink):
    s_h = jnp.where(valid, s_h, -jnp.inf)
    m = jnp.maximum(jnp.max(s_h, axis=-1, keepdims=True), sink)
    p_h = jnp.exp(s_h - m)
    den = jnp.sum(p_h, axis=-1, keepdims=True) + jnp.exp(sink - m)
    if p_h.shape[0] % BF16_ROWS == 0:
        p_h = p_h.astype(BF16)
    return p_h, 1.0 / den


def _swa_out(probs, rinv, v_even, v_odd):
    pairs = SWA_G // 2
    nq = probs[0].shape[0]
    lo = lax.broadcasted_iota(jnp.int32, (nq, LANES), 1) < SWA_HD
    p_even = jnp.concatenate(probs[:pairs], axis=0).astype(BF16)
    p_odd = jnp.concatenate(probs[pairs:], axis=0).astype(BF16)
    o = _dot(p_even, v_even) + _dot(p_odd, v_odd)
    outs = [o[p * nq:(p + 1) * nq] * jnp.where(lo, rinv[p], rinv[pairs + p]) for p in range(pairs)]
    return jnp.concatenate(outs, axis=1)


def _swa_sample_kernel(sink_ref, q_ref, sz_ref, kvn_ref, ck_ref, cv_ref, *rest, group, seq):
    o_ref, nk_ref, nv_ref = rest[-3:]
    wb = ck_ref.shape[2]
    nk = 2 * wb
    pairs = SWA_G // 2
    blk_rows = SWA_G * seq
    first = lax.broadcasted_iota(jnp.int32, (SWA_KV, wb), 1) < seq
    q = q_ref[...].astype(F32)
    kvn = jnp.concatenate([kvn_ref[...], jnp.zeros((LANES - group * seq, KV_COLS), F32)], axis=0)
    kn_all, vn_all = kvn[:, :SWA_KV].T, kvn[:, SWA_KV:].T

    scores, values = [], []
    for e in range(group):
        rows = slice(e * seq, (e + 1) * seq)
        k_old, v_old = ck_ref[e], cv_ref[e]
        shift = (LANES - e * seq) % LANES
        k_new = pltpu.roll(kn_all, shift, 1) if shift else kn_all
        v_new = pltpu.roll(vn_all, shift, 1) if shift else vn_all
        nk_ref[e] = pltpu.roll(jnp.where(first, k_new, k_old), wb - seq, 1)
        nv_ref[e] = pltpu.roll(jnp.where(first, v_new, v_old), wb - seq, 1)
        ops = _swa_operands_t(jnp.concatenate([k_old, k_new], axis=1),
                              jnp.concatenate([v_old, v_new], axis=1))
        for kh in range(SWA_KVH):
            k_dup, v_even, v_odd = ops[kh]
            scores.append(_swa_scores(q[rows], kh, k_dup, keys_on_lanes=True))
            values.append((v_even, v_odd))
    s_all = jnp.concatenate(scores, axis=0)
    i = lax.broadcasted_iota(jnp.int32, s_all.shape, 0) & (seq - 1)
    j = lax.broadcasted_iota(jnp.int32, s_all.shape, 1)
    diff = i + wb - j
    sink_kh = [jnp.concatenate([jnp.full((seq, 1), sink_ref[_swa_head_of_block(kh, blk)], F32)
                                for blk in range(SWA_G)], axis=0) for kh in range(SWA_KVH)]
    p_all, rinv_all = _swa_soft_head(s_all, (diff >= 0) & (diff <= WINDOW),
                                     jnp.concatenate(sink_kh * group, axis=0))

    lo = lax.broadcasted_iota(jnp.int32, (seq, LANES), 1) < SWA_HD
    outs = [[] for _ in range(group)]
    for idx, (v_even, v_odd) in enumerate(values):
        base = idx * blk_rows
        half = pairs * seq
        o = (_dot_nt(p_all[base:base + half], v_even)
             + _dot_nt(p_all[base + half:base + blk_rows], v_odd))
        for p in range(pairs):
            r_even = rinv_all[base + p * seq:base + (p + 1) * seq]
            r_odd = rinv_all[base + half + p * seq:base + half + (p + 1) * seq]
            outs[idx // SWA_KVH].append(o[p * seq:(p + 1) * seq] * jnp.where(lo, r_even, r_odd))
    o = jnp.concatenate([jnp.concatenate(row, axis=1) for row in outs], axis=0)
    o_ref[...] = (o * _silu(sz_ref[...].astype(F32))).astype(BF16)


def _swa_sample(z, kv, sink, ck, cv, prev, layer, batch, seq):
    group = 8
    rows = group * seq
    wb = ck.shape[3]
    assert wb == LANES, "the sliding window is one lane tile wide"
    cache_spec = pl.BlockSpec((None, group, SWA_KV, wb), lambda i: (layer, i, 0, 0))
    in_specs = [
        pl.BlockSpec(memory_space=pltpu.SMEM),
        pl.BlockSpec((rows, SWA_W), lambda i: (i, Z_SQ // SWA_W)),
        pl.BlockSpec((rows, SWA_W), lambda i: (i, Z_SZ // SWA_W)),
        pl.BlockSpec((rows, KV_COLS), lambda i: (i, 0)),
        cache_spec, cache_spec,
    ]
    args = [sink, z, z, kv, ck, cv]
    aliases = {}
    if prev is not None:
        aliases = {len(args): 1, len(args) + 1: 2}
        in_specs += [pl.BlockSpec(memory_space=pl.ANY)] * 2
        args += list(prev)
    return pl.pallas_call(
        functools.partial(_swa_sample_kernel, group=group, seq=seq),
        grid=(batch // group,),
        in_specs=in_specs,
        out_specs=[pl.BlockSpec((rows, SWA_W), lambda i: (i, 0)), cache_spec, cache_spec],
        out_shape=[
            jax.ShapeDtypeStruct((batch * seq, SWA_W), BF16),
            jax.ShapeDtypeStruct(ck.shape, F32),
            jax.ShapeDtypeStruct(cv.shape, F32),
        ],
        input_output_aliases=aliases,
        compiler_params=_params("parallel"),
        name="swa_sample",
    )(*args)


MEM_HALVES = X_HD // LANES
MEM_TILE_ROWS = X_H * MEM_HALVES


def _mem_rows_view(x):
    lead = x.shape[:-3]
    n = len(lead)
    x = x.reshape(*lead, MEM_LEN, X_H, MEM_HALVES, LANES)
    x = x.transpose(*range(n), n, n + 2, n + 1, n + 3)
    return x.reshape(*lead, MEM_LEN * MEM_TILE_ROWS, LANES)


def _mem_from_rows_view(x):
    lead = x.shape[:-2]
    n = len(lead)
    x = x.reshape(*lead, MEM_LEN, MEM_HALVES, X_H, LANES)
    x = x.transpose(*range(n), n, n + 2, n + 1, n + 3)
    return x.reshape(*lead, MEM_LEN, X_H, X_HD)


def _mem_head_rows(head, half):
    return pl.ds(half * X_H + head, MEM_LEN, stride=MEM_TILE_ROWS)


def _mem_kv_kernel(x_ref, g_ref, w_ref, *rest):
    kb_ref, vb_ref, kn_ref, vn_ref = rest[-4:]
    h = _rmsnorm_rows(x_ref[...], g_ref[...]).astype(BF16)
    for t, (b_ref, n_ref) in enumerate(((kb_ref, kn_ref), (vb_ref, vn_ref))):
        for head in range(X_H):
            c0 = head * X_HD
            res = _dot(h, w_ref[:, t * X_W + c0:t * X_W + c0 + X_HD])
            b_ref[:, c0:c0 + X_HD] = res.astype(BF16)
            for half in range(MEM_HALVES):
                n_ref[_mem_head_rows(head, half), :] = res[:, half * LANES:(half + 1) * LANES]


def _mem_kv(mem2d, g, w, prev, layer, depth, batch):
    d = mem2d.shape[1]
    rows_spec = pl.BlockSpec((None, None, MEM_LEN * MEM_TILE_ROWS, LANES), lambda i: (layer, i, 0, 0))
    rows_shape = jax.ShapeDtypeStruct((depth, batch, MEM_LEN * MEM_TILE_ROWS, LANES), F32)
    dense_spec = pl.BlockSpec((MEM_LEN, X_W), lambda i: (i, 0))
    dense_shape = jax.ShapeDtypeStruct((batch * MEM_LEN, X_W), BF16)
    in_specs = [pl.BlockSpec((MEM_LEN, d), lambda i: (i, 0)),
                pl.BlockSpec((1, d), lambda i: (0, 0)),
                _layer_weight_spec(w, layer)]
    args = [mem2d, g, w]
    aliases = {}
    if prev is not None:
        aliases = {len(args): 2, len(args) + 1: 3}
        in_specs += [pl.BlockSpec(memory_space=pl.ANY)] * 2
        args += list(prev)
    return pl.pallas_call(
        _mem_kv_kernel,
        grid=(batch,),
        in_specs=in_specs,
        out_specs=[dense_spec, dense_spec, rows_spec, rows_spec],
        out_shape=[dense_shape, dense_shape, rows_shape, rows_shape],
        input_output_aliases=aliases,
        compiler_params=_params("parallel"),
        name="mem_kv",
    )(*args)


def _mem_softmax(s):
    p = jnp.exp(s - jnp.max(s, axis=-1, keepdims=True))
    return p, 1.0 / jnp.sum(p, axis=-1, keepdims=True)


def _mem_sample_kernel(q_ref, mz_ref, mk_ref, mv_ref, o_ref, *, group, seq):
    def head_tile(ref, e, h):
        halves = [ref[e, _mem_head_rows(h, half), :] for half in range(MEM_HALVES)]
        return jnp.concatenate(halves, axis=1).astype(BF16)

    q = q_ref[...].astype(F32)
    pairs = [(e, h) for e in range(group) for h in range(X_H)]
    scores = []
    for e, h in pairs:
        q_eh = _pad_rows(q[e * seq:(e + 1) * seq, h * X_HD:(h + 1) * X_HD], BF16_ROWS).astype(BF16)
        scores.append(_dot_nt(q_eh, head_tile(mk_ref, e, h))[:seq])
    p, rinv = _mem_softmax(jnp.concatenate(scores, axis=0) * (X_HD ** -0.5))
    outs = [[] for _ in range(group)]
    for i, (e, h) in enumerate(pairs):
        p_eh = _pad_rows(p[i * seq:(i + 1) * seq], BF16_ROWS).astype(BF16)
        outs[e].append(_dot(p_eh, head_tile(mv_ref, e, h))[:seq] * rinv[i * seq:(i + 1) * seq])
    o = jnp.concatenate([jnp.concatenate(row, axis=1) for row in outs], axis=0)
    o_ref[...] = (o * _silu(mz_ref[...].astype(F32))).astype(BF16)


def _mem_sample(z, mk, mv, layer, batch, seq):
    group = 4
    rows = group * seq
    mem_spec = pl.BlockSpec((None, group, MEM_LEN * MEM_TILE_ROWS, LANES),
                            lambda i: (layer, i, 0, 0))
    return pl.pallas_call(
        functools.partial(_mem_sample_kernel, group=group, seq=seq),
        grid=(batch // group,),
        in_specs=[
            pl.BlockSpec((rows, X_W), lambda i: (i, Z_MQ // X_W)),
            pl.BlockSpec((rows, X_W), lambda i: (i, Z_MZ // X_W)),
            mem_spec, mem_spec,
        ],
        out_specs=pl.BlockSpec((rows, X_W), lambda i: (i, 0)),
        out_shape=jax.ShapeDtypeStruct((batch * seq, X_W), BF16),
        compiler_params=_params("parallel"),
        name="mem_sample",
    )(z, z, mk, mv)


FRONT_ROWS = 256
FRONT_GLA_CHUNK = 2 * GLA_CHUNK

def _front_kernel(sink_ref, x_ref, g_ref, w_ref, wa_ref, ba_ref, gout_ref, mk_ref, mv_ref, *rest):
    (ogla_ref, oswa_ref, omem_ref, mg_ref, kvl_ref, s_ref,
     gq_scr, gk_scr, gv_scr, gz_scr, ga_scr, sq_scr, sz_scr, mq_scr, mz_scr, kv_scr, kvp_scr) = rest[-17:]
    n = pl.program_id(1)
    tm = FRONT_ROWS

    @pl.when(n == 0)
    def _():
        s_ref[...] = jnp.zeros_like(s_ref)
        kvp_scr[...] = jnp.zeros_like(kvp_scr)

    h = _rmsnorm_rows(x_ref[...], g_ref[...]).astype(BF16)

    def project(chunk):
        dst, dst_c0, w_r0, width = chunk
        if dst is ga_scr:
            dst[...] = _project_decay(h, w_ref).astype(dst.dtype)
        else:
            dst[:, dst_c0:dst_c0 + width] = _project(h, w_ref, w_r0, width).astype(dst.dtype)

    def chunks_of(dst, w_r0, width):
        return [(dst, c, w_r0 + c, min(MXU_COLS, width - c)) for c in range(0, width, MXU_COLS)]

    swa_cols = chunks_of(kv_scr, W_SK, KV_COLS) + chunks_of(sq_scr, W_SQ, SWA_W) \
        + chunks_of(sz_scr, W_SZ, SWA_W)
    gla_cols = (chunks_of(ga_scr, W_GA, LANES) + chunks_of(gq_scr, W_GQ, GLA_QK)
                + chunks_of(gk_scr, W_GK, GLA_QK) + chunks_of(gv_scr, W_GV, GLA_V)
                + chunks_of(gz_scr, W_GZ, GLA_V))
    mem_cols = chunks_of(mq_scr, W_MQ, X_W) + chunks_of(mz_scr, W_MZ, X_W)
    merge_cols = chunks_of(mg_ref, W_MG, N_BRANCH * D_MODEL)

    i = lax.broadcasted_iota(jnp.int32, (WINDOW, 2 * WINDOW), 0)
    j = lax.broadcasted_iota(jnp.int32, (WINDOW, 2 * WINDOW), 1)
    diff = i + WINDOW - j
    band = (diff >= 0) & (diff <= WINDOW)
    swa = {}

    def swa_start(blk):
        prev = kvp_scr[...] if blk == 0 else kv_scr[(blk - 1) * WINDOW:blk * WINDOW, :]
        kvb = jnp.concatenate([prev, kv_scr[blk * WINDOW:(blk + 1) * WINDOW, :]], axis=0)
        swa["ops"] = _swa_operands(kvb[:, :SWA_KV], kvb[:, SWA_KV:])
        swa["valid"] = band & (j >= jnp.where(n > 0, 0, WINDOW)) if blk == 0 else band

    def swa_scores(blk, kh):
        swa["s"] = _swa_scores(sq_scr[blk * WINDOW:(blk + 1) * WINDOW, :], kh, swa["ops"][kh][0])
        swa["soft"] = []

    def swa_soft(kh, hb):
        swa["soft"].append(_swa_soft_head(swa["s"][hb * WINDOW:(hb + 1) * WINDOW], swa["valid"],
                                          sink_ref[_swa_head_of_block(kh, hb)]))

    def swa_out(blk, kh):
        _, v_even, v_odd = swa["ops"][kh]
        o = _swa_out([p for p, _ in swa["soft"]], [r for _, r in swa["soft"]], v_even, v_odd)
        rows = slice(blk * WINDOW, (blk + 1) * WINDOW)
        cols = slice(kh * SWA_G * SWA_HD, (kh + 1) * SWA_G * SWA_HD)
        oswa_ref[rows, cols] = (o * _silu(sz_scr[rows, cols].astype(F32))).astype(BF16)

    swa_steps = []
    for blk in range(tm // WINDOW):
        swa_steps.append((functools.partial(swa_start, blk), 50))
        for kh in range(SWA_KVH):
            swa_steps.append((functools.partial(swa_scores, blk, kh), 30))
            swa_steps += [(functools.partial(swa_soft, kh, hb), 40) for hb in range(SWA_G)]
            swa_steps.append((functools.partial(swa_out, blk, kh), 75))

    mem = {}

    def mem_scores(hd):
        cols = slice(hd * X_HD, (hd + 1) * X_HD)
        p, rinv = _mem_softmax(_dot_nt(mq_scr[:, cols], mk_ref[:, cols]) * (X_HD ** -0.5))
        mem["p"], mem["rinv"] = p.astype(BF16), rinv

    def mem_out(hd):
        cols = slice(hd * X_HD, (hd + 1) * X_HD)
        o = _dot(mem["p"], mv_ref[:, cols]) * mem["rinv"]
        omem_ref[:, cols] = (o * _silu(mz_scr[:, cols].astype(F32))).astype(BF16)

    mem_steps = []
    for hd in range(X_H):
        mem_steps += [(functools.partial(mem_scores, hd), 130), (functools.partial(mem_out, hd), 160)]

    gla_steps = _gla_thunks(gq_scr, gk_scr, gv_scr, gz_scr, ga_scr, wa_ref, ba_ref, gout_ref, None,
                            ogla_ref, s_ref, chunk=FRONT_GLA_CHUNK, nchunks=tm // FRONT_GLA_CHUNK,
                            carry=True, anchored=True)

    for chunk in swa_cols:
        project(chunk)
    chunks = gla_cols + mem_cols + merge_cols
    steps = [(t, w, 0) for t, w in swa_steps]
    steps += [(t, w, len(gla_cols)) for t, w in gla_steps]
    steps += [(t, w, len(gla_cols) + len(mem_cols)) for t, w in mem_steps]
    _interleave(steps, chunks, project)

    kv_tail = kv_scr[tm - WINDOW:, :]
    kvp_scr[...] = kv_tail
    kvl_ref[...] = kv_tail.T


def _front(x2d, g, w, sink, wa, ba, gout, mk, mv, prev_state, layer, depth, batch, seq):
    t, d = x2d.shape
    tm = FRONT_ROWS
    nblk = seq // tm
    row = lambda width: pl.BlockSpec((tm, width), lambda b, n: (b * nblk + n, 0))
    const = lambda shape: pl.BlockSpec(shape, lambda b, n: (0,) * len(shape))
    mem_spec = pl.BlockSpec((MEM_LEN, X_W), lambda b, n: (b, 0))
    state_spec = pl.BlockSpec((None, None, GLA_H, GLA_DK, GLA_DV), lambda b, n: (layer, b, 0, 0, 0))
    in_specs = [pl.BlockSpec(memory_space=pltpu.SMEM), row(d), const((1, d)),
                _layer_weight_spec(w, layer, resident=True),
                _layer_weight_spec(wa, layer), const((1, GLA_QK)), const((1, GLA_DV)),
                mem_spec, mem_spec]
    args = [sink, x2d, g, w, wa, ba, gout, mk, mv]
    aliases = {}
    if prev_state is not None:
        aliases = {len(args): 5}
        in_specs.append(pl.BlockSpec(memory_space=pl.ANY))
        args.append(prev_state)
    scratch = [pltpu.VMEM((tm, width), BF16) for width in
               (GLA_QK, GLA_QK, GLA_V, GLA_V, LANES, SWA_W, SWA_W, X_W, X_W)]
    scratch += [pltpu.VMEM((tm, KV_COLS), F32), pltpu.VMEM((WINDOW, KV_COLS), F32)]
    return pl.pallas_call(
        _front_kernel,
        grid=(batch, nblk),
        in_specs=in_specs,
        out_specs=[row(GLA_V), row(SWA_W), row(X_W), row(N_BRANCH * D_MODEL),
                   pl.BlockSpec((None, KV_COLS, WINDOW), lambda b, n: (b, 0, 0)), state_spec],
        out_shape=[jax.ShapeDtypeStruct((t, GLA_V), BF16), jax.ShapeDtypeStruct((t, SWA_W), BF16),
                   jax.ShapeDtypeStruct((t, X_W), BF16),
                   jax.ShapeDtypeStruct((t, N_BRANCH * D_MODEL), BF16),
                   jax.ShapeDtypeStruct((batch, KV_COLS, WINDOW), F32),
                   jax.ShapeDtypeStruct((depth, batch, GLA_H, GLA_DK, GLA_DV), F32)],
        scratch_shapes=scratch,
        input_output_aliases=aliases,
        compiler_params=_params("parallel", "arbitrary"),
        name="front",
    )(*args)


def _merge_kernel(x_ref, a_ref, s_ref, m_ref, ga_ref, gs_ref, gm_ref, wa_ref, ws_ref, wm_ref,
                  wo_ref, *rest, final):
    if final:
        gf_ref, y_ref = rest
    else:
        (y_ref,) = rest
    merged = (_sigmoid(ga_ref[...].astype(F32)) * _dot(a_ref[...], wa_ref[...])
              + _sigmoid(gs_ref[...].astype(F32)) * _dot(s_ref[...], ws_ref[...])
              + _sigmoid(gm_ref[...].astype(F32)) * _dot(m_ref[...], wm_ref[...]))
    y = x_ref[...] + _dot(merged.astype(BF16), wo_ref[...])
    if final:
        y = _rmsnorm_rows(y, gf_ref[...])
    y_ref[...] = y


def _merge(x2d, o_gla, o_swa, o_mem, gates, gate_col0, w_gla, w_swa, w_mem, w_out, g_final, layer):
    t, d = x2d.shape
    tm = _row_tile(t)
    final = g_final is not None
    row = lambda width: pl.BlockSpec((tm, width), lambda i: (i, 0))
    gate = lambda k: pl.BlockSpec((tm, d), lambda i: (i, gate_col0 // d + k))
    in_specs = [row(d), row(d), row(d), row(d), gate(0), gate(1), gate(2)]
    in_specs += [_layer_weight_spec(w, layer) for w in (w_gla, w_swa, w_mem, w_out)]
    args = [x2d, o_gla, o_swa, o_mem, gates, gates, gates, w_gla, w_swa, w_mem, w_out]
    if final:
        in_specs.append(pl.BlockSpec((1, d), lambda i: (0, 0)))
        args.append(g_final)
    return pl.pallas_call(
        functools.partial(_merge_kernel, final=final),
        grid=(t // tm,),
        in_specs=in_specs,
        out_specs=row(d),
        out_shape=jax.ShapeDtypeStruct((t, d), F32),
        compiler_params=_params("parallel"),
        name="merge_final" if final else "merge",
    )(*args)


def kernel(x_prompt, x_sample, mem_prompt, state_gla, cache_swa_k, cache_swa_v, cache_mem_k,
           cache_mem_v, g_norm, w_in, w_gla_a, b_gla_a, g_gla_out, swa_sink, g_mem, w_mem_kv,
           w_br_gla, w_br_swa, w_br_mem, w_out, g_final):
    depth = w_in.shape[0]
    batch, seq, d = x_prompt.shape
    dec_batch, dec_seq, _ = x_sample.shape
    wb = cache_swa_k.shape[2]

    w_in_r = w_in.transpose(0, 2, 1).astype(BF16)
    w_a = jnp.pad(w_gla_a, ((0, 0), (0, LANES - GLA_RANK), (0, 0))).astype(BF16)
    w_mem_b = w_mem_kv.astype(BF16)
    w_g, w_s, w_m, w_o = (w.astype(BF16) for w in (w_br_gla, w_br_swa, w_br_mem, w_out))
    g_fin = g_final.reshape(1, d)

    yp = x_prompt.reshape(batch * seq, d)
    ys = x_sample.reshape(dec_batch * dec_seq, d)
    mem2d = mem_prompt.reshape(batch * MEM_LEN, d)
    swa_t = lambda c: c.transpose(0, 1, 3, 4, 2).reshape(depth, dec_batch, SWA_KV, wb)
    ck_all, cv_all = swa_t(cache_swa_k), swa_t(cache_swa_v)
    cmk_all = _mem_rows_view(cache_mem_k)
    cmv_all = _mem_rows_view(cache_mem_v)
    gla_p = gla_s = swa_new = mem_new = None
    kp_l, vp_l = [], []
    for l in range(depth):
        last = l == depth - 1
        gn = g_norm[l].reshape(1, d)
        ba = b_gla_a[l].reshape(1, GLA_QK)
        gout = g_gla_out[l].reshape(1, GLA_DV)
        sink = swa_sink[l]
        branch_w = (w_g, w_s, w_m, w_o, g_fin if last else None, l)

        mk, mv, *mem_new = _mem_kv(mem2d, g_mem[l].reshape(1, d), w_mem_b, mem_new, l, depth,
                                   batch)
        o_gla, o_swa, o_mem, gates, kv_tail, gla_p = _front(
            yp, gn, w_in_r, sink, w_a, ba, gout, mk, mv, gla_p, l, depth, batch, seq)
        yp = _merge(yp, o_gla, o_swa, o_mem, gates, 0, *branch_w)
        kv_heads = kv_tail.reshape(batch, 2, SWA_KVH, SWA_HD, WINDOW).transpose(0, 1, 4, 2, 3)
        kp_l.append(kv_heads[:, 0])
        vp_l.append(kv_heads[:, 1])

        z, kv = _in_proj(ys, gn, w_in_r, l)
        o_gla, gla_s = _gla_sample(z, w_a, ba, gout, state_gla, gla_s, l, dec_batch, dec_seq)
        o_swa, *swa_new = _swa_sample(z, kv, sink, ck_all, cv_all, swa_new, l, dec_batch, dec_seq)
        o_mem = _mem_sample(z, cmk_all, cmv_all, l, dec_batch, dec_seq)
        ys = _merge(ys, o_gla, o_swa, o_mem, z, Z_MG, *branch_w)

    k_s, v_s = (c.reshape(depth, dec_batch, SWA_KVH, SWA_HD, wb).transpose(0, 1, 4, 2, 3)
                for c in swa_new)
    mk_p, mv_p = (_mem_from_rows_view(c) for c in mem_new)
    return (yp.reshape(batch, seq, d), ys.reshape(dec_batch, dec_seq, d), gla_p,
            jnp.stack(kp_l), jnp.stack(vp_l), mk_p, mv_p, gla_s, k_s, v_s)
```

```python
import functools

import jax
import jax.numpy as jnp
from jax import lax
from jax.experimental import pallas as pl
from jax.experimental.pallas import tpu as pltpu

F32 = jnp.float32
BF16 = jnp.bfloat16

D_MODEL = 1024
GLA_H = 4
GLA_DK = 128
GLA_DV = 256
GLA_QK = GLA_H * GLA_DK
GLA_V = GLA_H * GLA_DV
GLA_RANK = 16
GLA_TAU = 16.0
GLA_CHUNK = 32
SWA_HD = 64
SWA_H = 16
SWA_KVH = 2
SWA_G = SWA_H // SWA_KVH
SWA_W = SWA_H * SWA_HD
SWA_KV = SWA_KVH * SWA_HD
WINDOW = 128
MEM_LEN = 256
X_H = 4
X_HD = 256
X_W = X_H * X_HD
N_BRANCH = 3
IN_SPLITS = (GLA_QK, GLA_QK, GLA_V, GLA_V, GLA_RANK, SWA_W, SWA_KV, SWA_KV, SWA_W, X_W, X_W,
             N_BRANCH * D_MODEL)
EPS = 1e-6

LANES = 128
BF16_ROWS = 16
MXU_COLS = 256
VMEM_LIMIT = 56 * 1024 * 1024

Z_GQ = 0
Z_GK = Z_GQ + GLA_QK
Z_GV = Z_GK + GLA_QK
Z_GZ = Z_GV + GLA_V
Z_SQ = Z_GZ + GLA_V
Z_SZ = Z_SQ + SWA_W
Z_MQ = Z_SZ + SWA_W
Z_MZ = Z_MQ + X_W
Z_MG = Z_MZ + X_W
Z_GA = Z_MG + N_BRANCH * D_MODEL
Z_COLS = Z_GA + LANES
KV_COLS = 2 * SWA_KV

(W_GQ, W_GK, W_GV, W_GZ, W_GA, W_SQ, W_SK, W_SV, W_SZ, W_MQ, W_MZ, W_MG) = (
    sum(IN_SPLITS[:i]) for i in range(len(IN_SPLITS)))
Z_PIECES = ((Z_GQ, W_GQ, GLA_QK), (Z_GK, W_GK, GLA_QK), (Z_GV, W_GV, GLA_V), (Z_GZ, W_GZ, GLA_V),
            (Z_SQ, W_SQ, SWA_W), (Z_SZ, W_SZ, SWA_W), (Z_MQ, W_MQ, X_W), (Z_MZ, W_MZ, X_W),
            (Z_MG, W_MG, N_BRANCH * D_MODEL))


def _project(h, w_ref, row0, width):
    return _dot_nt(h, w_ref[row0:row0 + width, :])


def _project_decay(h, w_ref):
    res = _project(h, w_ref, W_GA, LANES)
    lane = lax.broadcasted_iota(jnp.int32, res.shape, 1)
    return jnp.where(lane < GLA_RANK, res, 0.0)


def _row_tile(rows):
    return 512 if rows >= 4096 else 256


def _params(*semantics):
    return pltpu.CompilerParams(dimension_semantics=semantics, vmem_limit_bytes=VMEM_LIMIT)


def _sigmoid(x):
    return 0.5 * jnp.tanh(0.5 * x) + 0.5


def _silu(x):
    return x * _sigmoid(x)


def _log_sigmoid(x):
    return jnp.minimum(x, 0.0) - jnp.log(1.0 + jnp.exp(-jnp.abs(x)))


def _rmsnorm_rows(x, g):
    return x * lax.rsqrt(jnp.mean(x * x, axis=-1, keepdims=True) + EPS) * g


def _dot(a, b):
    return jnp.dot(a, b, preferred_element_type=F32)


def _dot_nt(a, b):
    return lax.dot_general(a, b, (((1,), (1,)), ((), ())), preferred_element_type=F32)


def _dot_tn(a, b):
    return lax.dot_general(a, b, (((0,), (0,)), ((), ())), preferred_element_type=F32)


def _pad_rows(x, rows):
    if x.shape[0] >= rows:
        return x
    return jnp.concatenate([x, jnp.zeros((rows - x.shape[0], x.shape[1]), x.dtype)], axis=0)


def _interleave(steps, chunks, run_chunk):
    total = sum(w for _, w, _ in steps)
    done, acc = 0, 0
    for thunk, weight, needs in steps:
        acc += weight
        upto = max(needs, min(len(chunks), -(-acc * len(chunks) // total)))
        for chunk in chunks[done:upto]:
            run_chunk(chunk)
        done = max(done, upto)
        thunk()


def _in_proj_kernel(x_ref, g_ref, w_ref, z_ref, kv_ref):
    h = _rmsnorm_rows(x_ref[...], g_ref[...]).astype(BF16)
    for z0, w0, width in Z_PIECES:
        for c in range(0, width, 512):
            z_ref[:, z0 + c:z0 + c + 512] = _project(h, w_ref, w0 + c, 512).astype(BF16)
    z_ref[:, Z_GA:Z_COLS] = _project_decay(h, w_ref).astype(BF16)
    kv_ref[...] = _project(h, w_ref, W_SK, KV_COLS)


def _layer_weight_spec(w, layer, resident=False):
    mode = dict(pipeline_mode=pl.Buffered(1)) if resident else {}
    return pl.BlockSpec((None,) + w.shape[1:], lambda *ids: (layer, 0, 0), **mode)


def _in_proj(x2d, g, w, layer):
    t, d = x2d.shape
    tm = _row_tile(t)
    return pl.pallas_call(
        _in_proj_kernel,
        grid=(t // tm,),
        in_specs=[
            pl.BlockSpec((tm, d), lambda i: (i, 0)),
            pl.BlockSpec((1, d), lambda i: (0, 0)),
            _layer_weight_spec(w, layer, resident=True),
        ],
        out_specs=[pl.BlockSpec((tm, Z_COLS), lambda i: (i, 0)),
                   pl.BlockSpec((tm, KV_COLS), lambda i: (i, 0))],
        out_shape=[jax.ShapeDtypeStruct((t, Z_COLS), BF16), jax.ShapeDtypeStruct((t, KV_COLS), F32)],
        compiler_params=_params("parallel"),
        name="in_proj",
    )(x2d, g, w)


def _chunk_cumsum(g, chunk):
    pos = lax.broadcasted_iota(jnp.int32, g.shape, 0) & (chunk - 1)
    b = g
    shift = 1
    while shift < chunk:
        b = b + jnp.where(pos >= shift, pltpu.roll(b, shift, 0), 0.0)
        shift *= 2
    return b


def _gla_thunks(q_ref, k_ref, v_ref, gz_ref, ga_ref, wa_ref, ba_ref, gout_ref, s0_ref, o_ref,
                s_ref, *, chunk, nchunks, carry, anchored):
    rows = chunk * nchunks
    mm_rows = max(chunk, BF16_ROWS)
    heads = range(GLA_H)
    lanes = [slice(h * GLA_DK, (h + 1) * GLA_DK) for h in heads]
    vl = [slice(h * GLA_DV, (h + 1) * GLA_DV) for h in heads]
    per_head = lambda: [None] * GLA_H
    st = {name: per_head() for name in ("b", "qb", "kd", "qb_bf", "kd_bf", "qa_bf", "ka_bf",
                                        "dec_cols", "o_intra")}
    st["o_inter"] = [[] for _ in heads]

    def chunk_row(x, row):
        picked = [x[j * chunk + row:j * chunk + row + 1, :] for j in range(nchunks)]
        return picked, jnp.concatenate([jnp.broadcast_to(r, (chunk, x.shape[1])) for r in picked],
                                       axis=0)

    small = chunk < BF16_ROWS

    def chunk_rows(x_f32, x_bf, sl):
        return _pad_rows(x_f32[sl], mm_rows).astype(BF16) if small else x_bf[sl]

    def prepare():
        r_idx = lax.broadcasted_iota(jnp.int32, (rows, rows), 0)
        c_idx = lax.broadcasted_iota(jnp.int32, (rows, rows), 1)
        diff = r_idx - c_idx
        st["mask"] = (diff >= 0) & (diff <= (r_idx & (chunk - 1)))
        st["v_f32"] = v_ref[...].astype(F32) if small else None
        if carry:
            st["states"] = [s_ref[h] for h in heads]

    def decay(h):
        x = _dot(ga_ref[...], wa_ref[:, lanes[h]]) + ba_ref[:, lanes[h]]
        st["b"][h] = _chunk_cumsum(_log_sigmoid(x) * (1.0 / GLA_TAU), chunk)

    def factors(h):
        b = st["b"][h]
        b_last, b_last_full = chunk_row(b, chunk - 1)
        b_last_rows = jnp.concatenate(b_last + [jnp.zeros((LANES - nchunks, GLA_DK), F32)], axis=0)
        st["dec_cols"][h] = jnp.exp(b_last_rows.T)
        q = q_ref[:, lanes[h]].astype(F32) * (GLA_DK ** -0.5)
        k = k_ref[:, lanes[h]].astype(F32)
        qb = q * jnp.exp(b)
        kd = k * jnp.exp(b_last_full - b)
        st["qb"][h], st["kd"][h] = (qb, kd) if small else (None, None)
        st["qb_bf"][h], st["kd_bf"][h] = qb.astype(BF16), kd.astype(BF16)
        if anchored:
            _, b_mid_full = chunk_row(b, chunk // 2 - 1)
            st["qa_bf"][h] = (q * jnp.exp(b - b_mid_full)).astype(BF16)
            st["ka_bf"][h] = (k * jnp.exp(b_mid_full - b)).astype(BF16)
        else:
            st["qa_bf"][h] = st["qb_bf"][h]
            st["ka_bf"][h] = (k * jnp.exp(-b)).astype(BF16)

    def intra(h):
        a = jnp.where(st["mask"], _dot_nt(st["qa_bf"][h], st["ka_bf"][h]), 0.0)
        st["o_intra"][h] = _dot(a.astype(BF16), v_ref[:, vl[h]])

    def step(j):
        sl = slice(j * chunk, (j + 1) * chunk)
        for h in heads:
            state = st["states"][h] if carry else s0_ref[j, h]
            qb_j = chunk_rows(st["qb"][h], st["qb_bf"][h], sl)
            st["o_inter"][h].append(_dot(qb_j, state.astype(BF16))[:chunk])
            if small:
                v_j = _pad_rows(st["v_f32"][sl, vl[h]], mm_rows).astype(BF16)
            else:
                v_j = v_ref[sl, vl[h]]
            kd_j = chunk_rows(st["kd"][h], st["kd_bf"][h], sl)
            new_state = st["dec_cols"][h][:, j:j + 1] * state + _dot_tn(kd_j, v_j)
            if carry:
                st["states"][h] = new_state
            else:
                s_ref[j, h] = new_state

    def epilogue(h):
        if carry:
            s_ref[h] = st["states"][h]
        o = st["o_intra"][h] + jnp.concatenate(st["o_inter"][h], axis=0)
        gate = _silu(gz_ref[:, vl[h]].astype(F32))
        o_ref[:, vl[h]] = (_rmsnorm_rows(o, gout_ref[...]) * gate).astype(BF16)

    steps = [(prepare, 100)]
    steps += [(functools.partial(decay, h), 230) for h in heads]
    steps += [(functools.partial(factors, h), 170) for h in heads]
    steps += [(functools.partial(intra, h), 60) for h in heads]
    steps += [(functools.partial(step, j), 130) for j in range(nchunks)]
    steps += [(functools.partial(epilogue, h), 190) for h in heads]
    return steps


def _gla_sample_kernel(q_ref, k_ref, v_ref, gz_ref, ga_ref, wa_ref, ba_ref, gout_ref, s0_ref, *rest,
                       seq, group):
    o_ref, s_ref = rest[-2:]
    for thunk, _ in _gla_thunks(q_ref, k_ref, v_ref, gz_ref, ga_ref, wa_ref, ba_ref, gout_ref,
                                s0_ref, o_ref, s_ref, chunk=seq, nchunks=group, carry=False,
                                anchored=False):
        thunk()


def _swa_head_of_block(kv_head, blk):
    pairs = SWA_G // 2
    within = 2 * blk if blk < pairs else 2 * (blk - pairs) + 1
    return kv_head * SWA_G + within


def _swa_operands(kk, vv):
    lo = lax.broadcasted_iota(jnp.int32, kk.shape, 1) < SWA_HD
    kk_sw = pltpu.roll(kk, SWA_HD, 1)
    vv_sw = pltpu.roll(vv, SWA_HD, 1)
    ops = []
    for kh in range(SWA_KVH):
        k_this, k_other = (kk, kk_sw) if kh == 0 else (kk_sw, kk)
        v_this, v_other = (vv, vv_sw) if kh == 0 else (vv_sw, vv)
        ops.append((jnp.where(lo, k_this, k_other).astype(BF16),
                    jnp.where(lo, v_this, 0.0).astype(BF16),
                    jnp.where(lo, 0.0, v_other).astype(BF16)))
    return ops


def _swa_operands_t(kk_t, vv_t):
    zeros = jnp.zeros((SWA_HD, kk_t.shape[1]), F32)
    ops = []
    for kh in range(SWA_KVH):
        k_h = kk_t[kh * SWA_HD:(kh + 1) * SWA_HD]
        v_h = vv_t[kh * SWA_HD:(kh + 1) * SWA_HD]
        ops.append((jnp.concatenate([k_h, k_h], axis=0).astype(BF16),
                    jnp.concatenate([v_h, zeros], axis=0).astype(BF16),
                    jnp.concatenate([zeros, v_h], axis=0).astype(BF16)))
    return ops


def _swa_scores(q, kh, k_dup, keys_on_lanes=False):
    nq = q.shape[0]
    lo = lax.broadcasted_iota(jnp.int32, (nq, LANES), 1) < SWA_HD
    even, odd = [], []
    for p in range(SWA_G // 2):
        c0 = kh * SWA_G * SWA_HD + p * LANES
        q_pair = q[:, c0:c0 + LANES]
        even.append(jnp.where(lo, q_pair, jnp.zeros_like(q_pair)))
        odd.append(jnp.where(lo, jnp.zeros_like(q_pair), q_pair))
    lhs = jnp.concatenate(even + odd, axis=0).astype(BF16)
    s = _dot(lhs, k_dup) if keys_on_lanes else _dot_nt(lhs, k_dup)
    return s * (SWA_HD ** -0.5)


def _swa_soft_head(s_h, valid, sink):
    s_h = jnp.where(valid, s_h, -jnp.inf)
    m = jnp.maximum(jnp.max(s_h, axis=-1, keepdims=True), sink)
    p_h = jnp.exp(s_h - m)
    den = jnp.sum(p_h, axis=-1, keepdims=True) + jnp.exp(sink - m)
    if p_h.shape[0] % BF16_ROWS == 0:
        p_h = p_h.astype(BF16)
    return p_h, 1.0 / den


def _swa_out(probs, rinv, v_even, v_odd):
    pairs = SWA_G // 2
    nq = probs[0].shape[0]
    lo = lax.broadcasted_iota(jnp.int32, (nq, LANES), 1) < SWA_HD
    p_even = jnp.concatenate(probs[:pairs], axis=0).astype(BF16)
    p_odd = jnp.concatenate(probs[pairs:], axis=0).astype(BF16)
    o = _dot(p_even, v_even) + _dot(p_odd, v_odd)
    outs = [o[p * nq:(p + 1) * nq] * jnp.where(lo, rinv[p], rinv[pairs + p]) for p in range(pairs)]
    return jnp.concatenate(outs, axis=1)


def _swa_sample_kernel(sink_ref, q_ref, sz_ref, kvn_ref, ck_ref, cv_ref, *rest, group, seq):
    o_ref, nk_ref, nv_ref = rest[-3:]
    wb = ck_ref.shape[2]
    nk = 2 * wb
    pairs = SWA_G // 2
    blk_rows = SWA_G * seq
    first = lax.broadcasted_iota(jnp.int32, (SWA_KV, wb), 1) < seq
    q = q_ref[...].astype(F32)
    kvn = _pad_rows(kvn_ref[...], LANES)
    kn_all, vn_all = kvn[:, :SWA_KV].T, kvn[:, SWA_KV:].T

    scores, values = [], []
    for e in range(group):
        rows = slice(e * seq, (e + 1) * seq)
        k_old, v_old = ck_ref[e], cv_ref[e]
        shift = (LANES - e * seq) % LANES
        k_new = pltpu.roll(kn_all, shift, 1) if shift else kn_all
        v_new = pltpu.roll(vn_all, shift, 1) if shift else vn_all
        nk_ref[e] = pltpu.roll(jnp.where(first, k_new, k_old), wb - seq, 1)
        nv_ref[e] = pltpu.roll(jnp.where(first, v_new, v_old), wb - seq, 1)
        ops = _swa_operands_t(jnp.concatenate([k_old, k_new], axis=1),
                              jnp.concatenate([v_old, v_new], axis=1))
        for kh in range(SWA_KVH):
            k_dup, v_even, v_odd = ops[kh]
            scores.append(_swa_scores(q[rows], kh, k_dup, keys_on_lanes=True))
            values.append((v_even, v_odd))
    s_all = jnp.concatenate(scores, axis=0)
    i = lax.broadcasted_iota(jnp.int32, s_all.shape, 0) & (seq - 1)
    j = lax.broadcasted_iota(jnp.int32, s_all.shape, 1)
    diff = i + wb - j
    sink_kh = [jnp.concatenate([jnp.full((seq, 1), sink_ref[_swa_head_of_block(kh, blk)], F32)
                                for blk in range(SWA_G)], axis=0) for kh in range(SWA_KVH)]
    p_all, rinv_all = _swa_soft_head(s_all, (diff >= 0) & (diff <= WINDOW),
                                     jnp.concatenate(sink_kh * group, axis=0))

    lo = lax.broadcasted_iota(jnp.int32, (seq, LANES), 1) < SWA_HD
    outs = [[] for _ in range(group)]
    for idx, (v_even, v_odd) in enumerate(values):
        base = idx * blk_rows
        half = pairs * seq
        o = (_dot_nt(p_all[base:base + half], v_even)
             + _dot_nt(p_all[base + half:base + blk_rows], v_odd))
        for p in range(pairs):
            r_even = rinv_all[base + p * seq:base + (p + 1) * seq]
            r_odd = rinv_all[base + half + p * seq:base + half + (p + 1) * seq]
            outs[idx // SWA_KVH].append(o[p * seq:(p + 1) * seq] * jnp.where(lo, r_even, r_odd))
    o = jnp.concatenate([jnp.concatenate(row, axis=1) for row in outs], axis=0)
    o_ref[...] = (o * _silu(sz_ref[...].astype(F32))).astype(BF16)


MEM_HALVES = X_HD // LANES
MEM_TILE_ROWS = X_H * MEM_HALVES


def _mem_rows_view(x):
    lead = x.shape[:-3]
    n = len(lead)
    x = x.reshape(*lead, MEM_LEN, X_H, MEM_HALVES, LANES)
    x = x.transpose(*range(n), n, n + 2, n + 1, n + 3)
    return x.reshape(*lead, MEM_LEN * MEM_TILE_ROWS, LANES)


def _mem_from_rows_view(x):
    lead = x.shape[:-2]
    n = len(lead)
    x = x.reshape(*lead, MEM_LEN, MEM_HALVES, X_H, LANES)
    x = x.transpose(*range(n), n, n + 2, n + 1, n + 3)
    return x.reshape(*lead, MEM_LEN, X_H, X_HD)


def _mem_head_rows(head, half):
    return pl.ds(half * X_H + head, MEM_LEN, stride=MEM_TILE_ROWS)


def _mem_kv_kernel(x_ref, g_ref, w_ref, *rest):
    kb_ref, vb_ref, kn_ref, vn_ref = rest[-4:]
    h = _rmsnorm_rows(x_ref[...], g_ref[...]).astype(BF16)
    for t, (b_ref, n_ref) in enumerate(((kb_ref, kn_ref), (vb_ref, vn_ref))):
        for head in range(X_H):
            c0 = head * X_HD
            res = _dot(h, w_ref[:, t * X_W + c0:t * X_W + c0 + X_HD])
            b_ref[:, c0:c0 + X_HD] = res.astype(BF16)
            for half in range(MEM_HALVES):
                n_ref[_mem_head_rows(head, half), :] = res[:, half * LANES:(half + 1) * LANES]


def _mem_kv(mem2d, g, w, prev, layer, depth, batch):
    d = mem2d.shape[1]
    rows_spec = pl.BlockSpec((None, None, MEM_LEN * MEM_TILE_ROWS, LANES), lambda i: (layer, i, 0, 0))
    rows_shape = jax.ShapeDtypeStruct((depth, batch, MEM_LEN * MEM_TILE_ROWS, LANES), F32)
    dense_spec = pl.BlockSpec((MEM_LEN, X_W), lambda i: (i, 0))
    dense_shape = jax.ShapeDtypeStruct((batch * MEM_LEN, X_W), BF16)
    in_specs = [pl.BlockSpec((MEM_LEN, d), lambda i: (i, 0)),
                pl.BlockSpec((1, d), lambda i: (0, 0)),
                _layer_weight_spec(w, layer)]
    args = [mem2d, g, w]
    aliases = {}
    if prev is not None:
        aliases = {len(args): 2, len(args) + 1: 3}
        in_specs += [pl.BlockSpec(memory_space=pl.ANY)] * 2
        args += list(prev)
    return pl.pallas_call(
        _mem_kv_kernel,
        grid=(batch,),
        in_specs=in_specs,
        out_specs=[dense_spec, dense_spec, rows_spec, rows_spec],
        out_shape=[dense_shape, dense_shape, rows_shape, rows_shape],
        input_output_aliases=aliases,
        compiler_params=_params("parallel"),
        name="mem_kv",
    )(*args)


def _mem_softmax(s):
    p = jnp.exp(s - jnp.max(s, axis=-1, keepdims=True))
    return p, 1.0 / jnp.sum(p, axis=-1, keepdims=True)


def _mem_sample_kernel(q_ref, mz_ref, mk_ref, mv_ref, o_ref, *, group, seq):
    def head_tile(ref, e, h):
        halves = [ref[e, _mem_head_rows(h, half), :] for half in range(MEM_HALVES)]
        return jnp.concatenate(halves, axis=1).astype(BF16)

    q = q_ref[...].astype(F32)
    pairs = [(e, h) for e in range(group) for h in range(X_H)]
    scores = []
    for e, h in pairs:
        q_eh = _pad_rows(q[e * seq:(e + 1) * seq, h * X_HD:(h + 1) * X_HD], BF16_ROWS).astype(BF16)
        scores.append(_dot_nt(q_eh, head_tile(mk_ref, e, h))[:seq])
    p, rinv = _mem_softmax(jnp.concatenate(scores, axis=0) * (X_HD ** -0.5))
    outs = [[] for _ in range(group)]
    for i, (e, h) in enumerate(pairs):
        p_eh = _pad_rows(p[i * seq:(i + 1) * seq], BF16_ROWS).astype(BF16)
        outs[e].append(_dot(p_eh, head_tile(mv_ref, e, h))[:seq] * rinv[i * seq:(i + 1) * seq])
    o = jnp.concatenate([jnp.concatenate(row, axis=1) for row in outs], axis=0)
    o_ref[...] = (o * _silu(mz_ref[...].astype(F32))).astype(BF16)


SAMPLE_GROUP = 4


def _sample_mixers_kernel(sink_ref, gq_ref, gk_ref, gv_ref, gz_ref, ga_ref, sq_ref, sz_ref, mq_ref,
                          mz_ref, kvn_ref, wa_ref, ba_ref, gout_ref, s0_ref, ck_ref, cv_ref, mk_ref,
                          mv_ref, *rest, seq):
    ogla_ref, oswa_ref, omem_ref, s_ref, nk_ref, nv_ref = rest[-6:]
    _mem_sample_kernel(mq_ref, mz_ref, mk_ref, mv_ref, omem_ref, group=SAMPLE_GROUP, seq=seq)
    _swa_sample_kernel(sink_ref, sq_ref, sz_ref, kvn_ref, ck_ref, cv_ref, oswa_ref, nk_ref, nv_ref,
                       group=SAMPLE_GROUP, seq=seq)
    _gla_sample_kernel(gq_ref, gk_ref, gv_ref, gz_ref, ga_ref, wa_ref, ba_ref, gout_ref, s0_ref,
                       ogla_ref, s_ref, seq=seq, group=SAMPLE_GROUP)


def _sample_mixers(z, kv, sink, wa, ba, gout, s0, ck, cv, mk, mv, prev, layer, batch, seq):
    group = SAMPLE_GROUP
    rows = group * seq
    wb = ck.shape[3]
    assert wb == LANES, "the sliding window is one lane tile wide"
    zspec = lambda width, col0: pl.BlockSpec((rows, width), lambda i: (i, col0 // width))
    const = lambda shape: pl.BlockSpec(shape, lambda i: (0,) * len(shape))
    layered = lambda *dims: pl.BlockSpec((None, group) + dims, lambda i: (layer, i) + (0,) * len(dims))
    state_spec = layered(GLA_H, GLA_DK, GLA_DV)
    cache_spec = layered(SWA_KV, wb)
    mem_spec = layered(MEM_LEN * MEM_TILE_ROWS, LANES)
    in_specs = [pl.BlockSpec(memory_space=pltpu.SMEM),
                zspec(GLA_QK, Z_GQ), zspec(GLA_QK, Z_GK), zspec(GLA_V, Z_GV), zspec(GLA_V, Z_GZ),
                zspec(LANES, Z_GA), zspec(SWA_W, Z_SQ), zspec(SWA_W, Z_SZ), zspec(X_W, Z_MQ),
                zspec(X_W, Z_MZ), pl.BlockSpec((rows, KV_COLS), lambda i: (i, 0)),
                _layer_weight_spec(wa, layer), const((1, GLA_QK)), const((1, GLA_DV)),
                state_spec, cache_spec, cache_spec, mem_spec, mem_spec]
    args = [sink] + [z] * 9 + [kv, wa, ba, gout, s0, ck, cv, mk, mv]
    aliases = {}
    if prev is not None:
        aliases = {len(args) + k: 3 + k for k in range(3)}
        in_specs += [pl.BlockSpec(memory_space=pl.ANY)] * 3
        args += list(prev)
    out_row = lambda width: pl.BlockSpec((rows, width), lambda i: (i, 0))
    return pl.pallas_call(
        functools.partial(_sample_mixers_kernel, seq=seq),
        grid=(batch // group,),
        in_specs=in_specs,
        out_specs=[out_row(GLA_V), out_row(SWA_W), out_row(X_W), state_spec, cache_spec, cache_spec],
        out_shape=[jax.ShapeDtypeStruct((batch * seq, GLA_V), BF16),
                   jax.ShapeDtypeStruct((batch * seq, SWA_W), BF16),
                   jax.ShapeDtypeStruct((batch * seq, X_W), BF16),
                   jax.ShapeDtypeStruct(s0.shape, F32), jax.ShapeDtypeStruct(ck.shape, F32),
                   jax.ShapeDtypeStruct(cv.shape, F32)],
        input_output_aliases=aliases,
        compiler_params=_params("parallel"),
        name="sample_mixers",
    )(*args)


FRONT_ROWS = 256
FRONT_GLA_CHUNK = 2 * GLA_CHUNK

def _front_kernel(sink_ref, x_ref, g_ref, w_ref, wa_ref, ba_ref, gout_ref, mk_ref, mv_ref, *rest):
    (ogla_ref, oswa_ref, omem_ref, mg_ref, kvl_ref, s_ref,
     gq_scr, gk_scr, gv_scr, gz_scr, ga_scr, sq_scr, sz_scr, mq_scr, mz_scr, kv_scr, kvp_scr) = rest[-17:]
    n = pl.program_id(1)
    tm = FRONT_ROWS

    @pl.when(n == 0)
    def _():
        s_ref[...] = jnp.zeros_like(s_ref)
        kvp_scr[...] = jnp.zeros_like(kvp_scr)

    h = _rmsnorm_rows(x_ref[...], g_ref[...]).astype(BF16)

    def project(chunk):
        dst, dst_c0, w_r0, width = chunk
        if dst is ga_scr:
            dst[...] = _project_decay(h, w_ref).astype(dst.dtype)
        else:
            dst[:, dst_c0:dst_c0 + width] = _project(h, w_ref, w_r0, width).astype(dst.dtype)

    def chunks_of(dst, w_r0, width):
        return [(dst, c, w_r0 + c, min(MXU_COLS, width - c)) for c in range(0, width, MXU_COLS)]

    swa_cols = chunks_of(kv_scr, W_SK, KV_COLS) + chunks_of(sq_scr, W_SQ, SWA_W) \
        + chunks_of(sz_scr, W_SZ, SWA_W)
    gla_cols = (chunks_of(ga_scr, W_GA, LANES) + chunks_of(gq_scr, W_GQ, GLA_QK)
                + chunks_of(gk_scr, W_GK, GLA_QK) + chunks_of(gv_scr, W_GV, GLA_V)
                + chunks_of(gz_scr, W_GZ, GLA_V))
    mem_cols = chunks_of(mq_scr, W_MQ, X_W) + chunks_of(mz_scr, W_MZ, X_W)
    merge_cols = chunks_of(mg_ref, W_MG, N_BRANCH * D_MODEL)

    i = lax.broadcasted_iota(jnp.int32, (WINDOW, 2 * WINDOW), 0)
    j = lax.broadcasted_iota(jnp.int32, (WINDOW, 2 * WINDOW), 1)
    diff = i + WINDOW - j
    band = (diff >= 0) & (diff <= WINDOW)
    swa = {}

    def swa_start(blk):
        prev = kvp_scr[...] if blk == 0 else kv_scr[(blk - 1) * WINDOW:blk * WINDOW, :]
        kvb = jnp.concatenate([prev, kv_scr[blk * WINDOW:(blk + 1) * WINDOW, :]], axis=0)
        swa["ops"] = _swa_operands(kvb[:, :SWA_KV], kvb[:, SWA_KV:])
        swa["valid"] = band & (j >= jnp.where(n > 0, 0, WINDOW)) if blk == 0 else band

    def swa_scores(blk, kh):
        swa["s"] = _swa_scores(sq_scr[blk * WINDOW:(blk + 1) * WINDOW, :], kh, swa["ops"][kh][0])
        swa["soft"] = []

    def swa_soft(kh, hb):
        swa["soft"].append(_swa_soft_head(swa["s"][hb * WINDOW:(hb + 1) * WINDOW], swa["valid"],
                                          sink_ref[_swa_head_of_block(kh, hb)]))

    def swa_out(blk, kh):
        _, v_even, v_odd = swa["ops"][kh]
        o = _swa_out([p for p, _ in swa["soft"]], [r for _, r in swa["soft"]], v_even, v_odd)
        rows = slice(blk * WINDOW, (blk + 1) * WINDOW)
        cols = slice(kh * SWA_G * SWA_HD, (kh + 1) * SWA_G * SWA_HD)
        oswa_ref[rows, cols] = (o * _silu(sz_scr[rows, cols].astype(F32))).astype(BF16)

    swa_steps = []
    for blk in range(tm // WINDOW):
        swa_steps.append((functools.partial(swa_start, blk), 50))
        for kh in range(SWA_KVH):
            swa_steps.append((functools.partial(swa_scores, blk, kh), 30))
            swa_steps += [(functools.partial(swa_soft, kh, hb), 40) for hb in range(SWA_G)]
            swa_steps.append((functools.partial(swa_out, blk, kh), 75))

    mem = {}

    def mem_scores(hd):
        cols = slice(hd * X_HD, (hd + 1) * X_HD)
        p, rinv = _mem_softmax(_dot_nt(mq_scr[:, cols], mk_ref[:, cols]) * (X_HD ** -0.5))
        mem["p"], mem["rinv"] = p.astype(BF16), rinv

    def mem_out(hd):
        cols = slice(hd * X_HD, (hd + 1) * X_HD)
        o = _dot(mem["p"], mv_ref[:, cols]) * mem["rinv"]
        omem_ref[:, cols] = (o * _silu(mz_scr[:, cols].astype(F32))).astype(BF16)

    mem_steps = []
    for hd in range(X_H):
        mem_steps += [(functools.partial(mem_scores, hd), 130), (functools.partial(mem_out, hd), 160)]

    gla_steps = _gla_thunks(gq_scr, gk_scr, gv_scr, gz_scr, ga_scr, wa_ref, ba_ref, gout_ref, None,
                            ogla_ref, s_ref, chunk=FRONT_GLA_CHUNK, nchunks=tm // FRONT_GLA_CHUNK,
                            carry=True, anchored=True)

    for chunk in swa_cols:
        project(chunk)
    chunks = gla_cols + mem_cols + merge_cols
    steps = [(t, w, 0) for t, w in swa_steps]
    steps += [(t, w, len(gla_cols)) for t, w in gla_steps]
    steps += [(t, w, len(gla_cols) + len(mem_cols)) for t, w in mem_steps]
    _interleave(steps, chunks, project)

    kv_tail = kv_scr[tm - WINDOW:, :]
    kvp_scr[...] = kv_tail
    kvl_ref[...] = kv_tail.T


def _front(x2d, g, w, sink, wa, ba, gout, mk, mv, prev_state, layer, depth, batch, seq):
    t, d = x2d.shape
    tm = FRONT_ROWS
    nblk = seq // tm
    row = lambda width: pl.BlockSpec((tm, width), lambda b, n: (b * nblk + n, 0))
    const = lambda shape: pl.BlockSpec(shape, lambda b, n: (0,) * len(shape))
    mem_spec = pl.BlockSpec((MEM_LEN, X_W), lambda b, n: (b, 0))
    state_spec = pl.BlockSpec((None, None, GLA_H, GLA_DK, GLA_DV), lambda b, n: (layer, b, 0, 0, 0))
    in_specs = [pl.BlockSpec(memory_space=pltpu.SMEM), row(d), const((1, d)),
                _layer_weight_spec(w, layer, resident=True),
                _layer_weight_spec(wa, layer), const((1, GLA_QK)), const((1, GLA_DV)),
                mem_spec, mem_spec]
    args = [sink, x2d, g, w, wa, ba, gout, mk, mv]
    aliases = {}
    if prev_state is not None:
        aliases = {len(args): 5}
        in_specs.append(pl.BlockSpec(memory_space=pl.ANY))
        args.append(prev_state)
    scratch = [pltpu.VMEM((tm, width), BF16) for width in
               (GLA_QK, GLA_QK, GLA_V, GLA_V, LANES, SWA_W, SWA_W, X_W, X_W)]
    scratch += [pltpu.VMEM((tm, KV_COLS), F32), pltpu.VMEM((WINDOW, KV_COLS), F32)]
    return pl.pallas_call(
        _front_kernel,
        grid=(batch, nblk),
        in_specs=in_specs,
        out_specs=[row(GLA_V), row(SWA_W), row(X_W), row(N_BRANCH * D_MODEL),
                   pl.BlockSpec((None, KV_COLS, WINDOW), lambda b, n: (b, 0, 0)), state_spec],
        out_shape=[jax.ShapeDtypeStruct((t, GLA_V), BF16), jax.ShapeDtypeStruct((t, SWA_W), BF16),
                   jax.ShapeDtypeStruct((t, X_W), BF16),
                   jax.ShapeDtypeStruct((t, N_BRANCH * D_MODEL), BF16),
                   jax.ShapeDtypeStruct((batch, KV_COLS, WINDOW), F32),
                   jax.ShapeDtypeStruct((depth, batch, GLA_H, GLA_DK, GLA_DV), F32)],
        scratch_shapes=scratch,
        input_output_aliases=aliases,
        compiler_params=_params("parallel", "arbitrary"),
        name="front",
    )(*args)


def _merge_kernel(x_ref, a_ref, s_ref, m_ref, ga_ref, gs_ref, gm_ref, wa_ref, ws_ref, wm_ref,
                  wo_ref, *rest, final):
    if final:
        gf_ref, y_ref = rest
    else:
        (y_ref,) = rest
    merged = (_sigmoid(ga_ref[...].astype(F32)) * _dot(a_ref[...], wa_ref[...])
              + _sigmoid(gs_ref[...].astype(F32)) * _dot(s_ref[...], ws_ref[...])
              + _sigmoid(gm_ref[...].astype(F32)) * _dot(m_ref[...], wm_ref[...]))
    y = x_ref[...] + _dot(merged.astype(BF16), wo_ref[...])
    if final:
        y = _rmsnorm_rows(y, gf_ref[...])
    y_ref[...] = y


def _merge(x2d, o_gla, o_swa, o_mem, gates, gate_col0, w_gla, w_swa, w_mem, w_out, g_final, layer):
    t, d = x2d.shape
    tm = min(t, 2 * _row_tile(t))
    final = g_final is not None
    row = lambda width: pl.BlockSpec((tm, width), lambda i: (i, 0))
    gate = lambda k: pl.BlockSpec((tm, d), lambda i: (i, gate_col0 // d + k))
    in_specs = [row(d), row(d), row(d), row(d), gate(0), gate(1), gate(2)]
    in_specs += [_layer_weight_spec(w, layer, resident=True) for w in (w_gla, w_swa, w_mem, w_out)]
    args = [x2d, o_gla, o_swa, o_mem, gates, gates, gates, w_gla, w_swa, w_mem, w_out]
    if final:
        in_specs.append(pl.BlockSpec((1, d), lambda i: (0, 0)))
        args.append(g_final)
    return pl.pallas_call(
        functools.partial(_merge_kernel, final=final),
        grid=(t // tm,),
        in_specs=in_specs,
        out_specs=row(d),
        out_shape=jax.ShapeDtypeStruct((t, d), F32),
        compiler_params=_params("parallel"),
        name="merge_final" if final else "merge",
    )(*args)


def kernel(x_prompt, x_sample, mem_prompt, state_gla, cache_swa_k, cache_swa_v, cache_mem_k,
           cache_mem_v, g_norm, w_in, w_gla_a, b_gla_a, g_gla_out, swa_sink, g_mem, w_mem_kv,
           w_br_gla, w_br_swa, w_br_mem, w_out, g_final):
    depth = w_in.shape[0]
    batch, seq, d = x_prompt.shape
    dec_batch, dec_seq, _ = x_sample.shape
    wb = cache_swa_k.shape[2]

    w_in_r = w_in.transpose(0, 2, 1).astype(BF16)
    w_a = jnp.pad(w_gla_a, ((0, 0), (0, LANES - GLA_RANK), (0, 0))).astype(BF16)
    w_mem_b = w_mem_kv.astype(BF16)
    w_g, w_s, w_m, w_o = (w.astype(BF16) for w in (w_br_gla, w_br_swa, w_br_mem, w_out))
    g_fin = g_final.reshape(1, d)

    yp = x_prompt.reshape(batch * seq, d)
    ys = x_sample.reshape(dec_batch * dec_seq, d)
    mem2d = mem_prompt.reshape(batch * MEM_LEN, d)
    swa_t = lambda c: c.transpose(0, 1, 3, 4, 2).reshape(depth, dec_batch, SWA_KV, wb)
    ck_all, cv_all = swa_t(cache_swa_k), swa_t(cache_swa_v)
    cmk_all = _mem_rows_view(cache_mem_k)
    cmv_all = _mem_rows_view(cache_mem_v)
    gla_p = sample_new = mem_new = None
    kp_l, vp_l = [], []
    for l in range(depth):
        last = l == depth - 1
        gn = g_norm[l].reshape(1, d)
        ba = b_gla_a[l].reshape(1, GLA_QK)
        gout = g_gla_out[l].reshape(1, GLA_DV)
        sink = swa_sink[l]
        branch_w = (w_g, w_s, w_m, w_o, g_fin if last else None, l)

        mk, mv, *mem_new = _mem_kv(mem2d, g_mem[l].reshape(1, d), w_mem_b, mem_new, l, depth,
                                   batch)
        o_gla, o_swa, o_mem, gates, kv_tail, gla_p = _front(
            yp, gn, w_in_r, sink, w_a, ba, gout, mk, mv, gla_p, l, depth, batch, seq)
        yp = _merge(yp, o_gla, o_swa, o_mem, gates, 0, *branch_w)
        kv_heads = kv_tail.reshape(batch, 2, SWA_KVH, SWA_HD, WINDOW).transpose(0, 1, 4, 2, 3)
        kp_l.append(kv_heads[:, 0])
        vp_l.append(kv_heads[:, 1])

        z, kv = _in_proj(ys, gn, w_in_r, l)
        o_gla, o_swa, o_mem, *sample_new = _sample_mixers(
            z, kv, sink, w_a, ba, gout, state_gla, ck_all, cv_all, cmk_all, cmv_all, sample_new, l,
            dec_batch, dec_seq)
        ys = _merge(ys, o_gla, o_swa, o_mem, z, Z_MG, *branch_w)

    gla_s = sample_new[0]
    k_s, v_s = (c.reshape(depth, dec_batch, SWA_KVH, SWA_HD, wb).transpose(0, 1, 4, 2, 3)
                for c in sample_new[1:])
    mk_p, mv_p = (_mem_from_rows_view(c) for c in mem_new)
    return (yp.reshape(batch, seq, d), ys.reshape(dec_batch, dec_seq, d), gla_p,
            jnp.stack(kp_l), jnp.stack(vp_l), mk_p, mv_p, gla_s, k_s, v_s)
```

```python
import functools

import jax
import jax.numpy as jnp
from jax import lax
from jax.experimental import pallas as pl
from jax.experimental.pallas import tpu as pltpu

F32 = jnp.float32
BF16 = jnp.bfloat16

D_MODEL = 1024
GLA_H = 4
GLA_DK = 128
GLA_DV = 256
GLA_QK = GLA_H * GLA_DK
GLA_V = GLA_H * GLA_DV
GLA_RANK = 16
GLA_TAU = 16.0
GLA_CHUNK = 32
SWA_HD = 64
SWA_H = 16
SWA_KVH = 2
SWA_G = SWA_H // SWA_KVH
SWA_W = SWA_H * SWA_HD
SWA_KV = SWA_KVH * SWA_HD
WINDOW = 128
MEM_LEN = 256
X_H = 4
X_HD = 256
X_W = X_H * X_HD
N_BRANCH = 3
IN_SPLITS = (GLA_QK, GLA_QK, GLA_V, GLA_V, GLA_RANK, SWA_W, SWA_KV, SWA_KV, SWA_W, X_W, X_W,
             N_BRANCH * D_MODEL)
EPS = 1e-6

LANES = 128
BF16_ROWS = 16
MXU_COLS = 256
IN_PROJ_CHUNK = 2 * MXU_COLS
VMEM_LIMIT = 56 * 1024 * 1024

Z_GQ = 0
Z_GK = Z_GQ + GLA_QK
Z_GV = Z_GK + GLA_QK
Z_GZ = Z_GV + GLA_V
Z_SQ = Z_GZ + GLA_V
Z_SZ = Z_SQ + SWA_W
Z_MQ = Z_SZ + SWA_W
Z_MZ = Z_MQ + X_W
Z_MG = Z_MZ + X_W
Z_GA = Z_MG + N_BRANCH * D_MODEL
Z_COLS = Z_GA + LANES
KV_COLS = 2 * SWA_KV

(W_GQ, W_GK, W_GV, W_GZ, W_GA, W_SQ, W_SK, W_SV, W_SZ, W_MQ, W_MZ, W_MG) = (
    sum(IN_SPLITS[:i]) for i in range(len(IN_SPLITS)))
Z_PIECES = ((Z_GQ, W_GQ, GLA_QK), (Z_GK, W_GK, GLA_QK), (Z_GV, W_GV, GLA_V), (Z_GZ, W_GZ, GLA_V),
            (Z_SQ, W_SQ, SWA_W), (Z_SZ, W_SZ, SWA_W), (Z_MQ, W_MQ, X_W), (Z_MZ, W_MZ, X_W),
            (Z_MG, W_MG, N_BRANCH * D_MODEL))


def _project(h, w_ref, row0, width):
    return _dot_nt(h, w_ref[row0:row0 + width, :])


def _project_decay(h, w_ref):
    res = _project(h, w_ref, W_GA, LANES)
    lane = lax.broadcasted_iota(jnp.int32, res.shape, 1)
    return jnp.where(lane < GLA_RANK, res, 0.0)


def _row_tile(rows):
    return 512 if rows >= 4096 else 256


def _params(*semantics):
    return pltpu.CompilerParams(dimension_semantics=semantics, vmem_limit_bytes=VMEM_LIMIT)


def _sigmoid(x):
    return 0.5 * jnp.tanh(0.5 * x) + 0.5


def _silu(x):
    return x * _sigmoid(x)


def _log_sigmoid(x):
    return jnp.minimum(x, 0.0) - jnp.log(1.0 + jnp.exp(-jnp.abs(x)))


def _rmsnorm_rows(x, g):
    return x * lax.rsqrt(jnp.mean(x * x, axis=-1, keepdims=True) + EPS) * g


def _dot(a, b):
    return jnp.dot(a, b, preferred_element_type=F32)


def _dot_nt(a, b):
    return lax.dot_general(a, b, (((1,), (1,)), ((), ())), preferred_element_type=F32)


def _dot_tn(a, b):
    return lax.dot_general(a, b, (((0,), (0,)), ((), ())), preferred_element_type=F32)


def _pad_rows(x, rows):
    if x.shape[0] >= rows:
        return x
    return jnp.concatenate([x, jnp.zeros((rows - x.shape[0], x.shape[1]), x.dtype)], axis=0)


def _interleave(steps, chunks, run_chunk):
    total = sum(w for _, w, _ in steps)
    done, acc = 0, 0
    for thunk, weight, needs in steps:
        acc += weight
        upto = max(needs, min(len(chunks), -(-acc * len(chunks) // total)))
        for chunk in chunks[done:upto]:
            run_chunk(chunk)
        done = max(done, upto)
        thunk()


def _in_proj_kernel(x_ref, g_ref, w_ref, z_ref, kv_ref):
    h = _rmsnorm_rows(x_ref[...], g_ref[...]).astype(BF16)
    for z0, w0, width in Z_PIECES:
        for c in range(0, width, IN_PROJ_CHUNK):
            z_ref[:, z0 + c:z0 + c + IN_PROJ_CHUNK] = _project(h, w_ref, w0 + c,
                                                               IN_PROJ_CHUNK).astype(BF16)
    z_ref[:, Z_GA:Z_COLS] = _project_decay(h, w_ref).astype(BF16)
    kv_ref[...] = _project(h, w_ref, W_SK, KV_COLS)


def _layer_weight_spec(w, layer, resident=False):
    mode = dict(pipeline_mode=pl.Buffered(1)) if resident else {}
    return pl.BlockSpec((None,) + w.shape[1:], lambda *ids: (layer, 0, 0), **mode)


def _in_proj(x2d, g, w, layer):
    t, d = x2d.shape
    tm = _row_tile(t)
    return pl.pallas_call(
        _in_proj_kernel,
        grid=(t // tm,),
        in_specs=[
            pl.BlockSpec((tm, d), lambda i: (i, 0)),
            pl.BlockSpec((1, d), lambda i: (0, 0)),
            _layer_weight_spec(w, layer, resident=True),
        ],
        out_specs=[pl.BlockSpec((tm, Z_COLS), lambda i: (i, 0)),
                   pl.BlockSpec((tm, KV_COLS), lambda i: (i, 0))],
        out_shape=[jax.ShapeDtypeStruct((t, Z_COLS), BF16), jax.ShapeDtypeStruct((t, KV_COLS), F32)],
        compiler_params=_params("parallel"),
        name="in_proj",
    )(x2d, g, w)


GLA_STEP_COST = dict(prepare=100, decay=230, factors=170, intra=60, step=130, epilogue=190)
SWA_STEP_COST = dict(start=50, scores=30, soft=40, out=75)
MEM_STEP_COST = dict(scores=130, out=160)


def _chunk_cumsum(g, chunk):
    pos = lax.broadcasted_iota(jnp.int32, g.shape, 0) & (chunk - 1)
    b = g
    shift = 1
    while shift < chunk:
        b = b + jnp.where(pos >= shift, pltpu.roll(b, shift, 0), 0.0)
        shift *= 2
    return b


def _gla_thunks(q_ref, k_ref, v_ref, gz_ref, ga_ref, wa_ref, ba_ref, gout_ref, s0_ref, o_ref,
                s_ref, *, chunk, nchunks, carry, anchored):
    rows = chunk * nchunks
    mm_rows = max(chunk, BF16_ROWS)
    heads = range(GLA_H)
    lanes = [slice(h * GLA_DK, (h + 1) * GLA_DK) for h in heads]
    vl = [slice(h * GLA_DV, (h + 1) * GLA_DV) for h in heads]
    per_head = lambda: [None] * GLA_H
    st = {name: per_head() for name in ("b", "qb", "kd", "qb_bf", "kd_bf", "qa_bf", "ka_bf",
                                        "dec_cols", "o_intra")}
    st["o_inter"] = [[] for _ in heads]

    def chunk_row(x, row):
        picked = [x[j * chunk + row:j * chunk + row + 1, :] for j in range(nchunks)]
        return picked, jnp.concatenate([jnp.broadcast_to(r, (chunk, x.shape[1])) for r in picked],
                                       axis=0)

    small = chunk < BF16_ROWS

    def chunk_rows(x_f32, x_bf, sl):
        return _pad_rows(x_f32[sl], mm_rows).astype(BF16) if small else x_bf[sl]

    def prepare():
        r_idx = lax.broadcasted_iota(jnp.int32, (rows, rows), 0)
        c_idx = lax.broadcasted_iota(jnp.int32, (rows, rows), 1)
        diff = r_idx - c_idx
        st["mask"] = (diff >= 0) & (diff <= (r_idx & (chunk - 1)))
        st["v_f32"] = v_ref[...].astype(F32) if small else None
        if carry:
            st["states"] = [s_ref[h] for h in heads]

    def decay(h):
        x = _dot(ga_ref[...], wa_ref[:, lanes[h]]) + ba_ref[:, lanes[h]]
        st["b"][h] = _chunk_cumsum(_log_sigmoid(x) * (1.0 / GLA_TAU), chunk)

    def factors(h):
        b = st["b"][h]
        b_last, b_last_full = chunk_row(b, chunk - 1)
        b_last_rows = jnp.concatenate(b_last + [jnp.zeros((LANES - nchunks, GLA_DK), F32)], axis=0)
        st["dec_cols"][h] = jnp.exp(b_last_rows.T)
        q = q_ref[:, lanes[h]].astype(F32) * (GLA_DK ** -0.5)
        k = k_ref[:, lanes[h]].astype(F32)
        qb = q * jnp.exp(b)
        kd = k * jnp.exp(b_last_full - b)
        st["qb"][h], st["kd"][h] = (qb, kd) if small else (None, None)
        st["qb_bf"][h], st["kd_bf"][h] = qb.astype(BF16), kd.astype(BF16)
        if anchored:
            _, b_mid_full = chunk_row(b, chunk // 2 - 1)
            st["qa_bf"][h] = (q * jnp.exp(b - b_mid_full)).astype(BF16)
            st["ka_bf"][h] = (k * jnp.exp(b_mid_full - b)).astype(BF16)
        else:
            st["qa_bf"][h] = st["qb_bf"][h]
            st["ka_bf"][h] = (k * jnp.exp(-b)).astype(BF16)

    def intra(h):
        a = jnp.where(st["mask"], _dot_nt(st["qa_bf"][h], st["ka_bf"][h]), 0.0)
        st["o_intra"][h] = _dot(a.astype(BF16), v_ref[:, vl[h]])

    def step(j):
        sl = slice(j * chunk, (j + 1) * chunk)
        for h in heads:
            state = st["states"][h] if carry else s0_ref[j, h]
            qb_j = chunk_rows(st["qb"][h], st["qb_bf"][h], sl)
            st["o_inter"][h].append(_dot(qb_j, state.astype(BF16))[:chunk])
            if small:
                v_j = _pad_rows(st["v_f32"][sl, vl[h]], mm_rows).astype(BF16)
            else:
                v_j = v_ref[sl, vl[h]]
            kd_j = chunk_rows(st["kd"][h], st["kd_bf"][h], sl)
            new_state = st["dec_cols"][h][:, j:j + 1] * state + _dot_tn(kd_j, v_j)
            if carry:
                st["states"][h] = new_state
            else:
                s_ref[j, h] = new_state

    def epilogue(h):
        if carry:
            s_ref[h] = st["states"][h]
        o = st["o_intra"][h] + jnp.concatenate(st["o_inter"][h], axis=0)
        gate = _silu(gz_ref[:, vl[h]].astype(F32))
        o_ref[:, vl[h]] = (_rmsnorm_rows(o, gout_ref[...]) * gate).astype(BF16)

    cost = GLA_STEP_COST
    steps = [(prepare, cost["prepare"])]
    steps += [(functools.partial(decay, h), cost["decay"]) for h in heads]
    steps += [(functools.partial(factors, h), cost["factors"]) for h in heads]
    steps += [(functools.partial(intra, h), cost["intra"]) for h in heads]
    steps += [(functools.partial(step, j), cost["step"]) for j in range(nchunks)]
    steps += [(functools.partial(epilogue, h), cost["epilogue"]) for h in heads]
    return steps


def _gla_sample_kernel(q_ref, k_ref, v_ref, gz_ref, ga_ref, wa_ref, ba_ref, gout_ref, s0_ref, *rest,
                       seq, group):
    o_ref, s_ref = rest[-2:]
    for thunk, _ in _gla_thunks(q_ref, k_ref, v_ref, gz_ref, ga_ref, wa_ref, ba_ref, gout_ref,
                                s0_ref, o_ref, s_ref, chunk=seq, nchunks=group, carry=False,
                                anchored=False):
        thunk()


def _swa_head_of_block(kv_head, blk):
    pairs = SWA_G // 2
    within = 2 * blk if blk < pairs else 2 * (blk - pairs) + 1
    return kv_head * SWA_G + within


def _swa_operands(kk, vv):
    lo = lax.broadcasted_iota(jnp.int32, kk.shape, 1) < SWA_HD
    kk_sw = pltpu.roll(kk, SWA_HD, 1)
    vv_sw = pltpu.roll(vv, SWA_HD, 1)
    ops = []
    for kh in range(SWA_KVH):
        k_this, k_other = (kk, kk_sw) if kh == 0 else (kk_sw, kk)
        v_this, v_other = (vv, vv_sw) if kh == 0 else (vv_sw, vv)
        ops.append((jnp.where(lo, k_this, k_other).astype(BF16),
                    jnp.where(lo, v_this, 0.0).astype(BF16),
                    jnp.where(lo, 0.0, v_other).astype(BF16)))
    return ops


def _swa_operands_t(kk_t, vv_t):
    zeros = jnp.zeros((SWA_HD, kk_t.shape[1]), F32)
    ops = []
    for kh in range(SWA_KVH):
        k_h = kk_t[kh * SWA_HD:(kh + 1) * SWA_HD]
        v_h = vv_t[kh * SWA_HD:(kh + 1) * SWA_HD]
        ops.append((jnp.concatenate([k_h, k_h], axis=0).astype(BF16),
                    jnp.concatenate([v_h, zeros], axis=0).astype(BF16),
                    jnp.concatenate([zeros, v_h], axis=0).astype(BF16)))
    return ops


def _swa_scores(q, kh, k_dup, keys_on_lanes=False):
    nq = q.shape[0]
    lo = lax.broadcasted_iota(jnp.int32, (nq, LANES), 1) < SWA_HD
    even, odd = [], []
    for p in range(SWA_G // 2):
        c0 = kh * SWA_G * SWA_HD + p * LANES
        q_pair = q[:, c0:c0 + LANES]
        even.append(jnp.where(lo, q_pair, jnp.zeros_like(q_pair)))
        odd.append(jnp.where(lo, jnp.zeros_like(q_pair), q_pair))
    lhs = jnp.concatenate(even + odd, axis=0).astype(BF16)
    s = _dot(lhs, k_dup) if keys_on_lanes else _dot_nt(lhs, k_dup)
    return s * (SWA_HD ** -0.5)


def _swa_soft_head(s_h, valid, sink):
    s_h = jnp.where(valid, s_h, -jnp.inf)
    m = jnp.maximum(jnp.max(s_h, axis=-1, keepdims=True), sink)
    p_h = jnp.exp(s_h - m)
    den = jnp.sum(p_h, axis=-1, keepdims=True) + jnp.exp(sink - m)
    if p_h.shape[0] % BF16_ROWS == 0:
        p_h = p_h.astype(BF16)
    return p_h, 1.0 / den


def _swa_out(probs, rinv, v_even, v_odd):
    pairs = SWA_G // 2
    nq = probs[0].shape[0]
    lo = lax.broadcasted_iota(jnp.int32, (nq, LANES), 1) < SWA_HD
    p_even = jnp.concatenate(probs[:pairs], axis=0).astype(BF16)
    p_odd = jnp.concatenate(probs[pairs:], axis=0).astype(BF16)
    o = _dot(p_even, v_even) + _dot(p_odd, v_odd)
    outs = [o[p * nq:(p + 1) * nq] * jnp.where(lo, rinv[p], rinv[pairs + p]) for p in range(pairs)]
    return jnp.concatenate(outs, axis=1)


def _swa_sample_kernel(sink_ref, q_ref, sz_ref, kvn_ref, ck_ref, cv_ref, *rest, group, seq):
    o_ref, nk_ref, nv_ref = rest[-3:]
    wb = ck_ref.shape[2]
    nk = 2 * wb
    pairs = SWA_G // 2
    blk_rows = SWA_G * seq
    first = lax.broadcasted_iota(jnp.int32, (SWA_KV, wb), 1) < seq
    q = q_ref[...].astype(F32)
    kvn = _pad_rows(kvn_ref[...], LANES)
    kn_all, vn_all = kvn[:, :SWA_KV].T, kvn[:, SWA_KV:].T

    scores, values = [], []
    for e in range(group):
        rows = slice(e * seq, (e + 1) * seq)
        k_old, v_old = ck_ref[e], cv_ref[e]
        shift = (LANES - e * seq) % LANES
        k_new = pltpu.roll(kn_all, shift, 1) if shift else kn_all
        v_new = pltpu.roll(vn_all, shift, 1) if shift else vn_all
        nk_ref[e] = pltpu.roll(jnp.where(first, k_new, k_old), wb - seq, 1)
        nv_ref[e] = pltpu.roll(jnp.where(first, v_new, v_old), wb - seq, 1)
        ops = _swa_operands_t(jnp.concatenate([k_old, k_new], axis=1),
                              jnp.concatenate([v_old, v_new], axis=1))
        for kh in range(SWA_KVH):
            k_dup, v_even, v_odd = ops[kh]
            scores.append(_swa_scores(q[rows], kh, k_dup, keys_on_lanes=True))
            values.append((v_even, v_odd))
    s_all = jnp.concatenate(scores, axis=0)
    i = lax.broadcasted_iota(jnp.int32, s_all.shape, 0) & (seq - 1)
    j = lax.broadcasted_iota(jnp.int32, s_all.shape, 1)
    diff = i + wb - j
    sink_kh = [jnp.concatenate([jnp.full((seq, 1), sink_ref[_swa_head_of_block(kh, blk)], F32)
                                for blk in range(SWA_G)], axis=0) for kh in range(SWA_KVH)]
    p_all, rinv_all = _swa_soft_head(s_all, (diff >= 0) & (diff <= WINDOW),
                                     jnp.concatenate(sink_kh * group, axis=0))

    lo = lax.broadcasted_iota(jnp.int32, (seq, LANES), 1) < SWA_HD
    outs = [[] for _ in range(group)]
    for idx, (v_even, v_odd) in enumerate(values):
        base = idx * blk_rows
        half = pairs * seq
        o = (_dot_nt(p_all[base:base + half], v_even)
             + _dot_nt(p_all[base + half:base + blk_rows], v_odd))
        for p in range(pairs):
            r_even = rinv_all[base + p * seq:base + (p + 1) * seq]
            r_odd = rinv_all[base + half + p * seq:base + half + (p + 1) * seq]
            outs[idx // SWA_KVH].append(o[p * seq:(p + 1) * seq] * jnp.where(lo, r_even, r_odd))
    o = jnp.concatenate([jnp.concatenate(row, axis=1) for row in outs], axis=0)
    o_ref[...] = (o * _silu(sz_ref[...].astype(F32))).astype(BF16)


MEM_HALVES = X_HD // LANES
MEM_TILE_ROWS = X_H * MEM_HALVES


def _mem_rows_view(x):
    lead = x.shape[:-3]
    n = len(lead)
    x = x.reshape(*lead, MEM_LEN, X_H, MEM_HALVES, LANES)
    x = x.transpose(*range(n), n, n + 2, n + 1, n + 3)
    return x.reshape(*lead, MEM_LEN * MEM_TILE_ROWS, LANES)


def _mem_from_rows_view(x):
    lead = x.shape[:-2]
    n = len(lead)
    x = x.reshape(*lead, MEM_LEN, MEM_HALVES, X_H, LANES)
    x = x.transpose(*range(n), n, n + 2, n + 1, n + 3)
    return x.reshape(*lead, MEM_LEN, X_H, X_HD)


def _mem_head_rows(head, half):
    return pl.ds(half * X_H + head, MEM_LEN, stride=MEM_TILE_ROWS)


def _mem_kv_kernel(x_ref, g_ref, w_ref, *rest):
    kb_ref, vb_ref, kn_ref, vn_ref = rest[-4:]
    h = _rmsnorm_rows(x_ref[...], g_ref[...]).astype(BF16)
    for t, (b_ref, n_ref) in enumerate(((kb_ref, kn_ref), (vb_ref, vn_ref))):
        for head in range(X_H):
            c0 = head * X_HD
            res = _dot(h, w_ref[:, t * X_W + c0:t * X_W + c0 + X_HD])
            b_ref[:, c0:c0 + X_HD] = res.astype(BF16)
            for half in range(MEM_HALVES):
                n_ref[_mem_head_rows(head, half), :] = res[:, half * LANES:(half + 1) * LANES]


def _mem_kv(mem2d, g, w, prev, layer, depth, batch):
    d = mem2d.shape[1]
    rows_spec = pl.BlockSpec((None, None, MEM_LEN * MEM_TILE_ROWS, LANES), lambda i: (layer, i, 0, 0))
    rows_shape = jax.ShapeDtypeStruct((depth, batch, MEM_LEN * MEM_TILE_ROWS, LANES), F32)
    dense_spec = pl.BlockSpec((MEM_LEN, X_W), lambda i: (i, 0))
    dense_shape = jax.ShapeDtypeStruct((batch * MEM_LEN, X_W), BF16)
    in_specs = [pl.BlockSpec((MEM_LEN, d), lambda i: (i, 0)),
                pl.BlockSpec((1, d), lambda i: (0, 0)),
                _layer_weight_spec(w, layer)]
    args = [mem2d, g, w]
    aliases = {}
    if prev is not None:
        aliases = {len(args): 2, len(args) + 1: 3}
        in_specs += [pl.BlockSpec(memory_space=pl.ANY)] * 2
        args += list(prev)
    return pl.pallas_call(
        _mem_kv_kernel,
        grid=(batch,),
        in_specs=in_specs,
        out_specs=[dense_spec, dense_spec, rows_spec, rows_spec],
        out_shape=[dense_shape, dense_shape, rows_shape, rows_shape],
        input_output_aliases=aliases,
        compiler_params=_params("parallel"),
        name="mem_kv",
    )(*args)


def _mem_softmax(s):
    p = jnp.exp(s - jnp.max(s, axis=-1, keepdims=True))
    return p, 1.0 / jnp.sum(p, axis=-1, keepdims=True)


def _mem_sample_kernel(q_ref, mz_ref, mk_ref, mv_ref, o_ref, *, group, seq):
    def head_tile(ref, e, h):
        halves = [ref[e, _mem_head_rows(h, half), :] for half in range(MEM_HALVES)]
        return jnp.concatenate(halves, axis=1).astype(BF16)

    q = q_ref[...].astype(F32)
    pairs = [(e, h) for e in range(group) for h in range(X_H)]
    scores = []
    for e, h in pairs:
        q_eh = _pad_rows(q[e * seq:(e + 1) * seq, h * X_HD:(h + 1) * X_HD], BF16_ROWS).astype(BF16)
        scores.append(_dot_nt(q_eh, head_tile(mk_ref, e, h))[:seq])
    p, rinv = _mem_softmax(jnp.concatenate(scores, axis=0) * (X_HD ** -0.5))
    outs = [[] for _ in range(group)]
    for i, (e, h) in enumerate(pairs):
        p_eh = _pad_rows(p[i * seq:(i + 1) * seq], BF16_ROWS).astype(BF16)
        outs[e].append(_dot(p_eh, head_tile(mv_ref, e, h))[:seq] * rinv[i * seq:(i + 1) * seq])
    o = jnp.concatenate([jnp.concatenate(row, axis=1) for row in outs], axis=0)
    o_ref[...] = (o * _silu(mz_ref[...].astype(F32))).astype(BF16)


SAMPLE_GROUP = 4


def _sample_mixers_kernel(sink_ref, gq_ref, gk_ref, gv_ref, gz_ref, ga_ref, sq_ref, sz_ref, mq_ref,
                          mz_ref, kvn_ref, wa_ref, ba_ref, gout_ref, s0_ref, ck_ref, cv_ref, mk_ref,
                          mv_ref, *rest, seq):
    ogla_ref, oswa_ref, omem_ref, s_ref, nk_ref, nv_ref = rest[-6:]
    _mem_sample_kernel(mq_ref, mz_ref, mk_ref, mv_ref, omem_ref, group=SAMPLE_GROUP, seq=seq)
    _swa_sample_kernel(sink_ref, sq_ref, sz_ref, kvn_ref, ck_ref, cv_ref, oswa_ref, nk_ref, nv_ref,
                       group=SAMPLE_GROUP, seq=seq)
    _gla_sample_kernel(gq_ref, gk_ref, gv_ref, gz_ref, ga_ref, wa_ref, ba_ref, gout_ref, s0_ref,
                       ogla_ref, s_ref, seq=seq, group=SAMPLE_GROUP)


def _sample_mixers(z, kv, sink, wa, ba, gout, s0, ck, cv, mk, mv, prev, layer, batch, seq):
    group = SAMPLE_GROUP
    rows = group * seq
    wb = ck.shape[3]
    assert wb == LANES, "the sliding window is one lane tile wide"
    zspec = lambda width, col0: pl.BlockSpec((rows, width), lambda i: (i, col0 // width))
    const = lambda shape: pl.BlockSpec(shape, lambda i: (0,) * len(shape))
    layered = lambda *dims: pl.BlockSpec((None, group) + dims, lambda i: (layer, i) + (0,) * len(dims))
    state_spec = layered(GLA_H, GLA_DK, GLA_DV)
    cache_spec = layered(SWA_KV, wb)
    mem_spec = layered(MEM_LEN * MEM_TILE_ROWS, LANES)
    in_specs = [pl.BlockSpec(memory_space=pltpu.SMEM),
                zspec(GLA_QK, Z_GQ), zspec(GLA_QK, Z_GK), zspec(GLA_V, Z_GV), zspec(GLA_V, Z_GZ),
                zspec(LANES, Z_GA), zspec(SWA_W, Z_SQ), zspec(SWA_W, Z_SZ), zspec(X_W, Z_MQ),
                zspec(X_W, Z_MZ), pl.BlockSpec((rows, KV_COLS), lambda i: (i, 0)),
                _layer_weight_spec(wa, layer), const((1, GLA_QK)), const((1, GLA_DV)),
                state_spec, cache_spec, cache_spec, mem_spec, mem_spec]
    args = [sink] + [z] * 9 + [kv, wa, ba, gout, s0, ck, cv, mk, mv]
    aliases = {}
    if prev is not None:
        aliases = {len(args) + k: 3 + k for k in range(3)}
        in_specs += [pl.BlockSpec(memory_space=pl.ANY)] * 3
        args += list(prev)
    out_row = lambda width: pl.BlockSpec((rows, width), lambda i: (i, 0))
    return pl.pallas_call(
        functools.partial(_sample_mixers_kernel, seq=seq),
        grid=(batch // group,),
        in_specs=in_specs,
        out_specs=[out_row(GLA_V), out_row(SWA_W), out_row(X_W), state_spec, cache_spec, cache_spec],
        out_shape=[jax.ShapeDtypeStruct((batch * seq, GLA_V), BF16),
                   jax.ShapeDtypeStruct((batch * seq, SWA_W), BF16),
                   jax.ShapeDtypeStruct((batch * seq, X_W), BF16),
                   jax.ShapeDtypeStruct(s0.shape, F32), jax.ShapeDtypeStruct(ck.shape, F32),
                   jax.ShapeDtypeStruct(cv.shape, F32)],
        input_output_aliases=aliases,
        compiler_params=_params("parallel"),
        name="sample_mixers",
    )(*args)


FRONT_ROWS = 256
FRONT_GLA_CHUNK = 2 * GLA_CHUNK

def _front_kernel(sink_ref, x_ref, g_ref, w_ref, wa_ref, ba_ref, gout_ref, mk_ref, mv_ref, *rest):
    (ogla_ref, oswa_ref, omem_ref, mg_ref, kvl_ref, s_ref,
     gq_scr, gk_scr, gv_scr, gz_scr, ga_scr, sq_scr, sz_scr, mq_scr, mz_scr, kv_scr, kvp_scr) = rest[-17:]
    n = pl.program_id(1)
    tm = FRONT_ROWS

    @pl.when(n == 0)
    def _():
        s_ref[...] = jnp.zeros_like(s_ref)
        kvp_scr[...] = jnp.zeros_like(kvp_scr)

    h = _rmsnorm_rows(x_ref[...], g_ref[...]).astype(BF16)

    def project(chunk):
        dst, dst_c0, w_r0, width = chunk
        if dst is ga_scr:
            dst[...] = _project_decay(h, w_ref).astype(dst.dtype)
        else:
            dst[:, dst_c0:dst_c0 + width] = _project(h, w_ref, w_r0, width).astype(dst.dtype)

    def chunks_of(dst, w_r0, width):
        return [(dst, c, w_r0 + c, min(MXU_COLS, width - c)) for c in range(0, width, MXU_COLS)]

    swa_cols = chunks_of(kv_scr, W_SK, KV_COLS) + chunks_of(sq_scr, W_SQ, SWA_W) \
        + chunks_of(sz_scr, W_SZ, SWA_W)
    gla_cols = (chunks_of(ga_scr, W_GA, LANES) + chunks_of(gq_scr, W_GQ, GLA_QK)
                + chunks_of(gk_scr, W_GK, GLA_QK) + chunks_of(gv_scr, W_GV, GLA_V)
                + chunks_of(gz_scr, W_GZ, GLA_V))
    mem_cols = chunks_of(mq_scr, W_MQ, X_W) + chunks_of(mz_scr, W_MZ, X_W)
    merge_cols = chunks_of(mg_ref, W_MG, N_BRANCH * D_MODEL)

    i = lax.broadcasted_iota(jnp.int32, (WINDOW, 2 * WINDOW), 0)
    j = lax.broadcasted_iota(jnp.int32, (WINDOW, 2 * WINDOW), 1)
    diff = i + WINDOW - j
    band = (diff >= 0) & (diff <= WINDOW)
    swa = {}

    def swa_start(blk):
        prev = kvp_scr[...] if blk == 0 else kv_scr[(blk - 1) * WINDOW:blk * WINDOW, :]
        kvb = jnp.concatenate([prev, kv_scr[blk * WINDOW:(blk + 1) * WINDOW, :]], axis=0)
        swa["ops"] = _swa_operands(kvb[:, :SWA_KV], kvb[:, SWA_KV:])
        swa["valid"] = band & (j >= jnp.where(n > 0, 0, WINDOW)) if blk == 0 else band

    def swa_scores(blk, kh):
        swa["s"] = _swa_scores(sq_scr[blk * WINDOW:(blk + 1) * WINDOW, :], kh, swa["ops"][kh][0])
        swa["soft"] = []

    def swa_soft(kh, hb):
        swa["soft"].append(_swa_soft_head(swa["s"][hb * WINDOW:(hb + 1) * WINDOW], swa["valid"],
                                          sink_ref[_swa_head_of_block(kh, hb)]))

    def swa_out(blk, kh):
        _, v_even, v_odd = swa["ops"][kh]
        o = _swa_out([p for p, _ in swa["soft"]], [r for _, r in swa["soft"]], v_even, v_odd)
        rows = slice(blk * WINDOW, (blk + 1) * WINDOW)
        cols = slice(kh * SWA_G * SWA_HD, (kh + 1) * SWA_G * SWA_HD)
        oswa_ref[rows, cols] = (o * _silu(sz_scr[rows, cols].astype(F32))).astype(BF16)

    swa_steps = []
    for blk in range(tm // WINDOW):
        swa_steps.append((functools.partial(swa_start, blk), SWA_STEP_COST["start"]))
        for kh in range(SWA_KVH):
            swa_steps.append((functools.partial(swa_scores, blk, kh), SWA_STEP_COST["scores"]))
            swa_steps += [(functools.partial(swa_soft, kh, hb), SWA_STEP_COST["soft"])
                          for hb in range(SWA_G)]
            swa_steps.append((functools.partial(swa_out, blk, kh), SWA_STEP_COST["out"]))

    mem = {}

    def mem_scores(hd):
        cols = slice(hd * X_HD, (hd + 1) * X_HD)
        p, rinv = _mem_softmax(_dot_nt(mq_scr[:, cols], mk_ref[:, cols]) * (X_HD ** -0.5))
        mem["p"], mem["rinv"] = p.astype(BF16), rinv

    def mem_out(hd):
        cols = slice(hd * X_HD, (hd + 1) * X_HD)
        o = _dot(mem["p"], mv_ref[:, cols]) * mem["rinv"]
        omem_ref[:, cols] = (o * _silu(mz_scr[:, cols].astype(F32))).astype(BF16)

    mem_steps = []
    for hd in range(X_H):
        mem_steps += [(functools.partial(mem_scores, hd), MEM_STEP_COST["scores"]),
                      (functools.partial(mem_out, hd), MEM_STEP_COST["out"])]

    gla_steps = _gla_thunks(gq_scr, gk_scr, gv_scr, gz_scr, ga_scr, wa_ref, ba_ref, gout_ref, None,
                            ogla_ref, s_ref, chunk=FRONT_GLA_CHUNK, nchunks=tm // FRONT_GLA_CHUNK,
                            carry=True, anchored=True)

    for chunk in swa_cols:
        project(chunk)
    chunks = gla_cols + mem_cols + merge_cols
    steps = [(t, w, 0) for t, w in swa_steps]
    steps += [(t, w, len(gla_cols)) for t, w in gla_steps]
    steps += [(t, w, len(gla_cols) + len(mem_cols)) for t, w in mem_steps]
    _interleave(steps, chunks, project)

    kv_tail = kv_scr[tm - WINDOW:, :]
    kvp_scr[...] = kv_tail
    kvl_ref[...] = kv_tail.T


def _front(x2d, g, w, sink, wa, ba, gout, mk, mv, prev_state, layer, depth, batch, seq):
    t, d = x2d.shape
    tm = FRONT_ROWS
    nblk = seq // tm
    row = lambda width: pl.BlockSpec((tm, width), lambda b, n: (b * nblk + n, 0))
    const = lambda shape: pl.BlockSpec(shape, lambda b, n: (0,) * len(shape))
    mem_spec = pl.BlockSpec((MEM_LEN, X_W), lambda b, n: (b, 0))
    state_spec = pl.BlockSpec((None, None, GLA_H, GLA_DK, GLA_DV), lambda b, n: (layer, b, 0, 0, 0))
    in_specs = [pl.BlockSpec(memory_space=pltpu.SMEM), row(d), const((1, d)),
                _layer_weight_spec(w, layer, resident=True),
                _layer_weight_spec(wa, layer), const((1, GLA_QK)), const((1, GLA_DV)),
                mem_spec, mem_spec]
    args = [sink, x2d, g, w, wa, ba, gout, mk, mv]
    aliases = {}
    if prev_state is not None:
        aliases = {len(args): 5}
        in_specs.append(pl.BlockSpec(memory_space=pl.ANY))
        args.append(prev_state)
    scratch = [pltpu.VMEM((tm, width), BF16) for width in
               (GLA_QK, GLA_QK, GLA_V, GLA_V, LANES, SWA_W, SWA_W, X_W, X_W)]
    scratch += [pltpu.VMEM((tm, KV_COLS), F32), pltpu.VMEM((WINDOW, KV_COLS), F32)]
    return pl.pallas_call(
        _front_kernel,
        grid=(batch, nblk),
        in_specs=in_specs,
        out_specs=[row(GLA_V), row(SWA_W), row(X_W), row(N_BRANCH * D_MODEL),
                   pl.BlockSpec((None, KV_COLS, WINDOW), lambda b, n: (b, 0, 0)), state_spec],
        out_shape=[jax.ShapeDtypeStruct((t, GLA_V), BF16), jax.ShapeDtypeStruct((t, SWA_W), BF16),
                   jax.ShapeDtypeStruct((t, X_W), BF16),
                   jax.ShapeDtypeStruct((t, N_BRANCH * D_MODEL), BF16),
                   jax.ShapeDtypeStruct((batch, KV_COLS, WINDOW), F32),
                   jax.ShapeDtypeStruct((depth, batch, GLA_H, GLA_DK, GLA_DV), F32)],
        scratch_shapes=scratch,
        input_output_aliases=aliases,
        compiler_params=_params("parallel", "arbitrary"),
        name="front",
    )(*args)


def _merge_kernel(x_ref, a_ref, s_ref, m_ref, ga_ref, gs_ref, gm_ref, wa_ref, ws_ref, wm_ref,
                  wo_ref, *rest, final):
    if final:
        gf_ref, y_ref = rest
    else:
        (y_ref,) = rest
    merged = (_sigmoid(ga_ref[...].astype(F32)) * _dot(a_ref[...], wa_ref[...])
              + _sigmoid(gs_ref[...].astype(F32)) * _dot(s_ref[...], ws_ref[...])
              + _sigmoid(gm_ref[...].astype(F32)) * _dot(m_ref[...], wm_ref[...]))
    y = x_ref[...] + _dot(merged.astype(BF16), wo_ref[...])
    if final:
        y = _rmsnorm_rows(y, gf_ref[...])
    y_ref[...] = y


def _merge(x2d, o_gla, o_swa, o_mem, gates, gate_col0, w_gla, w_swa, w_mem, w_out, g_final, layer):
    t, d = x2d.shape
    tm = min(t, 2 * _row_tile(t))
    final = g_final is not None
    row = lambda width: pl.BlockSpec((tm, width), lambda i: (i, 0))
    gate = lambda k: pl.BlockSpec((tm, d), lambda i: (i, gate_col0 // d + k))
    in_specs = [row(d), row(d), row(d), row(d), gate(0), gate(1), gate(2)]
    in_specs += [_layer_weight_spec(w, layer, resident=True) for w in (w_gla, w_swa, w_mem, w_out)]
    args = [x2d, o_gla, o_swa, o_mem, gates, gates, gates, w_gla, w_swa, w_mem, w_out]
    if final:
        in_specs.append(pl.BlockSpec((1, d), lambda i: (0, 0)))
        args.append(g_final)
    return pl.pallas_call(
        functools.partial(_merge_kernel, final=final),
        grid=(t // tm,),
        in_specs=in_specs,
        out_specs=row(d),
        out_shape=jax.ShapeDtypeStruct((t, d), F32),
        compiler_params=_params("parallel"),
        name="merge_final" if final else "merge",
    )(*args)


def kernel(x_prompt, x_sample, mem_prompt, state_gla, cache_swa_k, cache_swa_v, cache_mem_k,
           cache_mem_v, g_norm, w_in, w_gla_a, b_gla_a, g_gla_out, swa_sink, g_mem, w_mem_kv,
           w_br_gla, w_br_swa, w_br_mem, w_out, g_final):
    depth = w_in.shape[0]
    batch, seq, d = x_prompt.shape
    dec_batch, dec_seq, _ = x_sample.shape
    wb = cache_swa_k.shape[2]

    w_in_r = w_in.transpose(0, 2, 1).astype(BF16)
    w_a = jnp.pad(w_gla_a, ((0, 0), (0, LANES - GLA_RANK), (0, 0))).astype(BF16)
    w_mem_b = w_mem_kv.astype(BF16)
    w_g, w_s, w_m, w_o = (w.astype(BF16) for w in (w_br_gla, w_br_swa, w_br_mem, w_out))
    g_fin = g_final.reshape(1, d)

    yp = x_prompt.reshape(batch * seq, d)
    ys = x_sample.reshape(dec_batch * dec_seq, d)
    mem2d = mem_prompt.reshape(batch * MEM_LEN, d)
    swa_t = lambda c: c.transpose(0, 1, 3, 4, 2).reshape(depth, dec_batch, SWA_KV, wb)
    ck_all, cv_all = swa_t(cache_swa_k), swa_t(cache_swa_v)
    cmk_all = _mem_rows_view(cache_mem_k)
    cmv_all = _mem_rows_view(cache_mem_v)
    gla_p = sample_new = mem_new = None
    kp_l, vp_l = [], []
    for l in range(depth):
        last = l == depth - 1
        gn = g_norm[l].reshape(1, d)
        ba = b_gla_a[l].reshape(1, GLA_QK)
        gout = g_gla_out[l].reshape(1, GLA_DV)
        sink = swa_sink[l]
        branch_w = (w_g, w_s, w_m, w_o, g_fin if last else None, l)

        mk, mv, *mem_new = _mem_kv(mem2d, g_mem[l].reshape(1, d), w_mem_b, mem_new, l, depth,
                                   batch)
        o_gla, o_swa, o_mem, gates, kv_tail, gla_p = _front(
            yp, gn, w_in_r, sink, w_a, ba, gout, mk, mv, gla_p, l, depth, batch, seq)
        yp = _merge(yp, o_gla, o_swa, o_mem, gates, 0, *branch_w)
        kv_heads = kv_tail.reshape(batch, 2, SWA_KVH, SWA_HD, WINDOW).transpose(0, 1, 4, 2, 3)
        kp_l.append(kv_heads[:, 0])
        vp_l.append(kv_heads[:, 1])

        z, kv = _in_proj(ys, gn, w_in_r, l)
        o_gla, o_swa, o_mem, *sample_new = _sample_mixers(
            z, kv, sink, w_a, ba, gout, state_gla, ck_all, cv_all, cmk_all, cmv_all, sample_new, l,
            dec_batch, dec_seq)
        ys = _merge(ys, o_gla, o_swa, o_mem, z, Z_MG, *branch_w)

    gla_s = sample_new[0]
    k_s, v_s = (c.reshape(depth, dec_batch, SWA_KVH, SWA_HD, wb).transpose(0, 1, 4, 2, 3)
                for c in sample_new[1:])
    mk_p, mv_p = (_mem_from_rows_view(c) for c in mem_new)
    return (yp.reshape(batch, seq, d), ys.reshape(dec_batch, dec_seq, d), gla_p,
            jnp.stack(kp_l), jnp.stack(vp_l), mk_p, mv_p, gla_s, k_s, v_s)
```

```python
import functools

import jax
import jax.numpy as jnp
from jax import lax
from jax.experimental import pallas as pl
from jax.experimental.pallas import tpu as pltpu

F32 = jnp.float32
BF16 = jnp.bfloat16

D_MODEL = 1024
GLA_H = 4
GLA_DK = 128
GLA_DV = 256
GLA_QK = GLA_H * GLA_DK
GLA_V = GLA_H * GLA_DV
GLA_RANK = 16
GLA_TAU = 16.0
GLA_CHUNK = 32
SWA_HD = 64
SWA_H = 16
SWA_KVH = 2
SWA_G = SWA_H // SWA_KVH
SWA_W = SWA_H * SWA_HD
SWA_KV = SWA_KVH * SWA_HD
WINDOW = 128
MEM_LEN = 256
X_H = 4
X_HD = 256
X_W = X_H * X_HD
N_BRANCH = 3
IN_SPLITS = (GLA_QK, GLA_QK, GLA_V, GLA_V, GLA_RANK, SWA_W, SWA_KV, SWA_KV, SWA_W, X_W, X_W,
             N_BRANCH * D_MODEL)
EPS = 1e-6

LANES = 128
BF16_ROWS = 16
MXU_COLS = 256
IN_PROJ_CHUNK = 2 * MXU_COLS
VMEM_LIMIT = 56 * 1024 * 1024

Z_GQ = 0
Z_GK = Z_GQ + GLA_QK
Z_GV = Z_GK + GLA_QK
Z_GZ = Z_GV + GLA_V
Z_SQ = Z_GZ + GLA_V
Z_SZ = Z_SQ + SWA_W
Z_MQ = Z_SZ + SWA_W
Z_MZ = Z_MQ + X_W
Z_MG = Z_MZ + X_W
Z_GA = Z_MG + N_BRANCH * D_MODEL
Z_COLS = Z_GA + LANES
KV_COLS = 2 * SWA_KV

(W_GQ, W_GK, W_GV, W_GZ, W_GA, W_SQ, W_SK, W_SV, W_SZ, W_MQ, W_MZ, W_MG) = (
    sum(IN_SPLITS[:i]) for i in range(len(IN_SPLITS)))
Z_PIECES = ((Z_GQ, W_GQ, GLA_QK), (Z_GK, W_GK, GLA_QK), (Z_GV, W_GV, GLA_V), (Z_GZ, W_GZ, GLA_V),
            (Z_SQ, W_SQ, SWA_W), (Z_SZ, W_SZ, SWA_W), (Z_MQ, W_MQ, X_W), (Z_MZ, W_MZ, X_W),
            (Z_MG, W_MG, N_BRANCH * D_MODEL))
W_PREP_CHUNK = MXU_COLS
W_PREP_DECAY = sum(IN_SPLITS) - GLA_RANK
W_PREP_COLS = W_PREP_DECAY + W_PREP_CHUNK


def _prepared_col(col):
    return col if col < W_GA else col - GLA_RANK


def _project(h, w_ref, col0, width):
    c = _prepared_col(col0)
    return _dot(h, w_ref[:, c:c + width])


def _project_decay(h, w_ref):
    return _dot(h, w_ref[:, W_PREP_DECAY:W_PREP_DECAY + LANES])


def _row_tile(rows):
    return 512 if rows >= 4096 else 256


def _params(*semantics):
    return pltpu.CompilerParams(dimension_semantics=semantics, vmem_limit_bytes=VMEM_LIMIT)


def _sigmoid(x):
    return 0.5 * jnp.tanh(0.5 * x) + 0.5


def _silu(x):
    return x * _sigmoid(x)


def _log_sigmoid(x):
    return jnp.minimum(x, 0.0) - jnp.log(1.0 + jnp.exp(-jnp.abs(x)))


def _rmsnorm_rows(x, g):
    return x * lax.rsqrt(jnp.mean(x * x, axis=-1, keepdims=True) + EPS) * g


def _dot(a, b):
    return jnp.dot(a, b, preferred_element_type=F32)


def _dot_nt(a, b):
    return lax.dot_general(a, b, (((1,), (1,)), ((), ())), preferred_element_type=F32)


def _dot_tn(a, b):
    return lax.dot_general(a, b, (((0,), (0,)), ((), ())), preferred_element_type=F32)


def _pad_rows(x, rows):
    if x.shape[0] >= rows:
        return x
    return jnp.concatenate([x, jnp.zeros((rows - x.shape[0], x.shape[1]), x.dtype)], axis=0)


def _interleave(steps, chunks, run_chunk):
    total = sum(w for _, w, _ in steps)
    done, acc = 0, 0
    for thunk, weight, needs in steps:
        acc += weight
        upto = max(needs, min(len(chunks), -(-acc * len(chunks) // total)))
        for chunk in chunks[done:upto]:
            run_chunk(chunk)
        done = max(done, upto)
        thunk()


def _prep_w_in_kernel(wt_ref, o_ref):
    w = wt_ref[0].T
    lane = lax.broadcasted_iota(jnp.int32, w.shape, 1)
    is_decay = pl.program_id(1) == pl.num_programs(1) - 1
    o_ref[...] = jnp.where(is_decay & (lane >= GLA_RANK), 0.0, w).astype(BF16)


def _prep_w_in(w_in):
    depth, d, _ = w_in.shape
    w_t = w_in.transpose(0, 2, 1)
    nchunks = W_PREP_COLS // W_PREP_CHUNK

    def source_row(c):
        col = c * W_PREP_CHUNK
        row = jnp.where(col < W_GA, col, col + GLA_RANK)
        return pl.multiple_of(jnp.where(c == nchunks - 1, W_GA, row), BF16_ROWS)

    return pl.pallas_call(
        _prep_w_in_kernel,
        grid=(depth, nchunks),
        in_specs=[pl.BlockSpec((pl.Element(1), pl.Element(W_PREP_CHUNK), pl.Element(d)),
                               lambda l, c: (l, source_row(c), 0))],
        out_specs=pl.BlockSpec((None, d, W_PREP_CHUNK), lambda l, c: (l, 0, c)),
        out_shape=jax.ShapeDtypeStruct((depth, d, W_PREP_COLS), BF16),
        compiler_params=_params("parallel", "parallel"),
        name="prep_w_in",
    )(w_t)


def _in_proj_kernel(x_ref, g_ref, w_ref, z_ref, kv_ref):
    h = _rmsnorm_rows(x_ref[...], g_ref[...]).astype(BF16)
    for z0, w0, width in Z_PIECES:
        for c in range(0, width, IN_PROJ_CHUNK):
            z_ref[:, z0 + c:z0 + c + IN_PROJ_CHUNK] = _project(h, w_ref, w0 + c,
                                                               IN_PROJ_CHUNK).astype(BF16)
    z_ref[:, Z_GA:Z_COLS] = _project_decay(h, w_ref).astype(BF16)
    kv_ref[...] = _project(h, w_ref, W_SK, KV_COLS)


def _layer_weight_spec(w, layer, resident=False):
    mode = dict(pipeline_mode=pl.Buffered(1)) if resident else {}
    return pl.BlockSpec((None,) + w.shape[1:], lambda *ids: (layer, 0, 0), **mode)


def _in_proj(x2d, g, w, layer):
    t, d = x2d.shape
    tm = _row_tile(t)
    return pl.pallas_call(
        _in_proj_kernel,
        grid=(t // tm,),
        in_specs=[
            pl.BlockSpec((tm, d), lambda i: (i, 0)),
            pl.BlockSpec((1, d), lambda i: (0, 0)),
            _layer_weight_spec(w, layer, resident=True),
        ],
        out_specs=[pl.BlockSpec((tm, Z_COLS), lambda i: (i, 0)),
                   pl.BlockSpec((tm, KV_COLS), lambda i: (i, 0))],
        out_shape=[jax.ShapeDtypeStruct((t, Z_COLS), BF16), jax.ShapeDtypeStruct((t, KV_COLS), F32)],
        compiler_params=_params("parallel"),
        name="in_proj",
    )(x2d, g, w)


GLA_STEP_COST = dict(prepare=100, decay=230, factors=170, intra=60, step=130, epilogue=190)
SWA_STEP_COST = dict(start=50, scores=30, soft=40, out=75)
MEM_STEP_COST = dict(scores=130, out=160)


def _chunk_cumsum(g, chunk):
    pos = lax.broadcasted_iota(jnp.int32, g.shape, 0) & (chunk - 1)
    b = g
    shift = 1
    while shift < chunk:
        b = b + jnp.where(pos >= shift, pltpu.roll(b, shift, 0), 0.0)
        shift *= 2
    return b


def _gla_thunks(q_ref, k_ref, v_ref, gz_ref, ga_ref, wa_ref, ba_ref, gout_ref, s0_ref, o_ref,
                s_ref, *, chunk, nchunks, carry, anchored):
    rows = chunk * nchunks
    mm_rows = max(chunk, BF16_ROWS)
    heads = range(GLA_H)
    lanes = [slice(h * GLA_DK, (h + 1) * GLA_DK) for h in heads]
    vl = [slice(h * GLA_DV, (h + 1) * GLA_DV) for h in heads]
    per_head = lambda: [None] * GLA_H
    st = {name: per_head() for name in ("b", "qb", "kd", "qb_bf", "kd_bf", "qa_bf", "ka_bf",
                                        "dec_cols", "o_intra")}
    st["o_inter"] = [[] for _ in heads]

    def chunk_row(x, row):
        picked = [x[j * chunk + row:j * chunk + row + 1, :] for j in range(nchunks)]
        return picked, jnp.concatenate([jnp.broadcast_to(r, (chunk, x.shape[1])) for r in picked],
                                       axis=0)

    small = chunk < BF16_ROWS

    def chunk_rows(x_f32, x_bf, sl):
        return _pad_rows(x_f32[sl], mm_rows).astype(BF16) if small else x_bf[sl]

    def prepare():
        r_idx = lax.broadcasted_iota(jnp.int32, (rows, rows), 0)
        c_idx = lax.broadcasted_iota(jnp.int32, (rows, rows), 1)
        diff = r_idx - c_idx
        st["mask"] = (diff >= 0) & (diff <= (r_idx & (chunk - 1)))
        st["v_f32"] = v_ref[...].astype(F32) if small else None
        if carry:
            st["states"] = [s_ref[h] for h in heads]

    def decay(h):
        x = _dot(ga_ref[...], wa_ref[:, lanes[h]]) + ba_ref[:, lanes[h]]
        st["b"][h] = _chunk_cumsum(_log_sigmoid(x) * (1.0 / GLA_TAU), chunk)

    def factors(h):
        b = st["b"][h]
        b_last, b_last_full = chunk_row(b, chunk - 1)
        b_last_rows = jnp.concatenate(b_last + [jnp.zeros((LANES - nchunks, GLA_DK), F32)], axis=0)
        st["dec_cols"][h] = jnp.exp(b_last_rows.T)
        q = q_ref[:, lanes[h]].astype(F32) * (GLA_DK ** -0.5)
        k = k_ref[:, lanes[h]].astype(F32)
        qb = q * jnp.exp(b)
        kd = k * jnp.exp(b_last_full - b)
        st["qb"][h], st["kd"][h] = (qb, kd) if small else (None, None)
        st["qb_bf"][h], st["kd_bf"][h] = qb.astype(BF16), kd.astype(BF16)
        if anchored:
            _, b_mid_full = chunk_row(b, chunk // 2 - 1)
            st["qa_bf"][h] = (q * jnp.exp(b - b_mid_full)).astype(BF16)
            st["ka_bf"][h] = (k * jnp.exp(b_mid_full - b)).astype(BF16)
        else:
            st["qa_bf"][h] = st["qb_bf"][h]
            st["ka_bf"][h] = (k * jnp.exp(-b)).astype(BF16)

    def intra(h):
        a = jnp.where(st["mask"], _dot_nt(st["qa_bf"][h], st["ka_bf"][h]), 0.0)
        st["o_intra"][h] = _dot(a.astype(BF16), v_ref[:, vl[h]])

    def step(j):
        sl = slice(j * chunk, (j + 1) * chunk)
        for h in heads:
            state = st["states"][h] if carry else s0_ref[j, h]
            qb_j = chunk_rows(st["qb"][h], st["qb_bf"][h], sl)
            st["o_inter"][h].append(_dot(qb_j, state.astype(BF16))[:chunk])
            if small:
                v_j = _pad_rows(st["v_f32"][sl, vl[h]], mm_rows).astype(BF16)
            else:
                v_j = v_ref[sl, vl[h]]
            kd_j = chunk_rows(st["kd"][h], st["kd_bf"][h], sl)
            new_state = st["dec_cols"][h][:, j:j + 1] * state + _dot_tn(kd_j, v_j)
            if carry:
                st["states"][h] = new_state
            else:
                s_ref[j, h] = new_state

    def epilogue(h):
        if carry:
            s_ref[h] = st["states"][h]
        o = st["o_intra"][h] + jnp.concatenate(st["o_inter"][h], axis=0)
        gate = _silu(gz_ref[:, vl[h]].astype(F32))
        o_ref[:, vl[h]] = (_rmsnorm_rows(o, gout_ref[...]) * gate).astype(BF16)

    cost = GLA_STEP_COST
    steps = [(prepare, cost["prepare"])]
    steps += [(functools.partial(decay, h), cost["decay"]) for h in heads]
    steps += [(functools.partial(factors, h), cost["factors"]) for h in heads]
    steps += [(functools.partial(intra, h), cost["intra"]) for h in heads]
    steps += [(functools.partial(step, j), cost["step"]) for j in range(nchunks)]
    steps += [(functools.partial(epilogue, h), cost["epilogue"]) for h in heads]
    return steps


def _gla_sample_kernel(q_ref, k_ref, v_ref, gz_ref, ga_ref, wa_ref, ba_ref, gout_ref, s0_ref, *rest,
                       seq, group):
    o_ref, s_ref = rest[-2:]
    for thunk, _ in _gla_thunks(q_ref, k_ref, v_ref, gz_ref, ga_ref, wa_ref, ba_ref, gout_ref,
                                s0_ref, o_ref, s_ref, chunk=seq, nchunks=group, carry=False,
                                anchored=False):
        thunk()


def _swa_head_of_block(kv_head, blk):
    pairs = SWA_G // 2
    within = 2 * blk if blk < pairs else 2 * (blk - pairs) + 1
    return kv_head * SWA_G + within


def _swa_operands(kk, vv):
    lo = lax.broadcasted_iota(jnp.int32, kk.shape, 1) < SWA_HD
    kk_sw = pltpu.roll(kk, SWA_HD, 1)
    vv_sw = pltpu.roll(vv, SWA_HD, 1)
    ops = []
    for kh in range(SWA_KVH):
        k_this, k_other = (kk, kk_sw) if kh == 0 else (kk_sw, kk)
        v_this, v_other = (vv, vv_sw) if kh == 0 else (vv_sw, vv)
        ops.append((jnp.where(lo, k_this, k_other).astype(BF16),
                    jnp.where(lo, v_this, 0.0).astype(BF16),
                    jnp.where(lo, 0.0, v_other).astype(BF16)))
    return ops


def _swa_operands_t(kk_t, vv_t):
    zeros = jnp.zeros((SWA_HD, kk_t.shape[1]), F32)
    ops = []
    for kh in range(SWA_KVH):
        k_h = kk_t[kh * SWA_HD:(kh + 1) * SWA_HD]
        v_h = vv_t[kh * SWA_HD:(kh + 1) * SWA_HD]
        ops.append((jnp.concatenate([k_h, k_h], axis=0).astype(BF16),
                    jnp.concatenate([v_h, zeros], axis=0).astype(BF16),
                    jnp.concatenate([zeros, v_h], axis=0).astype(BF16)))
    return ops


def _swa_scores(q, kh, k_dup, keys_on_lanes=False):
    nq = q.shape[0]
    lo = lax.broadcasted_iota(jnp.int32, (nq, LANES), 1) < SWA_HD
    even, odd = [], []
    for p in range(SWA_G // 2):
        c0 = kh * SWA_G * SWA_HD + p * LANES
        q_pair = q[:, c0:c0 + LANES]
        even.append(jnp.where(lo, q_pair, jnp.zeros_like(q_pair)))
        odd.append(jnp.where(lo, jnp.zeros_like(q_pair), q_pair))
    lhs = jnp.concatenate(even + odd, axis=0).astype(BF16)
    s = _dot(lhs, k_dup) if keys_on_lanes else _dot_nt(lhs, k_dup)
    return s * (SWA_HD ** -0.5)


def _swa_soft_head(s_h, valid, sink):
    s_h = jnp.where(valid, s_h, -jnp.inf)
    m = jnp.maximum(jnp.max(s_h, axis=-1, keepdims=True), sink)
    p_h = jnp.exp(s_h - m)
    den = jnp.sum(p_h, axis=-1, keepdims=True) + jnp.exp(sink - m)
    if p_h.shape[0] % BF16_ROWS == 0:
        p_h = p_h.astype(BF16)
    return p_h, 1.0 / den


def _swa_out(probs, rinv, v_even, v_odd):
    pairs = SWA_G // 2
    nq = probs[0].shape[0]
    lo = lax.broadcasted_iota(jnp.int32, (nq, LANES), 1) < SWA_HD
    p_even = jnp.concatenate(probs[:pairs], axis=0).astype(BF16)
    p_odd = jnp.concatenate(probs[pairs:], axis=0).astype(BF16)
    o = _dot(p_even, v_even) + _dot(p_odd, v_odd)
    outs = [o[p * nq:(p + 1) * nq] * jnp.where(lo, rinv[p], rinv[pairs + p]) for p in range(pairs)]
    return jnp.concatenate(outs, axis=1)


def _swa_sample_kernel(sink_ref, q_ref, sz_ref, kvn_ref, ck_ref, cv_ref, *rest, group, seq):
    o_ref, nk_ref, nv_ref = rest[-3:]
    wb = ck_ref.shape[2]
    nk = 2 * wb
    pairs = SWA_G // 2
    blk_rows = SWA_G * seq
    first = lax.broadcasted_iota(jnp.int32, (SWA_KV, wb), 1) < seq
    q = q_ref[...].astype(F32)
    kvn = _pad_rows(kvn_ref[...], LANES)
    kn_all, vn_all = kvn[:, :SWA_KV].T, kvn[:, SWA_KV:].T

    scores, values = [], []
    for e in range(group):
        rows = slice(e * seq, (e + 1) * seq)
        k_old, v_old = ck_ref[e], cv_ref[e]
        shift = (LANES - e * seq) % LANES
        k_new = pltpu.roll(kn_all, shift, 1) if shift else kn_all
        v_new = pltpu.roll(vn_all, shift, 1) if shift else vn_all
        nk_ref[e] = pltpu.roll(jnp.where(first, k_new, k_old), wb - seq, 1)
        nv_ref[e] = pltpu.roll(jnp.where(first, v_new, v_old), wb - seq, 1)
        ops = _swa_operands_t(jnp.concatenate([k_old, k_new], axis=1),
                              jnp.concatenate([v_old, v_new], axis=1))
        for kh in range(SWA_KVH):
            k_dup, v_even, v_odd = ops[kh]
            scores.append(_swa_scores(q[rows], kh, k_dup, keys_on_lanes=True))
            values.append((v_even, v_odd))
    s_all = jnp.concatenate(scores, axis=0)
    i = lax.broadcasted_iota(jnp.int32, s_all.shape, 0) & (seq - 1)
    j = lax.broadcasted_iota(jnp.int32, s_all.shape, 1)
    diff = i + wb - j
    sink_kh = [jnp.concatenate([jnp.full((seq, 1), sink_ref[_swa_head_of_block(kh, blk)], F32)
                                for blk in range(SWA_G)], axis=0) for kh in range(SWA_KVH)]
    p_all, rinv_all = _swa_soft_head(s_all, (diff >= 0) & (diff <= WINDOW),
                                     jnp.concatenate(sink_kh * group, axis=0))

    lo = lax.broadcasted_iota(jnp.int32, (seq, LANES), 1) < SWA_HD
    outs = [[] for _ in range(group)]
    for idx, (v_even, v_odd) in enumerate(values):
        base = idx * blk_rows
        half = pairs * seq
        o = (_dot_nt(p_all[base:base + half], v_even)
             + _dot_nt(p_all[base + half:base + blk_rows], v_odd))
        for p in range(pairs):
            r_even = rinv_all[base + p * seq:base + (p + 1) * seq]
            r_odd = rinv_all[base + half + p * seq:base + half + (p + 1) * seq]
            outs[idx // SWA_KVH].append(o[p * seq:(p + 1) * seq] * jnp.where(lo, r_even, r_odd))
    o = jnp.concatenate([jnp.concatenate(row, axis=1) for row in outs], axis=0)
    o_ref[...] = (o * _silu(sz_ref[...].astype(F32))).astype(BF16)


MEM_HALVES = X_HD // LANES
MEM_TILE_ROWS = X_H * MEM_HALVES


def _mem_rows_view(x):
    lead = x.shape[:-3]
    n = len(lead)
    x = x.reshape(*lead, MEM_LEN, X_H, MEM_HALVES, LANES)
    x = x.transpose(*range(n), n, n + 2, n + 1, n + 3)
    return x.reshape(*lead, MEM_LEN * MEM_TILE_ROWS, LANES)


def _mem_from_rows_view(x):
    lead = x.shape[:-2]
    n = len(lead)
    x = x.reshape(*lead, MEM_LEN, MEM_HALVES, X_H, LANES)
    x = x.transpose(*range(n), n, n + 2, n + 1, n + 3)
    return x.reshape(*lead, MEM_LEN, X_H, X_HD)


def _mem_head_rows(head, half):
    return pl.ds(half * X_H + head, MEM_LEN, stride=MEM_TILE_ROWS)


def _mem_kv_kernel(x_ref, g_ref, w_ref, *rest):
    kb_ref, vb_ref, kn_ref, vn_ref = rest[-4:]
    h = _rmsnorm_rows(x_ref[...], g_ref[...]).astype(BF16)
    for t, (b_ref, n_ref) in enumerate(((kb_ref, kn_ref), (vb_ref, vn_ref))):
        for head in range(X_H):
            c0 = head * X_HD
            res = _dot(h, w_ref[:, t * X_W + c0:t * X_W + c0 + X_HD])
            b_ref[:, c0:c0 + X_HD] = res.astype(BF16)
            for half in range(MEM_HALVES):
                n_ref[_mem_head_rows(head, half), :] = res[:, half * LANES:(half + 1) * LANES]


def _mem_kv(mem2d, g, w, prev, layer, depth, batch):
    d = mem2d.shape[1]
    rows_spec = pl.BlockSpec((None, None, MEM_LEN * MEM_TILE_ROWS, LANES), lambda i: (layer, i, 0, 0))
    rows_shape = jax.ShapeDtypeStruct((depth, batch, MEM_LEN * MEM_TILE_ROWS, LANES), F32)
    dense_spec = pl.BlockSpec((MEM_LEN, X_W), lambda i: (i, 0))
    dense_shape = jax.ShapeDtypeStruct((batch * MEM_LEN, X_W), BF16)
    in_specs = [pl.BlockSpec((MEM_LEN, d), lambda i: (i, 0)),
                pl.BlockSpec((1, d), lambda i: (0, 0)),
                _layer_weight_spec(w, layer)]
    args = [mem2d, g, w]
    aliases = {}
    if prev is not None:
        aliases = {len(args): 2, len(args) + 1: 3}
        in_specs += [pl.BlockSpec(memory_space=pl.ANY)] * 2
        args += list(prev)
    return pl.pallas_call(
        _mem_kv_kernel,
        grid=(batch,),
        in_specs=in_specs,
        out_specs=[dense_spec, dense_spec, rows_spec, rows_spec],
        out_shape=[dense_shape, dense_shape, rows_shape, rows_shape],
        input_output_aliases=aliases,
        compiler_params=_params("parallel"),
        name="mem_kv",
    )(*args)


def _mem_softmax(s):
    p = jnp.exp(s - jnp.max(s, axis=-1, keepdims=True))
    return p, 1.0 / jnp.sum(p, axis=-1, keepdims=True)


def _mem_sample_kernel(q_ref, mz_ref, mk_ref, mv_ref, o_ref, *, group, seq):
    def head_tile(ref, e, h):
        halves = [ref[e, _mem_head_rows(h, half), :] for half in range(MEM_HALVES)]
        return jnp.concatenate(halves, axis=1).astype(BF16)

    q = q_ref[...].astype(F32)
    pairs = [(e, h) for e in range(group) for h in range(X_H)]
    scores = []
    for e, h in pairs:
        q_eh = _pad_rows(q[e * seq:(e + 1) * seq, h * X_HD:(h + 1) * X_HD], BF16_ROWS).astype(BF16)
        scores.append(_dot_nt(q_eh, head_tile(mk_ref, e, h))[:seq])
    p, rinv = _mem_softmax(jnp.concatenate(scores, axis=0) * (X_HD ** -0.5))
    outs = [[] for _ in range(group)]
    for i, (e, h) in enumerate(pairs):
        p_eh = _pad_rows(p[i * seq:(i + 1) * seq], BF16_ROWS).astype(BF16)
        outs[e].append(_dot(p_eh, head_tile(mv_ref, e, h))[:seq] * rinv[i * seq:(i + 1) * seq])
    o = jnp.concatenate([jnp.concatenate(row, axis=1) for row in outs], axis=0)
    o_ref[...] = (o * _silu(mz_ref[...].astype(F32))).astype(BF16)


SAMPLE_GROUP = 4


def _sample_mixers_kernel(sink_ref, gq_ref, gk_ref, gv_ref, gz_ref, ga_ref, sq_ref, sz_ref, mq_ref,
                          mz_ref, kvn_ref, wa_ref, ba_ref, gout_ref, s0_ref, ck_ref, cv_ref, mk_ref,
                          mv_ref, *rest, seq):
    ogla_ref, oswa_ref, omem_ref, s_ref, nk_ref, nv_ref = rest[-6:]
    _mem_sample_kernel(mq_ref, mz_ref, mk_ref, mv_ref, omem_ref, group=SAMPLE_GROUP, seq=seq)
    _swa_sample_kernel(sink_ref, sq_ref, sz_ref, kvn_ref, ck_ref, cv_ref, oswa_ref, nk_ref, nv_ref,
                       group=SAMPLE_GROUP, seq=seq)
    _gla_sample_kernel(gq_ref, gk_ref, gv_ref, gz_ref, ga_ref, wa_ref, ba_ref, gout_ref, s0_ref,
                       ogla_ref, s_ref, seq=seq, group=SAMPLE_GROUP)


def _sample_mixers(z, kv, sink, wa, ba, gout, s0, ck, cv, mk, mv, prev, layer, batch, seq):
    group = SAMPLE_GROUP
    rows = group * seq
    wb = ck.shape[3]
    assert wb == LANES, "the sliding window is one lane tile wide"
    zspec = lambda width, col0: pl.BlockSpec((rows, width), lambda i: (i, col0 // width))
    const = lambda shape: pl.BlockSpec(shape, lambda i: (0,) * len(shape))
    layered = lambda *dims: pl.BlockSpec((None, group) + dims, lambda i: (layer, i) + (0,) * len(dims))
    state_spec = layered(GLA_H, GLA_DK, GLA_DV)
    cache_spec = layered(SWA_KV, wb)
    mem_spec = layered(MEM_LEN * MEM_TILE_ROWS, LANES)
    in_specs = [pl.BlockSpec(memory_space=pltpu.SMEM),
                zspec(GLA_QK, Z_GQ), zspec(GLA_QK, Z_GK), zspec(GLA_V, Z_GV), zspec(GLA_V, Z_GZ),
                zspec(LANES, Z_GA), zspec(SWA_W, Z_SQ), zspec(SWA_W, Z_SZ), zspec(X_W, Z_MQ),
                zspec(X_W, Z_MZ), pl.BlockSpec((rows, KV_COLS), lambda i: (i, 0)),
                _layer_weight_spec(wa, layer), const((1, GLA_QK)), const((1, GLA_DV)),
                state_spec, cache_spec, cache_spec, mem_spec, mem_spec]
    args = [sink] + [z] * 9 + [kv, wa, ba, gout, s0, ck, cv, mk, mv]
    aliases = {}
    if prev is not None:
        aliases = {len(args) + k: 3 + k for k in range(3)}
        in_specs += [pl.BlockSpec(memory_space=pl.ANY)] * 3
        args += list(prev)
    out_row = lambda width: pl.BlockSpec((rows, width), lambda i: (i, 0))
    return pl.pallas_call(
        functools.partial(_sample_mixers_kernel, seq=seq),
        grid=(batch // group,),
        in_specs=in_specs,
        out_specs=[out_row(GLA_V), out_row(SWA_W), out_row(X_W), state_spec, cache_spec, cache_spec],
        out_shape=[jax.ShapeDtypeStruct((batch * seq, GLA_V), BF16),
                   jax.ShapeDtypeStruct((batch * seq, SWA_W), BF16),
                   jax.ShapeDtypeStruct((batch * seq, X_W), BF16),
                   jax.ShapeDtypeStruct(s0.shape, F32), jax.ShapeDtypeStruct(ck.shape, F32),
                   jax.ShapeDtypeStruct(cv.shape, F32)],
        input_output_aliases=aliases,
        compiler_params=_params("parallel"),
        name="sample_mixers",
    )(*args)


FRONT_ROWS = 256
FRONT_GLA_CHUNK = 2 * GLA_CHUNK

def _front_kernel(sink_ref, x_ref, g_ref, w_ref, wa_ref, ba_ref, gout_ref, mk_ref, mv_ref, *rest):
    (ogla_ref, oswa_ref, omem_ref, mg_ref, kvl_ref, s_ref,
     gq_scr, gk_scr, gv_scr, gz_scr, ga_scr, sq_scr, sz_scr, mq_scr, mz_scr, kv_scr, kvp_scr) = rest[-17:]
    n = pl.program_id(1)
    tm = FRONT_ROWS

    @pl.when(n == 0)
    def _():
        s_ref[...] = jnp.zeros_like(s_ref)
        kvp_scr[...] = jnp.zeros_like(kvp_scr)

    h = _rmsnorm_rows(x_ref[...], g_ref[...]).astype(BF16)

    def project(chunk):
        dst, dst_c0, w_r0, width = chunk
        if dst is ga_scr:
            dst[...] = _project_decay(h, w_ref).astype(dst.dtype)
        else:
            dst[:, dst_c0:dst_c0 + width] = _project(h, w_ref, w_r0, width).astype(dst.dtype)

    def chunks_of(dst, w_r0, width):
        return [(dst, c, w_r0 + c, min(MXU_COLS, width - c)) for c in range(0, width, MXU_COLS)]

    swa_cols = chunks_of(kv_scr, W_SK, KV_COLS) + chunks_of(sq_scr, W_SQ, SWA_W) \
        + chunks_of(sz_scr, W_SZ, SWA_W)
    gla_cols = (chunks_of(ga_scr, W_GA, LANES) + chunks_of(gq_scr, W_GQ, GLA_QK)
                + chunks_of(gk_scr, W_GK, GLA_QK) + chunks_of(gv_scr, W_GV, GLA_V)
                + chunks_of(gz_scr, W_GZ, GLA_V))
    mem_cols = chunks_of(mq_scr, W_MQ, X_W) + chunks_of(mz_scr, W_MZ, X_W)
    merge_cols = chunks_of(mg_ref, W_MG, N_BRANCH * D_MODEL)

    i = lax.broadcasted_iota(jnp.int32, (WINDOW, 2 * WINDOW), 0)
    j = lax.broadcasted_iota(jnp.int32, (WINDOW, 2 * WINDOW), 1)
    diff = i + WINDOW - j
    band = (diff >= 0) & (diff <= WINDOW)
    swa = {}

    def swa_start(blk):
        prev = kvp_scr[...] if blk == 0 else kv_scr[(blk - 1) * WINDOW:blk * WINDOW, :]
        kvb = jnp.concatenate([prev, kv_scr[blk * WINDOW:(blk + 1) * WINDOW, :]], axis=0)
        swa["ops"] = _swa_operands(kvb[:, :SWA_KV], kvb[:, SWA_KV:])
        swa["valid"] = band & (j >= jnp.where(n > 0, 0, WINDOW)) if blk == 0 else band

    def swa_scores(blk, kh):
        swa["s"] = _swa_scores(sq_scr[blk * WINDOW:(blk + 1) * WINDOW, :], kh, swa["ops"][kh][0])
        swa["soft"] = []

    def swa_soft(kh, hb):
        swa["soft"].append(_swa_soft_head(swa["s"][hb * WINDOW:(hb + 1) * WINDOW], swa["valid"],
                                          sink_ref[_swa_head_of_block(kh, hb)]))

    def swa_out(blk, kh):
        _, v_even, v_odd = swa["ops"][kh]
        o = _swa_out([p for p, _ in swa["soft"]], [r for _, r in swa["soft"]], v_even, v_odd)
        rows = slice(blk * WINDOW, (blk + 1) * WINDOW)
        cols = slice(kh * SWA_G * SWA_HD, (kh + 1) * SWA_G * SWA_HD)
        oswa_ref[rows, cols] = (o * _silu(sz_scr[rows, cols].astype(F32))).astype(BF16)

    swa_steps = []
    for blk in range(tm // WINDOW):
        swa_steps.append((functools.partial(swa_start, blk), SWA_STEP_COST["start"]))
        for kh in range(SWA_KVH):
            swa_steps.append((functools.partial(swa_scores, blk, kh), SWA_STEP_COST["scores"]))
            swa_steps += [(functools.partial(swa_soft, kh, hb), SWA_STEP_COST["soft"])
                          for hb in range(SWA_G)]
            swa_steps.append((functools.partial(swa_out, blk, kh), SWA_STEP_COST["out"]))

    mem = {}

    def mem_scores(hd):
        cols = slice(hd * X_HD, (hd + 1) * X_HD)
        p, rinv = _mem_softmax(_dot_nt(mq_scr[:, cols], mk_ref[:, cols]) * (X_HD ** -0.5))
        mem["p"], mem["rinv"] = p.astype(BF16), rinv

    def mem_out(hd):
        cols = slice(hd * X_HD, (hd + 1) * X_HD)
        o = _dot(mem["p"], mv_ref[:, cols]) * mem["rinv"]
        omem_ref[:, cols] = (o * _silu(mz_scr[:, cols].astype(F32))).astype(BF16)

    mem_steps = []
    for hd in range(X_H):
        mem_steps += [(functools.partial(mem_scores, hd), MEM_STEP_COST["scores"]),
                      (functools.partial(mem_out, hd), MEM_STEP_COST["out"])]

    gla_steps = _gla_thunks(gq_scr, gk_scr, gv_scr, gz_scr, ga_scr, wa_ref, ba_ref, gout_ref, None,
                            ogla_ref, s_ref, chunk=FRONT_GLA_CHUNK, nchunks=tm // FRONT_GLA_CHUNK,
                            carry=True, anchored=True)

    for chunk in swa_cols:
        project(chunk)
    chunks = gla_cols + mem_cols + merge_cols
    steps = [(t, w, 0) for t, w in swa_steps]
    steps += [(t, w, len(gla_cols)) for t, w in gla_steps]
    steps += [(t, w, len(gla_cols) + len(mem_cols)) for t, w in mem_steps]
    _interleave(steps, chunks, project)

    kv_tail = kv_scr[tm - WINDOW:, :]
    kvp_scr[...] = kv_tail
    kvl_ref[...] = kv_tail.T


def _front(x2d, g, w, sink, wa, ba, gout, mk, mv, prev_state, layer, depth, batch, seq):
    t, d = x2d.shape
    tm = FRONT_ROWS
    nblk = seq // tm
    row = lambda width: pl.BlockSpec((tm, width), lambda b, n: (b * nblk + n, 0))
    const = lambda shape: pl.BlockSpec(shape, lambda b, n: (0,) * len(shape))
    mem_spec = pl.BlockSpec((MEM_LEN, X_W), lambda b, n: (b, 0))
    state_spec = pl.BlockSpec((None, None, GLA_H, GLA_DK, GLA_DV), lambda b, n: (layer, b, 0, 0, 0))
    in_specs = [pl.BlockSpec(memory_space=pltpu.SMEM), row(d), const((1, d)),
                _layer_weight_spec(w, layer, resident=True),
                _layer_weight_spec(wa, layer), const((1, GLA_QK)), const((1, GLA_DV)),
                mem_spec, mem_spec]
    args = [sink, x2d, g, w, wa, ba, gout, mk, mv]
    aliases = {}
    if prev_state is not None:
        aliases = {len(args): 5}
        in_specs.append(pl.BlockSpec(memory_space=pl.ANY))
        args.append(prev_state)
    scratch = [pltpu.VMEM((tm, width), BF16) for width in
               (GLA_QK, GLA_QK, GLA_V, GLA_V, LANES, SWA_W, SWA_W, X_W, X_W)]
    scratch += [pltpu.VMEM((tm, KV_COLS), F32), pltpu.VMEM((WINDOW, KV_COLS), F32)]
    return pl.pallas_call(
        _front_kernel,
        grid=(batch, nblk),
        in_specs=in_specs,
        out_specs=[row(GLA_V), row(SWA_W), row(X_W), row(N_BRANCH * D_MODEL),
                   pl.BlockSpec((None, KV_COLS, WINDOW), lambda b, n: (b, 0, 0)), state_spec],
        out_shape=[jax.ShapeDtypeStruct((t, GLA_V), BF16), jax.ShapeDtypeStruct((t, SWA_W), BF16),
                   jax.ShapeDtypeStruct((t, X_W), BF16),
                   jax.ShapeDtypeStruct((t, N_BRANCH * D_MODEL), BF16),
                   jax.ShapeDtypeStruct((batch, KV_COLS, WINDOW), F32),
                   jax.ShapeDtypeStruct((depth, batch, GLA_H, GLA_DK, GLA_DV), F32)],
        scratch_shapes=scratch,
        input_output_aliases=aliases,
        compiler_params=_params("parallel", "arbitrary"),
        name="front",
    )(*args)


def _merge_kernel(x_ref, a_ref, s_ref, m_ref, ga_ref, gs_ref, gm_ref, wa_ref, ws_ref, wm_ref,
                  wo_ref, *rest, final):
    if final:
        gf_ref, y_ref = rest
    else:
        (y_ref,) = rest
    merged = (_sigmoid(ga_ref[...].astype(F32)) * _dot(a_ref[...], wa_ref[...])
              + _sigmoid(gs_ref[...].astype(F32)) * _dot(s_ref[...], ws_ref[...])
              + _sigmoid(gm_ref[...].astype(F32)) * _dot(m_ref[...], wm_ref[...]))
    y = x_ref[...] + _dot(merged.astype(BF16), wo_ref[...])
    if final:
        y = _rmsnorm_rows(y, gf_ref[...])
    y_ref[...] = y


def _merge(x2d, o_gla, o_swa, o_mem, gates, gate_col0, w_gla, w_swa, w_mem, w_out, g_final, layer):
    t, d = x2d.shape
    tm = min(t, 2 * _row_tile(t))
    final = g_final is not None
    row = lambda width: pl.BlockSpec((tm, width), lambda i: (i, 0))
    gate = lambda k: pl.BlockSpec((tm, d), lambda i: (i, gate_col0 // d + k))
    in_specs = [row(d), row(d), row(d), row(d), gate(0), gate(1), gate(2)]
    in_specs += [_layer_weight_spec(w, layer, resident=True) for w in (w_gla, w_swa, w_mem, w_out)]
    args = [x2d, o_gla, o_swa, o_mem, gates, gates, gates, w_gla, w_swa, w_mem, w_out]
    if final:
        in_specs.append(pl.BlockSpec((1, d), lambda i: (0, 0)))
        args.append(g_final)
    return pl.pallas_call(
        functools.partial(_merge_kernel, final=final),
        grid=(t // tm,),
        in_specs=in_specs,
        out_specs=row(d),
        out_shape=jax.ShapeDtypeStruct((t, d), F32),
        compiler_params=_params("parallel"),
        name="merge_final" if final else "merge",
    )(*args)


def kernel(x_prompt, x_sample, mem_prompt, state_gla, cache_swa_k, cache_swa_v, cache_mem_k,
           cache_mem_v, g_norm, w_in, w_gla_a, b_gla_a, g_gla_out, swa_sink, g_mem, w_mem_kv,
           w_br_gla, w_br_swa, w_br_mem, w_out, g_final):
    depth = w_in.shape[0]
    batch, seq, d = x_prompt.shape
    dec_batch, dec_seq, _ = x_sample.shape
    wb = cache_swa_k.shape[2]

    w_in_r = _prep_w_in(w_in)
    w_a = jnp.pad(w_gla_a, ((0, 0), (0, LANES - GLA_RANK), (0, 0))).astype(BF16)
    w_mem_b = w_mem_kv.astype(BF16)
    w_g, w_s, w_m, w_o = (w.astype(BF16) for w in (w_br_gla, w_br_swa, w_br_mem, w_out))
    g_fin = g_final.reshape(1, d)

    yp = x_prompt.reshape(batch * seq, d)
    ys = x_sample.reshape(dec_batch * dec_seq, d)
    mem2d = mem_prompt.reshape(batch * MEM_LEN, d)
    swa_t = lambda c: c.transpose(0, 1, 3, 4, 2).reshape(depth, dec_batch, SWA_KV, wb)
    ck_all, cv_all = swa_t(cache_swa_k), swa_t(cache_swa_v)
    cmk_all = _mem_rows_view(cache_mem_k)
    cmv_all = _mem_rows_view(cache_mem_v)
    gla_p = sample_new = mem_new = None
    kp_l, vp_l = [], []
    for l in range(depth):
        last = l == depth - 1
        gn = g_norm[l].reshape(1, d)
        ba = b_gla_a[l].reshape(1, GLA_QK)
        gout = g_gla_out[l].reshape(1, GLA_DV)
        sink = swa_sink[l]
        branch_w = (w_g, w_s, w_m, w_o, g_fin if last else None, l)

        mk, mv, *mem_new = _mem_kv(mem2d, g_mem[l].reshape(1, d), w_mem_b, mem_new, l, depth,
                                   batch)
        o_gla, o_swa, o_mem, gates, kv_tail, gla_p = _front(
            yp, gn, w_in_r, sink, w_a, ba, gout, mk, mv, gla_p, l, depth, batch, seq)
        yp = _merge(yp, o_gla, o_swa, o_mem, gates, 0, *branch_w)
        kv_heads = kv_tail.reshape(batch, 2, SWA_KVH, SWA_HD, WINDOW).transpose(0, 1, 4, 2, 3)
        kp_l.append(kv_heads[:, 0])
        vp_l.append(kv_heads[:, 1])

        z, kv = _in_proj(ys, gn, w_in_r, l)
        o_gla, o_swa, o_mem, *sample_new = _sample_mixers(
            z, kv, sink, w_a, ba, gout, state_gla, ck_all, cv_all, cmk_all, cmv_all, sample_new, l,
            dec_batch, dec_seq)
        ys = _merge(ys, o_gla, o_swa, o_mem, z, Z_MG, *branch_w)

    gla_s = sample_new[0]
    k_s, v_s = (c.reshape(depth, dec_batch, SWA_KVH, SWA_HD, wb).transpose(0, 1, 4, 2, 3)
                for c in sample_new[1:])
    mk_p, mv_p = (_mem_from_rows_view(c) for c in mem_new)
    return (yp.reshape(batch, seq, d), ys.reshape(dec_batch, dec_seq, d), gla_p,
            jnp.stack(kp_l), jnp.stack(vp_l), mk_p, mv_p, gla_s, k_s, v_s)
```

```python
import functools

import jax
import jax.numpy as jnp
from jax import lax
from jax.experimental import pallas as pl
from jax.experimental.pallas import tpu as pltpu

F32 = jnp.float32
BF16 = jnp.bfloat16

D_MODEL = 1024
GLA_H = 4
GLA_DK = 128
GLA_DV = 256
GLA_QK = GLA_H * GLA_DK
GLA_V = GLA_H * GLA_DV
GLA_RANK = 16
GLA_TAU = 16.0
GLA_CHUNK = 32
SWA_HD = 64
SWA_H = 16
SWA_KVH = 2
SWA_G = SWA_H // SWA_KVH
SWA_W = SWA_H * SWA_HD
SWA_KV = SWA_KVH * SWA_HD
WINDOW = 128
MEM_LEN = 256
X_H = 4
X_HD = 256
X_W = X_H * X_HD
N_BRANCH = 3
IN_SPLITS = (GLA_QK, GLA_QK, GLA_V, GLA_V, GLA_RANK, SWA_W, SWA_KV, SWA_KV, SWA_W, X_W, X_W,
             N_BRANCH * D_MODEL)
EPS = 1e-6

LANES = 128
BF16_ROWS = 16
MXU_COLS = 256
IN_PROJ_CHUNK = 2 * MXU_COLS
VMEM_LIMIT = 56 * 1024 * 1024

Z_GQ = 0
Z_GK = Z_GQ + GLA_QK
Z_GV = Z_GK + GLA_QK
Z_GZ = Z_GV + GLA_V
Z_SQ = Z_GZ + GLA_V
Z_SZ = Z_SQ + SWA_W
Z_MQ = Z_SZ + SWA_W
Z_MZ = Z_MQ + X_W
Z_MG = Z_MZ + X_W
Z_GA = Z_MG + N_BRANCH * D_MODEL
Z_COLS = Z_GA + LANES
KV_COLS = 2 * SWA_KV

(W_GQ, W_GK, W_GV, W_GZ, W_GA, W_SQ, W_SK, W_SV, W_SZ, W_MQ, W_MZ, W_MG) = (
    sum(IN_SPLITS[:i]) for i in range(len(IN_SPLITS)))
Z_PIECES = ((Z_GQ, W_GQ, GLA_QK), (Z_GK, W_GK, GLA_QK), (Z_GV, W_GV, GLA_V), (Z_GZ, W_GZ, GLA_V),
            (Z_SQ, W_SQ, SWA_W), (Z_SZ, W_SZ, SWA_W), (Z_MQ, W_MQ, X_W), (Z_MZ, W_MZ, X_W),
            (Z_MG, W_MG, N_BRANCH * D_MODEL))
W_PREP_CHUNK = MXU_COLS
W_PREP_DECAY = sum(IN_SPLITS) - GLA_RANK
W_PREP_COLS = W_PREP_DECAY + W_PREP_CHUNK


def _prepared_col(col):
    return col if col < W_GA else col - GLA_RANK


def _project(h, w_ref, col0, width):
    c = _prepared_col(col0)
    return _dot(h, w_ref[:, c:c + width])


def _project_decay(h, w_ref):
    return _dot(h, w_ref[:, W_PREP_DECAY:W_PREP_DECAY + LANES])


def _row_tile(rows):
    return 512 if rows >= 4096 else 256


def _params(*semantics):
    return pltpu.CompilerParams(dimension_semantics=semantics, vmem_limit_bytes=VMEM_LIMIT)


def _sigmoid(x):
    return 0.5 * jnp.tanh(0.5 * x) + 0.5


def _silu(x):
    return x * _sigmoid(x)


def _log_sigmoid(x):
    return jnp.minimum(x, 0.0) - jnp.log(1.0 + jnp.exp(-jnp.abs(x)))


def _rmsnorm_rows(x, g):
    return x * lax.rsqrt(jnp.mean(x * x, axis=-1, keepdims=True) + EPS) * g


def _dot(a, b):
    return jnp.dot(a, b, preferred_element_type=F32)


def _dot_nt(a, b):
    return lax.dot_general(a, b, (((1,), (1,)), ((), ())), preferred_element_type=F32)


def _dot_tn(a, b):
    return lax.dot_general(a, b, (((0,), (0,)), ((), ())), preferred_element_type=F32)


def _pad_rows(x, rows):
    if x.shape[0] >= rows:
        return x
    return jnp.concatenate([x, jnp.zeros((rows - x.shape[0], x.shape[1]), x.dtype)], axis=0)


def _interleave(steps, chunks, run_chunk):
    total = sum(w for _, w, _ in steps)
    done, acc = 0, 0
    for thunk, weight, needs in steps:
        acc += weight
        upto = max(needs, min(len(chunks), -(-acc * len(chunks) // total)))
        for chunk in chunks[done:upto]:
            run_chunk(chunk)
        done = max(done, upto)
        thunk()


def _prep_w_in_kernel(main_ref, tail_ref, decay_ref, o_ref):
    last = pl.program_id(1) == pl.num_programs(1) - 1

    @pl.when(jnp.logical_not(last))
    def _():
        o_ref[...] = main_ref[0].T.astype(BF16)

    @pl.when(last)
    def _():
        decay = decay_ref[0].T
        lane = lax.broadcasted_iota(jnp.int32, decay.shape, 1)
        decay = jnp.where(lane < GLA_RANK, decay, 0.0)
        o_ref[...] = jnp.concatenate([tail_ref[0].T, decay], axis=1).astype(BF16)


def _prep_w_in(w_in):
    depth, d, _ = w_in.shape
    w_t = w_in.transpose(0, 2, 1)
    step_cols = 2 * W_PREP_CHUNK
    nsteps = W_PREP_COLS // step_cols
    assert nsteps * step_cols == W_PREP_COLS and W_GA % step_cols == 0

    def source_row(c):
        col = c * step_cols
        row = jnp.where(col < W_GA, col, col + GLA_RANK)
        return pl.multiple_of(jnp.where(c == nsteps - 1, 0, row), BF16_ROWS)

    rows_at = lambda rows, row0: pl.BlockSpec(
        (pl.Element(1), pl.Element(rows), pl.Element(d)), lambda l, c: (l, row0, 0))
    return pl.pallas_call(
        _prep_w_in_kernel,
        grid=(depth, nsteps),
        in_specs=[pl.BlockSpec((pl.Element(1), pl.Element(step_cols), pl.Element(d)),
                               lambda l, c: (l, source_row(c), 0)),
                  rows_at(W_PREP_CHUNK, sum(IN_SPLITS) - W_PREP_CHUNK),
                  rows_at(W_PREP_CHUNK, W_GA)],
        out_specs=pl.BlockSpec((None, d, step_cols), lambda l, c: (l, 0, c)),
        out_shape=jax.ShapeDtypeStruct((depth, d, W_PREP_COLS), BF16),
        compiler_params=_params("parallel", "parallel"),
        name="prep_w_in",
    )(w_t, w_t, w_t)


def _in_proj_kernel(x_ref, g_ref, w_ref, z_ref, kv_ref):
    h = _rmsnorm_rows(x_ref[...], g_ref[...]).astype(BF16)
    for z0, w0, width in Z_PIECES:
        for c in range(0, width, IN_PROJ_CHUNK):
            z_ref[:, z0 + c:z0 + c + IN_PROJ_CHUNK] = _project(h, w_ref, w0 + c,
                                                               IN_PROJ_CHUNK).astype(BF16)
    z_ref[:, Z_GA:Z_COLS] = _project_decay(h, w_ref).astype(BF16)
    kv_ref[...] = _project(h, w_ref, W_SK, KV_COLS)


def _layer_weight_spec(w, layer, resident=False):
    mode = dict(pipeline_mode=pl.Buffered(1)) if resident else {}
    return pl.BlockSpec((None,) + w.shape[1:], lambda *ids: (layer, 0, 0), **mode)


def _in_proj(x2d, g, w, layer):
    t, d = x2d.shape
    tm = _row_tile(t)
    return pl.pallas_call(
        _in_proj_kernel,
        grid=(t // tm,),
        in_specs=[
            pl.BlockSpec((tm, d), lambda i: (i, 0)),
            pl.BlockSpec((1, d), lambda i: (0, 0)),
            _layer_weight_spec(w, layer, resident=True),
        ],
        out_specs=[pl.BlockSpec((tm, Z_COLS), lambda i: (i, 0)),
                   pl.BlockSpec((tm, KV_COLS), lambda i: (i, 0))],
        out_shape=[jax.ShapeDtypeStruct((t, Z_COLS), BF16), jax.ShapeDtypeStruct((t, KV_COLS), F32)],
        compiler_params=_params("parallel"),
        name="in_proj",
    )(x2d, g, w)


GLA_STEP_COST = dict(prepare=100, decay=230, factors=170, intra=60, step=130, epilogue=190)
SWA_STEP_COST = dict(start=50, scores=30, soft=40, out=75)
MEM_STEP_COST = dict(scores=130, out=160)


def _chunk_cumsum(g, chunk):
    pos = lax.broadcasted_iota(jnp.int32, g.shape, 0) & (chunk - 1)
    b = g
    shift = 1
    while shift < chunk:
        b = b + jnp.where(pos >= shift, pltpu.roll(b, shift, 0), 0.0)
        shift *= 2
    return b


def _gla_thunks(q_ref, k_ref, v_ref, gz_ref, ga_ref, wa_ref, ba_ref, gout_ref, s0_ref, o_ref,
                s_ref, *, chunk, nchunks, carry, anchored):
    rows = chunk * nchunks
    mm_rows = max(chunk, BF16_ROWS)
    heads = range(GLA_H)
    lanes = [slice(h * GLA_DK, (h + 1) * GLA_DK) for h in heads]
    vl = [slice(h * GLA_DV, (h + 1) * GLA_DV) for h in heads]
    per_head = lambda: [None] * GLA_H
    st = {name: per_head() for name in ("b", "qb", "kd", "qb_bf", "kd_bf", "qa_bf", "ka_bf",
                                        "dec_cols", "o_intra")}
    st["o_inter"] = [[] for _ in heads]

    def chunk_row(x, row):
        picked = [x[j * chunk + row:j * chunk + row + 1, :] for j in range(nchunks)]
        return picked, jnp.concatenate([jnp.broadcast_to(r, (chunk, x.shape[1])) for r in picked],
                                       axis=0)

    small = chunk < BF16_ROWS

    def chunk_rows(x_f32, x_bf, sl):
        return _pad_rows(x_f32[sl], mm_rows).astype(BF16) if small else x_bf[sl]

    def prepare():
        r_idx = lax.broadcasted_iota(jnp.int32, (rows, rows), 0)
        c_idx = lax.broadcasted_iota(jnp.int32, (rows, rows), 1)
        diff = r_idx - c_idx
        st["mask"] = (diff >= 0) & (diff <= (r_idx & (chunk - 1)))
        st["v_f32"] = v_ref[...].astype(F32) if small else None
        if carry:
            st["states"] = [s_ref[h] for h in heads]

    def decay(h):
        x = _dot(ga_ref[...], wa_ref[:, lanes[h]]) + ba_ref[:, lanes[h]]
        st["b"][h] = _chunk_cumsum(_log_sigmoid(x) * (1.0 / GLA_TAU), chunk)

    def factors(h):
        b = st["b"][h]
        b_last, b_last_full = chunk_row(b, chunk - 1)
        b_last_rows = jnp.concatenate(b_last + [jnp.zeros((LANES - nchunks, GLA_DK), F32)], axis=0)
        st["dec_cols"][h] = jnp.exp(b_last_rows.T)
        q = q_ref[:, lanes[h]].astype(F32) * (GLA_DK ** -0.5)
        k = k_ref[:, lanes[h]].astype(F32)
        qb = q * jnp.exp(b)
        kd = k * jnp.exp(b_last_full - b)
        st["qb"][h], st["kd"][h] = (qb, kd) if small else (None, None)
        st["qb_bf"][h], st["kd_bf"][h] = qb.astype(BF16), kd.astype(BF16)
        if anchored:
            _, b_mid_full = chunk_row(b, chunk // 2 - 1)
            st["qa_bf"][h] = (q * jnp.exp(b - b_mid_full)).astype(BF16)
            st["ka_bf"][h] = (k * jnp.exp(b_mid_full - b)).astype(BF16)
        else:
            st["qa_bf"][h] = st["qb_bf"][h]
            st["ka_bf"][h] = (k * jnp.exp(-b)).astype(BF16)

    def intra(h):
        a = jnp.where(st["mask"], _dot_nt(st["qa_bf"][h], st["ka_bf"][h]), 0.0)
        st["o_intra"][h] = _dot(a.astype(BF16), v_ref[:, vl[h]])

    def step(j):
        sl = slice(j * chunk, (j + 1) * chunk)
        for h in heads:
            state = st["states"][h] if carry else s0_ref[j, h]
            qb_j = chunk_rows(st["qb"][h], st["qb_bf"][h], sl)
            st["o_inter"][h].append(_dot(qb_j, state.astype(BF16))[:chunk])
            if small:
                v_j = _pad_rows(st["v_f32"][sl, vl[h]], mm_rows).astype(BF16)
            else:
                v_j = v_ref[sl, vl[h]]
            kd_j = chunk_rows(st["kd"][h], st["kd_bf"][h], sl)
            new_state = st["dec_cols"][h][:, j:j + 1] * state + _dot_tn(kd_j, v_j)
            if carry:
                st["states"][h] = new_state
            else:
                s_ref[j, h] = new_state

    def epilogue(h):
        if carry:
            s_ref[h] = st["states"][h]
        o = st["o_intra"][h] + jnp.concatenate(st["o_inter"][h], axis=0)
        gate = _silu(gz_ref[:, vl[h]].astype(F32))
        o_ref[:, vl[h]] = (_rmsnorm_rows(o, gout_ref[...]) * gate).astype(BF16)

    cost = GLA_STEP_COST
    steps = [(prepare, cost["prepare"])]
    steps += [(functools.partial(decay, h), cost["decay"]) for h in heads]
    steps += [(functools.partial(factors, h), cost["factors"]) for h in heads]
    steps += [(functools.partial(intra, h), cost["intra"]) for h in heads]
    steps += [(functools.partial(step, j), cost["step"]) for j in range(nchunks)]
    steps += [(functools.partial(epilogue, h), cost["epilogue"]) for h in heads]
    return steps


def _gla_sample_kernel(q_ref, k_ref, v_ref, gz_ref, ga_ref, wa_ref, ba_ref, gout_ref, s0_ref, *rest,
                       seq, group):
    o_ref, s_ref = rest[-2:]
    for thunk, _ in _gla_thunks(q_ref, k_ref, v_ref, gz_ref, ga_ref, wa_ref, ba_ref, gout_ref,
                                s0_ref, o_ref, s_ref, chunk=seq, nchunks=group, carry=False,
                                anchored=False):
        thunk()


def _swa_head_of_block(kv_head, blk):
    pairs = SWA_G // 2
    within = 2 * blk if blk < pairs else 2 * (blk - pairs) + 1
    return kv_head * SWA_G + within


def _swa_operands(kk, vv):
    lo = lax.broadcasted_iota(jnp.int32, kk.shape, 1) < SWA_HD
    kk_sw = pltpu.roll(kk, SWA_HD, 1)
    vv_sw = pltpu.roll(vv, SWA_HD, 1)
    ops = []
    for kh in range(SWA_KVH):
        k_this, k_other = (kk, kk_sw) if kh == 0 else (kk_sw, kk)
        v_this, v_other = (vv, vv_sw) if kh == 0 else (vv_sw, vv)
        ops.append((jnp.where(lo, k_this, k_other).astype(BF16),
                    jnp.where(lo, v_this, 0.0).astype(BF16),
                    jnp.where(lo, 0.0, v_other).astype(BF16)))
    return ops


def _swa_operands_t(kk_t, vv_t):
    zeros = jnp.zeros((SWA_HD, kk_t.shape[1]), F32)
    ops = []
    for kh in range(SWA_KVH):
        k_h = kk_t[kh * SWA_HD:(kh + 1) * SWA_HD]
        v_h = vv_t[kh * SWA_HD:(kh + 1) * SWA_HD]
        ops.append((jnp.concatenate([k_h, k_h], axis=0).astype(BF16),
                    jnp.concatenate([v_h, zeros], axis=0).astype(BF16),
                    jnp.concatenate([zeros, v_h], axis=0).astype(BF16)))
    return ops


def _swa_scores(q, kh, k_dup, keys_on_lanes=False):
    nq = q.shape[0]
    lo = lax.broadcasted_iota(jnp.int32, (nq, LANES), 1) < SWA_HD
    even, odd = [], []
    for p in range(SWA_G // 2):
        c0 = kh * SWA_G * SWA_HD + p * LANES
        q_pair = q[:, c0:c0 + LANES]
        even.append(jnp.where(lo, q_pair, jnp.zeros_like(q_pair)))
        odd.append(jnp.where(lo, jnp.zeros_like(q_pair), q_pair))
    lhs = jnp.concatenate(even + odd, axis=0).astype(BF16)
    s = _dot(lhs, k_dup) if keys_on_lanes else _dot_nt(lhs, k_dup)
    return s * (SWA_HD ** -0.5)


def _swa_soft_head(s_h, valid, sink):
    s_h = jnp.where(valid, s_h, -jnp.inf)
    m = jnp.maximum(jnp.max(s_h, axis=-1, keepdims=True), sink)
    p_h = jnp.exp(s_h - m)
    den = jnp.sum(p_h, axis=-1, keepdims=True) + jnp.exp(sink - m)
    if p_h.shape[0] % BF16_ROWS == 0:
        p_h = p_h.astype(BF16)
    return p_h, 1.0 / den


def _swa_out(probs, rinv, v_even, v_odd):
    pairs = SWA_G // 2
    nq = probs[0].shape[0]
    lo = lax.broadcasted_iota(jnp.int32, (nq, LANES), 1) < SWA_HD
    p_even = jnp.concatenate(probs[:pairs], axis=0).astype(BF16)
    p_odd = jnp.concatenate(probs[pairs:], axis=0).astype(BF16)
    o = _dot(p_even, v_even) + _dot(p_odd, v_odd)
    outs = [o[p * nq:(p + 1) * nq] * jnp.where(lo, rinv[p], rinv[pairs + p]) for p in range(pairs)]
    return jnp.concatenate(outs, axis=1)


def _swa_sample_kernel(sink_ref, q_ref, sz_ref, kvn_ref, ck_ref, cv_ref, *rest, group, seq):
    o_ref, nk_ref, nv_ref = rest[-3:]
    wb = ck_ref.shape[2]
    nk = 2 * wb
    pairs = SWA_G // 2
    blk_rows = SWA_G * seq
    first = lax.broadcasted_iota(jnp.int32, (SWA_KV, wb), 1) < seq
    q = q_ref[...].astype(F32)
    kvn = _pad_rows(kvn_ref[...], LANES)
    kn_all, vn_all = kvn[:, :SWA_KV].T, kvn[:, SWA_KV:].T

    scores, values = [], []
    for e in range(group):
        rows = slice(e * seq, (e + 1) * seq)
        k_old, v_old = ck_ref[e], cv_ref[e]
        shift = (LANES - e * seq) % LANES
        k_new = pltpu.roll(kn_all, shift, 1) if shift else kn_all
        v_new = pltpu.roll(vn_all, shift, 1) if shift else vn_all
        nk_ref[e] = pltpu.roll(jnp.where(first, k_new, k_old), wb - seq, 1)
        nv_ref[e] = pltpu.roll(jnp.where(first, v_new, v_old), wb - seq, 1)
        ops = _swa_operands_t(jnp.concatenate([k_old, k_new], axis=1),
                              jnp.concatenate([v_old, v_new], axis=1))
        for kh in range(SWA_KVH):
            k_dup, v_even, v_odd = ops[kh]
            scores.append(_swa_scores(q[rows], kh, k_dup, keys_on_lanes=True))
            values.append((v_even, v_odd))
    s_all = jnp.concatenate(scores, axis=0)
    i = lax.broadcasted_iota(jnp.int32, s_all.shape, 0) & (seq - 1)
    j = lax.broadcasted_iota(jnp.int32, s_all.shape, 1)
    diff = i + wb - j
    sink_kh = [jnp.concatenate([jnp.full((seq, 1), sink_ref[_swa_head_of_block(kh, blk)], F32)
                                for blk in range(SWA_G)], axis=0) for kh in range(SWA_KVH)]
    p_all, rinv_all = _swa_soft_head(s_all, (diff >= 0) & (diff <= WINDOW),
                                     jnp.concatenate(sink_kh * group, axis=0))

    lo = lax.broadcasted_iota(jnp.int32, (seq, LANES), 1) < SWA_HD
    outs = [[] for _ in range(group)]
    for idx, (v_even, v_odd) in enumerate(values):
        base = idx * blk_rows
        half = pairs * seq
        o = (_dot_nt(p_all[base:base + half], v_even)
             + _dot_nt(p_all[base + half:base + blk_rows], v_odd))
        for p in range(pairs):
            r_even = rinv_all[base + p * seq:base + (p + 1) * seq]
            r_odd = rinv_all[base + half + p * seq:base + half + (p + 1) * seq]
            outs[idx // SWA_KVH].append(o[p * seq:(p + 1) * seq] * jnp.where(lo, r_even, r_odd))
    o = jnp.concatenate([jnp.concatenate(row, axis=1) for row in outs], axis=0)
    o_ref[...] = (o * _silu(sz_ref[...].astype(F32))).astype(BF16)


MEM_HALVES = X_HD // LANES
MEM_TILE_ROWS = X_H * MEM_HALVES


def _mem_rows_view(x):
    lead = x.shape[:-3]
    n = len(lead)
    x = x.reshape(*lead, MEM_LEN, X_H, MEM_HALVES, LANES)
    x = x.transpose(*range(n), n, n + 2, n + 1, n + 3)
    return x.reshape(*lead, MEM_LEN * MEM_TILE_ROWS, LANES)


def _mem_from_rows_view(x):
    lead = x.shape[:-2]
    n = len(lead)
    x = x.reshape(*lead, MEM_LEN, MEM_HALVES, X_H, LANES)
    x = x.transpose(*range(n), n, n + 2, n + 1, n + 3)
    return x.reshape(*lead, MEM_LEN, X_H, X_HD)


def _mem_head_rows(head, half):
    return pl.ds(half * X_H + head, MEM_LEN, stride=MEM_TILE_ROWS)


def _mem_kv_kernel(x_ref, g_ref, w_ref, *rest):
    kb_ref, vb_ref, kn_ref, vn_ref = rest[-4:]
    h = _rmsnorm_rows(x_ref[...], g_ref[...]).astype(BF16)
    for t, (b_ref, n_ref) in enumerate(((kb_ref, kn_ref), (vb_ref, vn_ref))):
        for head in range(X_H):
            c0 = head * X_HD
            res = _dot(h, w_ref[:, t * X_W + c0:t * X_W + c0 + X_HD])
            b_ref[:, c0:c0 + X_HD] = res.astype(BF16)
            for half in range(MEM_HALVES):
                n_ref[_mem_head_rows(head, half), :] = res[:, half * LANES:(half + 1) * LANES]


def _mem_kv(mem2d, g, w, prev, layer, depth, batch):
    d = mem2d.shape[1]
    rows_spec = pl.BlockSpec((None, None, MEM_LEN * MEM_TILE_ROWS, LANES), lambda i: (layer, i, 0, 0))
    rows_shape = jax.ShapeDtypeStruct((depth, batch, MEM_LEN * MEM_TILE_ROWS, LANES), F32)
    dense_spec = pl.BlockSpec((MEM_LEN, X_W), lambda i: (i, 0))
    dense_shape = jax.ShapeDtypeStruct((batch * MEM_LEN, X_W), BF16)
    in_specs = [pl.BlockSpec((MEM_LEN, d), lambda i: (i, 0)),
                pl.BlockSpec((1, d), lambda i: (0, 0)),
                _layer_weight_spec(w, layer)]
    args = [mem2d, g, w]
    aliases = {}
    if prev is not None:
        aliases = {len(args): 2, len(args) + 1: 3}
        in_specs += [pl.BlockSpec(memory_space=pl.ANY)] * 2
        args += list(prev)
    return pl.pallas_call(
        _mem_kv_kernel,
        grid=(batch,),
        in_specs=in_specs,
        out_specs=[dense_spec, dense_spec, rows_spec, rows_spec],
        out_shape=[dense_shape, dense_shape, rows_shape, rows_shape],
        input_output_aliases=aliases,
        compiler_params=_params("parallel"),
        name="mem_kv",
    )(*args)


def _mem_softmax(s):
    p = jnp.exp(s - jnp.max(s, axis=-1, keepdims=True))
    return p, 1.0 / jnp.sum(p, axis=-1, keepdims=True)


def _mem_sample_kernel(q_ref, mz_ref, mk_ref, mv_ref, o_ref, *, group, seq):
    def head_tile(ref, e, h):
        halves = [ref[e, _mem_head_rows(h, half), :] for half in range(MEM_HALVES)]
        return jnp.concatenate(halves, axis=1).astype(BF16)

    q = q_ref[...].astype(F32)
    pairs = [(e, h) for e in range(group) for h in range(X_H)]
    scores = []
    for e, h in pairs:
        q_eh = _pad_rows(q[e * seq:(e + 1) * seq, h * X_HD:(h + 1) * X_HD], BF16_ROWS).astype(BF16)
        scores.append(_dot_nt(q_eh, head_tile(mk_ref, e, h))[:seq])
    p, rinv = _mem_softmax(jnp.concatenate(scores, axis=0) * (X_HD ** -0.5))
    outs = [[] for _ in range(group)]
    for i, (e, h) in enumerate(pairs):
        p_eh = _pad_rows(p[i * seq:(i + 1) * seq], BF16_ROWS).astype(BF16)
        outs[e].append(_dot(p_eh, head_tile(mv_ref, e, h))[:seq] * rinv[i * seq:(i + 1) * seq])
    o = jnp.concatenate([jnp.concatenate(row, axis=1) for row in outs], axis=0)
    o_ref[...] = (o * _silu(mz_ref[...].astype(F32))).astype(BF16)


SAMPLE_GROUP = 4


def _sample_mixers_kernel(sink_ref, gq_ref, gk_ref, gv_ref, gz_ref, ga_ref, sq_ref, sz_ref, mq_ref,
                          mz_ref, kvn_ref, wa_ref, ba_ref, gout_ref, s0_ref, ck_ref, cv_ref, mk_ref,
                          mv_ref, *rest, seq):
    ogla_ref, oswa_ref, omem_ref, s_ref, nk_ref, nv_ref = rest[-6:]
    _mem_sample_kernel(mq_ref, mz_ref, mk_ref, mv_ref, omem_ref, group=SAMPLE_GROUP, seq=seq)
    _swa_sample_kernel(sink_ref, sq_ref, sz_ref, kvn_ref, ck_ref, cv_ref, oswa_ref, nk_ref, nv_ref,
                       group=SAMPLE_GROUP, seq=seq)
    _gla_sample_kernel(gq_ref, gk_ref, gv_ref, gz_ref, ga_ref, wa_ref, ba_ref, gout_ref, s0_ref,
                       ogla_ref, s_ref, seq=seq, group=SAMPLE_GROUP)


def _sample_mixers(z, kv, sink, wa, ba, gout, s0, ck, cv, mk, mv, prev, layer, batch, seq):
    group = SAMPLE_GROUP
    rows = group * seq
    wb = ck.shape[3]
    assert wb == LANES, "the sliding window is one lane tile wide"
    zspec = lambda width, col0: pl.BlockSpec((rows, width), lambda i: (i, col0 // width))
    const = lambda shape: pl.BlockSpec(shape, lambda i: (0,) * len(shape))
    layered = lambda *dims: pl.BlockSpec((None, group) + dims, lambda i: (layer, i) + (0,) * len(dims))
    state_spec = layered(GLA_H, GLA_DK, GLA_DV)
    cache_spec = layered(SWA_KV, wb)
    mem_spec = layered(MEM_LEN * MEM_TILE_ROWS, LANES)
    in_specs = [pl.BlockSpec(memory_space=pltpu.SMEM),
                zspec(GLA_QK, Z_GQ), zspec(GLA_QK, Z_GK), zspec(GLA_V, Z_GV), zspec(GLA_V, Z_GZ),
                zspec(LANES, Z_GA), zspec(SWA_W, Z_SQ), zspec(SWA_W, Z_SZ), zspec(X_W, Z_MQ),
                zspec(X_W, Z_MZ), pl.BlockSpec((rows, KV_COLS), lambda i: (i, 0)),
                _layer_weight_spec(wa, layer), const((1, GLA_QK)), const((1, GLA_DV)),
                state_spec, cache_spec, cache_spec, mem_spec, mem_spec]
    args = [sink] + [z] * 9 + [kv, wa, ba, gout, s0, ck, cv, mk, mv]
    aliases = {}
    if prev is not None:
        aliases = {len(args) + k: 3 + k for k in range(3)}
        in_specs += [pl.BlockSpec(memory_space=pl.ANY)] * 3
        args += list(prev)
    out_row = lambda width: pl.BlockSpec((rows, width), lambda i: (i, 0))
    return pl.pallas_call(
        functools.partial(_sample_mixers_kernel, seq=seq),
        grid=(batch // group,),
        in_specs=in_specs,
        out_specs=[out_row(GLA_V), out_row(SWA_W), out_row(X_W), state_spec, cache_spec, cache_spec],
        out_shape=[jax.ShapeDtypeStruct((batch * seq, GLA_V), BF16),
                   jax.ShapeDtypeStruct((batch * seq, SWA_W), BF16),
                   jax.ShapeDtypeStruct((batch * seq, X_W), BF16),
                   jax.ShapeDtypeStruct(s0.shape, F32), jax.ShapeDtypeStruct(ck.shape, F32),
                   jax.ShapeDtypeStruct(cv.shape, F32)],
        input_output_aliases=aliases,
        compiler_params=_params("parallel"),
        name="sample_mixers",
    )(*args)


FRONT_ROWS = 256
FRONT_GLA_CHUNK = 2 * GLA_CHUNK

def _front_kernel(sink_ref, x_ref, g_ref, w_ref, wa_ref, ba_ref, gout_ref, mk_ref, mv_ref, *rest):
    (ogla_ref, oswa_ref, omem_ref, mg_ref, kvl_ref, s_ref,
     gq_scr, gk_scr, gv_scr, gz_scr, ga_scr, sq_scr, sz_scr, mq_scr, mz_scr, kv_scr, kvp_scr) = rest[-17:]
    n = pl.program_id(1)
    tm = FRONT_ROWS

    @pl.when(n == 0)
    def _():
        s_ref[...] = jnp.zeros_like(s_ref)
        kvp_scr[...] = jnp.zeros_like(kvp_scr)

    h = _rmsnorm_rows(x_ref[...], g_ref[...]).astype(BF16)

    def project(chunk):
        dst, dst_c0, w_r0, width = chunk
        if dst is ga_scr:
            dst[...] = _project_decay(h, w_ref).astype(dst.dtype)
        else:
            dst[:, dst_c0:dst_c0 + width] = _project(h, w_ref, w_r0, width).astype(dst.dtype)

    def chunks_of(dst, w_r0, width):
        return [(dst, c, w_r0 + c, min(MXU_COLS, width - c)) for c in range(0, width, MXU_COLS)]

    swa_cols = chunks_of(kv_scr, W_SK, KV_COLS) + chunks_of(sq_scr, W_SQ, SWA_W) \
        + chunks_of(sz_scr, W_SZ, SWA_W)
    gla_cols = (chunks_of(ga_scr, W_GA, LANES) + chunks_of(gq_scr, W_GQ, GLA_QK)
                + chunks_of(gk_scr, W_GK, GLA_QK) + chunks_of(gv_scr, W_GV, GLA_V)
                + chunks_of(gz_scr, W_GZ, GLA_V))
    mem_cols = chunks_of(mq_scr, W_MQ, X_W) + chunks_of(mz_scr, W_MZ, X_W)
    merge_cols = chunks_of(mg_ref, W_MG, N_BRANCH * D_MODEL)

    i = lax.broadcasted_iota(jnp.int32, (WINDOW, 2 * WINDOW), 0)
    j = lax.broadcasted_iota(jnp.int32, (WINDOW, 2 * WINDOW), 1)
    diff = i + WINDOW - j
    band = (diff >= 0) & (diff <= WINDOW)
    swa = {}

    def swa_start(blk):
        prev = kvp_scr[...] if blk == 0 else kv_scr[(blk - 1) * WINDOW:blk * WINDOW, :]
        kvb = jnp.concatenate([prev, kv_scr[blk * WINDOW:(blk + 1) * WINDOW, :]], axis=0)
        swa["ops"] = _swa_operands(kvb[:, :SWA_KV], kvb[:, SWA_KV:])
        swa["valid"] = band & (j >= jnp.where(n > 0, 0, WINDOW)) if blk == 0 else band

    def swa_scores(blk, kh):
        swa["s"] = _swa_scores(sq_scr[blk * WINDOW:(blk + 1) * WINDOW, :], kh, swa["ops"][kh][0])
        swa["soft"] = []

    def swa_soft(kh, hb):
        swa["soft"].append(_swa_soft_head(swa["s"][hb * WINDOW:(hb + 1) * WINDOW], swa["valid"],
                                          sink_ref[_swa_head_of_block(kh, hb)]))

    def swa_out(blk, kh):
        _, v_even, v_odd = swa["ops"][kh]
        o = _swa_out([p for p, _ in swa["soft"]], [r for _, r in swa["soft"]], v_even, v_odd)
        rows = slice(blk * WINDOW, (blk + 1) * WINDOW)
        cols = slice(kh * SWA_G * SWA_HD, (kh + 1) * SWA_G * SWA_HD)
        oswa_ref[rows, cols] = (o * _silu(sz_scr[rows, cols].astype(F32))).astype(BF16)

    swa_steps = []
    for blk in range(tm // WINDOW):
        swa_steps.append((functools.partial(swa_start, blk), SWA_STEP_COST["start"]))
        for kh in range(SWA_KVH):
            swa_steps.append((functools.partial(swa_scores, blk, kh), SWA_STEP_COST["scores"]))
            swa_steps += [(functools.partial(swa_soft, kh, hb), SWA_STEP_COST["soft"])
                          for hb in range(SWA_G)]
            swa_steps.append((functools.partial(swa_out, blk, kh), SWA_STEP_COST["out"]))

    mem = {}

    def mem_scores(hd):
        cols = slice(hd * X_HD, (hd + 1) * X_HD)
        p, rinv = _mem_softmax(_dot_nt(mq_scr[:, cols], mk_ref[:, cols]) * (X_HD ** -0.5))
        mem["p"], mem["rinv"] = p.astype(BF16), rinv

    def mem_out(hd):
        cols = slice(hd * X_HD, (hd + 1) * X_HD)
        o = _dot(mem["p"], mv_ref[:, cols]) * mem["rinv"]
        omem_ref[:, cols] = (o * _silu(mz_scr[:, cols].astype(F32))).astype(BF16)

    mem_steps = []
    for hd in range(X_H):
        mem_steps += [(functools.partial(mem_scores, hd), MEM_STEP_COST["scores"]),
                      (functools.partial(mem_out, hd), MEM_STEP_COST["out"])]

    gla_steps = _gla_thunks(gq_scr, gk_scr, gv_scr, gz_scr, ga_scr, wa_ref, ba_ref, gout_ref, None,
                            ogla_ref, s_ref, chunk=FRONT_GLA_CHUNK, nchunks=tm // FRONT_GLA_CHUNK,
                            carry=True, anchored=True)

    for chunk in swa_cols:
        project(chunk)
    chunks = gla_cols + mem_cols + merge_cols
    steps = [(t, w, 0) for t, w in swa_steps]
    steps += [(t, w, len(gla_cols)) for t, w in gla_steps]
    steps += [(t, w, len(gla_cols) + len(mem_cols)) for t, w in mem_steps]
    _interleave(steps, chunks, project)

    kv_tail = kv_scr[tm - WINDOW:, :]
    kvp_scr[...] = kv_tail
    kvl_ref[...] = kv_tail.T


def _front(x2d, g, w, sink, wa, ba, gout, mk, mv, prev_state, layer, depth, batch, seq):
    t, d = x2d.shape
    tm = FRONT_ROWS
    nblk = seq // tm
    row = lambda width: pl.BlockSpec((tm, width), lambda b, n: (b * nblk + n, 0))
    const = lambda shape: pl.BlockSpec(shape, lambda b, n: (0,) * len(shape))
    mem_spec = pl.BlockSpec((MEM_LEN, X_W), lambda b, n: (b, 0))
    state_spec = pl.BlockSpec((None, None, GLA_H, GLA_DK, GLA_DV), lambda b, n: (layer, b, 0, 0, 0))
    in_specs = [pl.BlockSpec(memory_space=pltpu.SMEM), row(d), const((1, d)),
                _layer_weight_spec(w, layer, resident=True),
                _layer_weight_spec(wa, layer), const((1, GLA_QK)), const((1, GLA_DV)),
                mem_spec, mem_spec]
    args = [sink, x2d, g, w, wa, ba, gout, mk, mv]
    aliases = {}
    if prev_state is not None:
        aliases = {len(args): 5}
        in_specs.append(pl.BlockSpec(memory_space=pl.ANY))
        args.append(prev_state)
    scratch = [pltpu.VMEM((tm, width), BF16) for width in
               (GLA_QK, GLA_QK, GLA_V, GLA_V, LANES, SWA_W, SWA_W, X_W, X_W)]
    scratch += [pltpu.VMEM((tm, KV_COLS), F32), pltpu.VMEM((WINDOW, KV_COLS), F32)]
    return pl.pallas_call(
        _front_kernel,
        grid=(batch, nblk),
        in_specs=in_specs,
        out_specs=[row(GLA_V), row(SWA_W), row(X_W), row(N_BRANCH * D_MODEL),
                   pl.BlockSpec((None, KV_COLS, WINDOW), lambda b, n: (b, 0, 0)), state_spec],
        out_shape=[jax.ShapeDtypeStruct((t, GLA_V), BF16), jax.ShapeDtypeStruct((t, SWA_W), BF16),
                   jax.ShapeDtypeStruct((t, X_W), BF16),
                   jax.ShapeDtypeStruct((t, N_BRANCH * D_MODEL), BF16),
                   jax.ShapeDtypeStruct((batch, KV_COLS, WINDOW), F32),
                   jax.ShapeDtypeStruct((depth, batch, GLA_H, GLA_DK, GLA_DV), F32)],
        scratch_shapes=scratch,
        input_output_aliases=aliases,
        compiler_params=_params("parallel", "arbitrary"),
        name="front",
    )(*args)


def _merge_kernel(x_ref, a_ref, s_ref, m_ref, ga_ref, gs_ref, gm_ref, wa_ref, ws_ref, wm_ref,
                  wo_ref, *rest, final):
    if final:
        gf_ref, y_ref = rest
    else:
        (y_ref,) = rest
    merged = (_sigmoid(ga_ref[...].astype(F32)) * _dot(a_ref[...], wa_ref[...])
              + _sigmoid(gs_ref[...].astype(F32)) * _dot(s_ref[...], ws_ref[...])
              + _sigmoid(gm_ref[...].astype(F32)) * _dot(m_ref[...], wm_ref[...]))
    y = x_ref[...] + _dot(merged.astype(BF16), wo_ref[...])
    if final:
        y = _rmsnorm_rows(y, gf_ref[...])
    y_ref[...] = y


def _merge(x2d, o_gla, o_swa, o_mem, gates, gate_col0, w_gla, w_swa, w_mem, w_out, g_final, layer):
    t, d = x2d.shape
    tm = min(t, 2 * _row_tile(t))
    final = g_final is not None
    row = lambda width: pl.BlockSpec((tm, width), lambda i: (i, 0))
    gate = lambda k: pl.BlockSpec((tm, d), lambda i: (i, gate_col0 // d + k))
    in_specs = [row(d), row(d), row(d), row(d), gate(0), gate(1), gate(2)]
    in_specs += [_layer_weight_spec(w, layer, resident=True) for w in (w_gla, w_swa, w_mem, w_out)]
    args = [x2d, o_gla, o_swa, o_mem, gates, gates, gates, w_gla, w_swa, w_mem, w_out]
    if final:
        in_specs.append(pl.BlockSpec((1, d), lambda i: (0, 0)))
        args.append(g_final)
    return pl.pallas_call(
        functools.partial(_merge_kernel, final=final),
        grid=(t // tm,),
        in_specs=in_specs,
        out_specs=row(d),
        out_shape=jax.ShapeDtypeStruct((t, d), F32),
        compiler_params=_params("parallel"),
        name="merge_final" if final else "merge",
    )(*args)


def kernel(x_prompt, x_sample, mem_prompt, state_gla, cache_swa_k, cache_swa_v, cache_mem_k,
           cache_mem_v, g_norm, w_in, w_gla_a, b_gla_a, g_gla_out, swa_sink, g_mem, w_mem_kv,
           w_br_gla, w_br_swa, w_br_mem, w_out, g_final):
    depth = w_in.shape[0]
    batch, seq, d = x_prompt.shape
    dec_batch, dec_seq, _ = x_sample.shape
    wb = cache_swa_k.shape[2]

    w_in_r = _prep_w_in(w_in)
    w_a = jnp.pad(w_gla_a, ((0, 0), (0, LANES - GLA_RANK), (0, 0))).astype(BF16)
    w_mem_b = w_mem_kv.astype(BF16)
    w_g, w_s, w_m, w_o = (w.astype(BF16) for w in (w_br_gla, w_br_swa, w_br_mem, w_out))
    g_fin = g_final.reshape(1, d)

    yp = x_prompt.reshape(batch * seq, d)
    ys = x_sample.reshape(dec_batch * dec_seq, d)
    mem2d = mem_prompt.reshape(batch * MEM_LEN, d)
    swa_t = lambda c: c.transpose(0, 1, 3, 4, 2).reshape(depth, dec_batch, SWA_KV, wb)
    ck_all, cv_all = swa_t(cache_swa_k), swa_t(cache_swa_v)
    cmk_all = _mem_rows_view(cache_mem_k)
    cmv_all = _mem_rows_view(cache_mem_v)
    gla_p = sample_new = mem_new = None
    kp_l, vp_l = [], []
    for l in range(depth):
        last = l == depth - 1
        gn = g_norm[l].reshape(1, d)
        ba = b_gla_a[l].reshape(1, GLA_QK)
        gout = g_gla_out[l].reshape(1, GLA_DV)
        sink = swa_sink[l]
        branch_w = (w_g, w_s, w_m, w_o, g_fin if last else None, l)

        mk, mv, *mem_new = _mem_kv(mem2d, g_mem[l].reshape(1, d), w_mem_b, mem_new, l, depth,
                                   batch)
        o_gla, o_swa, o_mem, gates, kv_tail, gla_p = _front(
            yp, gn, w_in_r, sink, w_a, ba, gout, mk, mv, gla_p, l, depth, batch, seq)
        yp = _merge(yp, o_gla, o_swa, o_mem, gates, 0, *branch_w)
        kv_heads = kv_tail.reshape(batch, 2, SWA_KVH, SWA_HD, WINDOW).transpose(0, 1, 4, 2, 3)
        kp_l.append(kv_heads[:, 0])
        vp_l.append(kv_heads[:, 1])

        z, kv = _in_proj(ys, gn, w_in_r, l)
        o_gla, o_swa, o_mem, *sample_new = _sample_mixers(
            z, kv, sink, w_a, ba, gout, state_gla, ck_all, cv_all, cmk_all, cmv_all, sample_new, l,
            dec_batch, dec_seq)
        ys = _merge(ys, o_gla, o_swa, o_mem, z, Z_MG, *branch_w)

    gla_s = sample_new[0]
    k_s, v_s = (c.reshape(depth, dec_batch, SWA_KVH, SWA_HD, wb).transpose(0, 1, 4, 2, 3)
                for c in sample_new[1:])
    mk_p, mv_p = (_mem_from_rows_view(c) for c in mem_new)
    return (yp.reshape(batch, seq, d), ys.reshape(dec_batch, dec_seq, d), gla_p,
            jnp.stack(kp_l), jnp.stack(vp_l), mk_p, mv_p, gla_s, k_s, v_s)
```

```python
import functools

import jax
import jax.numpy as jnp
from jax import lax
from jax.experimental import pallas as pl
from jax.experimental.pallas import tpu as pltpu

F32 = jnp.float32
BF16 = jnp.bfloat16

D_MODEL = 1024
GLA_H = 4
GLA_DK = 128
GLA_DV = 256
GLA_QK = GLA_H * GLA_DK
GLA_V = GLA_H * GLA_DV
GLA_RANK = 16
GLA_TAU = 16.0
GLA_CHUNK = 32
SWA_HD = 64
SWA_H = 16
SWA_KVH = 2
SWA_G = SWA_H // SWA_KVH
SWA_W = SWA_H * SWA_HD
SWA_KV = SWA_KVH * SWA_HD
WINDOW = 128
MEM_LEN = 256
X_H = 4
X_HD = 256
X_W = X_H * X_HD
N_BRANCH = 3
IN_SPLITS = (GLA_QK, GLA_QK, GLA_V, GLA_V, GLA_RANK, SWA_W, SWA_KV, SWA_KV, SWA_W, X_W, X_W,
             N_BRANCH * D_MODEL)
EPS = 1e-6

LANES = 128
BF16_ROWS = 16
MXU_COLS = 256
IN_PROJ_CHUNK = 2 * MXU_COLS
VMEM_LIMIT = 56 * 1024 * 1024

Z_GQ = 0
Z_GK = Z_GQ + GLA_QK
Z_GV = Z_GK + GLA_QK
Z_GZ = Z_GV + GLA_V
Z_SQ = Z_GZ + GLA_V
Z_SZ = Z_SQ + SWA_W
Z_MQ = Z_SZ + SWA_W
Z_MZ = Z_MQ + X_W
Z_MG = Z_MZ + X_W
Z_GA = Z_MG + N_BRANCH * D_MODEL
Z_COLS = Z_GA + LANES
KV_COLS = 2 * SWA_KV

(W_GQ, W_GK, W_GV, W_GZ, W_GA, W_SQ, W_SK, W_SV, W_SZ, W_MQ, W_MZ, W_MG) = (
    sum(IN_SPLITS[:i]) for i in range(len(IN_SPLITS)))
Z_PIECES = ((Z_GQ, W_GQ, GLA_QK), (Z_GK, W_GK, GLA_QK), (Z_GV, W_GV, GLA_V), (Z_GZ, W_GZ, GLA_V),
            (Z_SQ, W_SQ, SWA_W), (Z_SZ, W_SZ, SWA_W), (Z_MQ, W_MQ, X_W), (Z_MZ, W_MZ, X_W),
            (Z_MG, W_MG, N_BRANCH * D_MODEL))
W_PREP_CHUNK = MXU_COLS
W_PREP_DECAY = sum(IN_SPLITS) - GLA_RANK
W_PREP_COLS = W_PREP_DECAY + W_PREP_CHUNK


def _prepared_col(col):
    return col if col < W_GA else col - GLA_RANK


def _project(h, w_ref, col0, width):
    c = _prepared_col(col0)
    return _dot(h, w_ref[:, c:c + width])


def _project_decay(h, w_ref):
    return _dot(h, w_ref[:, W_PREP_DECAY:W_PREP_DECAY + LANES])


def _row_tile(rows):
    return 512 if rows >= 4096 else 256


def _params(*semantics):
    return pltpu.CompilerParams(dimension_semantics=semantics, vmem_limit_bytes=VMEM_LIMIT)


def _sigmoid(x):
    return 0.5 * jnp.tanh(0.5 * x) + 0.5


def _silu(x):
    return x * _sigmoid(x)


def _log_sigmoid(x):
    return jnp.minimum(x, 0.0) - jnp.log(1.0 + jnp.exp(-jnp.abs(x)))


def _rmsnorm_rows(x, g):
    return x * lax.rsqrt(jnp.mean(x * x, axis=-1, keepdims=True) + EPS) * g


def _dot(a, b):
    return jnp.dot(a, b, preferred_element_type=F32)


def _dot_nt(a, b):
    return lax.dot_general(a, b, (((1,), (1,)), ((), ())), preferred_element_type=F32)


def _dot_tn(a, b):
    return lax.dot_general(a, b, (((0,), (0,)), ((), ())), preferred_element_type=F32)


def _pad_rows(x, rows):
    if x.shape[0] >= rows:
        return x
    return jnp.concatenate([x, jnp.zeros((rows - x.shape[0], x.shape[1]), x.dtype)], axis=0)


def _interleave(steps, chunks, run_chunk):
    total = sum(w for _, w, _ in steps)
    done, acc = 0, 0
    for thunk, weight, needs in steps:
        acc += weight
        upto = max(needs, min(len(chunks), -(-acc * len(chunks) // total)))
        for chunk in chunks[done:upto]:
            run_chunk(chunk)
        done = max(done, upto)
        thunk()


def _prep_w_in_kernel(main_ref, tail_ref, decay_ref, o_ref):
    last = pl.program_id(1) == pl.num_programs(1) - 1

    @pl.when(jnp.logical_not(last))
    def _():
        o_ref[...] = main_ref[0].T.astype(BF16)

    @pl.when(last)
    def _():
        decay = decay_ref[0].T
        lane = lax.broadcasted_iota(jnp.int32, decay.shape, 1)
        decay = jnp.where(lane < GLA_RANK, decay, 0.0)
        o_ref[...] = jnp.concatenate([tail_ref[0].T, decay], axis=1).astype(BF16)


def _prep_w_in(w_in):
    depth, d, _ = w_in.shape
    w_t = w_in.transpose(0, 2, 1)
    step_cols = 2 * W_PREP_CHUNK
    nsteps = W_PREP_COLS // step_cols
    assert nsteps * step_cols == W_PREP_COLS and W_GA % step_cols == 0

    def source_row(c):
        col = c * step_cols
        row = jnp.where(col < W_GA, col, col + GLA_RANK)
        return pl.multiple_of(jnp.where(c == nsteps - 1, 0, row), BF16_ROWS)

    rows_at = lambda rows, row0: pl.BlockSpec(
        (pl.Element(1), pl.Element(rows), pl.Element(d)), lambda l, c: (l, row0, 0))
    return pl.pallas_call(
        _prep_w_in_kernel,
        grid=(depth, nsteps),
        in_specs=[pl.BlockSpec((pl.Element(1), pl.Element(step_cols), pl.Element(d)),
                               lambda l, c: (l, source_row(c), 0)),
                  rows_at(W_PREP_CHUNK, sum(IN_SPLITS) - W_PREP_CHUNK),
                  rows_at(W_PREP_CHUNK, W_GA)],
        out_specs=pl.BlockSpec((None, d, step_cols), lambda l, c: (l, 0, c)),
        out_shape=jax.ShapeDtypeStruct((depth, d, W_PREP_COLS), BF16),
        compiler_params=_params("parallel", "parallel"),
        name="prep_w_in",
    )(w_t, w_t, w_t)


def _in_proj_kernel(x_ref, g_ref, w_ref, z_ref, kv_ref):
    h = _rmsnorm_rows(x_ref[...], g_ref[...]).astype(BF16)
    for z0, w0, width in Z_PIECES:
        for c in range(0, width, IN_PROJ_CHUNK):
            z_ref[:, z0 + c:z0 + c + IN_PROJ_CHUNK] = _project(h, w_ref, w0 + c,
                                                               IN_PROJ_CHUNK).astype(BF16)
    z_ref[:, Z_GA:Z_COLS] = _project_decay(h, w_ref).astype(BF16)
    kv_ref[...] = _project(h, w_ref, W_SK, KV_COLS)


def _layer_weight_spec(w, layer, resident=False):
    mode = dict(pipeline_mode=pl.Buffered(1)) if resident else {}
    return pl.BlockSpec((None,) + w.shape[1:], lambda *ids: (layer, 0, 0), **mode)


def _in_proj(x2d, g, w, layer):
    t, d = x2d.shape
    tm = _row_tile(t)
    return pl.pallas_call(
        _in_proj_kernel,
        grid=(t // tm,),
        in_specs=[
            pl.BlockSpec((tm, d), lambda i: (i, 0)),
            pl.BlockSpec((1, d), lambda i: (0, 0)),
            _layer_weight_spec(w, layer, resident=True),
        ],
        out_specs=[pl.BlockSpec((tm, Z_COLS), lambda i: (i, 0)),
                   pl.BlockSpec((tm, KV_COLS), lambda i: (i, 0))],
        out_shape=[jax.ShapeDtypeStruct((t, Z_COLS), BF16), jax.ShapeDtypeStruct((t, KV_COLS), F32)],
        compiler_params=_params("parallel"),
        name="in_proj",
    )(x2d, g, w)


GLA_STEP_COST = dict(prepare=100, decay=230, factors=170, intra=60, step=130, epilogue=190)
SWA_STEP_COST = dict(start=50, scores=30, soft=40, out=75)
MEM_STEP_COST = dict(scores=130, out=160)


def _chunk_cumsum(g, chunk):
    pos = lax.broadcasted_iota(jnp.int32, g.shape, 0) & (chunk - 1)
    b = g
    shift = 1
    while shift < chunk:
        b = b + jnp.where(pos >= shift, pltpu.roll(b, shift, 0), 0.0)
        shift *= 2
    return b


def _gla_thunks(q_ref, k_ref, v_ref, gz_ref, ga_ref, wa_ref, ba_ref, gout_ref, s0_ref, o_ref,
                s_ref, *, chunk, nchunks, carry, anchored):
    rows = chunk * nchunks
    mm_rows = max(chunk, BF16_ROWS)
    heads = range(GLA_H)
    lanes = [slice(h * GLA_DK, (h + 1) * GLA_DK) for h in heads]
    vl = [slice(h * GLA_DV, (h + 1) * GLA_DV) for h in heads]
    per_head = lambda: [None] * GLA_H
    st = {name: per_head() for name in ("b", "qb", "kd", "qb_bf", "kd_bf", "qa_bf", "ka_bf",
                                        "dec_cols", "o_intra")}
    st["o_inter"] = [[] for _ in heads]

    def chunk_row(x, row):
        picked = [x[j * chunk + row:j * chunk + row + 1, :] for j in range(nchunks)]
        return picked, jnp.concatenate([jnp.broadcast_to(r, (chunk, x.shape[1])) for r in picked],
                                       axis=0)

    small = chunk < BF16_ROWS

    def chunk_rows(x_f32, x_bf, sl):
        return _pad_rows(x_f32[sl], mm_rows).astype(BF16) if small else x_bf[sl]

    def prepare():
        r_idx = lax.broadcasted_iota(jnp.int32, (rows, rows), 0)
        c_idx = lax.broadcasted_iota(jnp.int32, (rows, rows), 1)
        diff = r_idx - c_idx
        st["mask"] = (diff >= 0) & (diff <= (r_idx & (chunk - 1)))
        st["v_f32"] = v_ref[...].astype(F32) if small else None
        if carry:
            st["states"] = [s_ref[h] for h in heads]

    def decay(h):
        x = _dot(ga_ref[...], wa_ref[:, lanes[h]]) + ba_ref[:, lanes[h]]
        st["b"][h] = _chunk_cumsum(_log_sigmoid(x) * (1.0 / GLA_TAU), chunk)

    def factors(h):
        b = st["b"][h]
        b_last, b_last_full = chunk_row(b, chunk - 1)
        b_last_rows = jnp.concatenate(b_last + [jnp.zeros((LANES - nchunks, GLA_DK), F32)], axis=0)
        st["dec_cols"][h] = jnp.exp(b_last_rows.T)
        q = q_ref[:, lanes[h]].astype(F32) * (GLA_DK ** -0.5)
        k = k_ref[:, lanes[h]].astype(F32)
        qb = q * jnp.exp(b)
        kd = k * jnp.exp(b_last_full - b)
        st["qb"][h], st["kd"][h] = (qb, kd) if small else (None, None)
        st["qb_bf"][h], st["kd_bf"][h] = qb.astype(BF16), kd.astype(BF16)
        if anchored:
            _, b_mid_full = chunk_row(b, chunk // 2 - 1)
            st["qa_bf"][h] = (q * jnp.exp(b - b_mid_full)).astype(BF16)
            st["ka_bf"][h] = (k * jnp.exp(b_mid_full - b)).astype(BF16)
        else:
            st["qa_bf"][h] = st["qb_bf"][h]
            st["ka_bf"][h] = (k * jnp.exp(-b)).astype(BF16)

    def intra(h):
        a = jnp.where(st["mask"], _dot_nt(st["qa_bf"][h], st["ka_bf"][h]), 0.0)
        st["o_intra"][h] = _dot(a.astype(BF16), v_ref[:, vl[h]])

    def step(j):
        sl = slice(j * chunk, (j + 1) * chunk)
        for h in heads:
            state = st["states"][h] if carry else s0_ref[j, h]
            qb_j = chunk_rows(st["qb"][h], st["qb_bf"][h], sl)
            st["o_inter"][h].append(_dot(qb_j, state.astype(BF16))[:chunk])
            if small:
                v_j = _pad_rows(st["v_f32"][sl, vl[h]], mm_rows).astype(BF16)
            else:
                v_j = v_ref[sl, vl[h]]
            kd_j = chunk_rows(st["kd"][h], st["kd_bf"][h], sl)
            new_state = st["dec_cols"][h][:, j:j + 1] * state + _dot_tn(kd_j, v_j)
            if carry:
                st["states"][h] = new_state
            else:
                s_ref[j, h] = new_state

    def epilogue(h):
        if carry:
            s_ref[h] = st["states"][h]
        o = st["o_intra"][h] + jnp.concatenate(st["o_inter"][h], axis=0)
        gate = _silu(gz_ref[:, vl[h]].astype(F32))
        o_ref[:, vl[h]] = (_rmsnorm_rows(o, gout_ref[...]) * gate).astype(BF16)

    cost = GLA_STEP_COST
    steps = [(prepare, cost["prepare"])]
    steps += [(functools.partial(decay, h), cost["decay"]) for h in heads]
    steps += [(functools.partial(factors, h), cost["factors"]) for h in heads]
    steps += [(functools.partial(intra, h), cost["intra"]) for h in heads]
    steps += [(functools.partial(step, j), cost["step"]) for j in range(nchunks)]
    steps += [(functools.partial(epilogue, h), cost["epilogue"]) for h in heads]
    return steps


def _gla_sample_kernel(q_ref, k_ref, v_ref, gz_ref, ga_ref, wa_ref, ba_ref, gout_ref, s0_ref, *rest,
                       seq, group):
    o_ref, s_ref = rest[-2:]
    for thunk, _ in _gla_thunks(q_ref, k_ref, v_ref, gz_ref, ga_ref, wa_ref, ba_ref, gout_ref,
                                s0_ref, o_ref, s_ref, chunk=seq, nchunks=group, carry=False,
                                anchored=False):
        thunk()


def _swa_head_of_block(kv_head, blk):
    pairs = SWA_G // 2
    within = 2 * blk if blk < pairs else 2 * (blk - pairs) + 1
    return kv_head * SWA_G + within


def _swa_operands(kk, vv):
    lo = lax.broadcasted_iota(jnp.int32, kk.shape, 1) < SWA_HD
    kk_sw = pltpu.roll(kk, SWA_HD, 1)
    vv_sw = pltpu.roll(vv, SWA_HD, 1)
    ops = []
    for kh in range(SWA_KVH):
        k_this, k_other = (kk, kk_sw) if kh == 0 else (kk_sw, kk)
        v_this, v_other = (vv, vv_sw) if kh == 0 else (vv_sw, vv)
        ops.append((jnp.where(lo, k_this, k_other).astype(BF16),
                    jnp.where(lo, v_this, 0.0).astype(BF16),
                    jnp.where(lo, 0.0, v_other).astype(BF16)))
    return ops


def _swa_operands_t(kk_t, vv_t):
    zeros = jnp.zeros((SWA_HD, kk_t.shape[1]), F32)
    ops = []
    for kh in range(SWA_KVH):
        k_h = kk_t[kh * SWA_HD:(kh + 1) * SWA_HD]
        v_h = vv_t[kh * SWA_HD:(kh + 1) * SWA_HD]
        ops.append((jnp.concatenate([k_h, k_h], axis=0).astype(BF16),
                    jnp.concatenate([v_h, zeros], axis=0).astype(BF16),
                    jnp.concatenate([zeros, v_h], axis=0).astype(BF16)))
    return ops


def _swa_scores(q, kh, k_dup, keys_on_lanes=False):
    nq = q.shape[0]
    lo = lax.broadcasted_iota(jnp.int32, (nq, LANES), 1) < SWA_HD
    even, odd = [], []
    for p in range(SWA_G // 2):
        c0 = kh * SWA_G * SWA_HD + p * LANES
        q_pair = q[:, c0:c0 + LANES]
        even.append(jnp.where(lo, q_pair, jnp.zeros_like(q_pair)))
        odd.append(jnp.where(lo, jnp.zeros_like(q_pair), q_pair))
    lhs = jnp.concatenate(even + odd, axis=0).astype(BF16)
    s = _dot(lhs, k_dup) if keys_on_lanes else _dot_nt(lhs, k_dup)
    return s * (SWA_HD ** -0.5)


def _swa_soft_head(s_h, valid, sink):
    s_h = jnp.where(valid, s_h, -jnp.inf)
    m = jnp.maximum(jnp.max(s_h, axis=-1, keepdims=True), sink)
    p_h = jnp.exp(s_h - m)
    den = jnp.sum(p_h, axis=-1, keepdims=True) + jnp.exp(sink - m)
    if p_h.shape[0] % BF16_ROWS == 0:
        p_h = p_h.astype(BF16)
    return p_h, 1.0 / den


def _swa_out(probs, rinv, v_even, v_odd):
    pairs = SWA_G // 2
    nq = probs[0].shape[0]
    lo = lax.broadcasted_iota(jnp.int32, (nq, LANES), 1) < SWA_HD
    p_even = jnp.concatenate(probs[:pairs], axis=0).astype(BF16)
    p_odd = jnp.concatenate(probs[pairs:], axis=0).astype(BF16)
    o = _dot(p_even, v_even) + _dot(p_odd, v_odd)
    outs = [o[p * nq:(p + 1) * nq] * jnp.where(lo, rinv[p], rinv[pairs + p]) for p in range(pairs)]
    return jnp.concatenate(outs, axis=1)


def _swa_sample_kernel(sink_ref, q_ref, sz_ref, kvn_ref, ck_ref, cv_ref, *rest, group, seq):
    o_ref, nk_ref, nv_ref = rest[-3:]
    wb = ck_ref.shape[2]
    nk = 2 * wb
    pairs = SWA_G // 2
    blk_rows = SWA_G * seq
    first = lax.broadcasted_iota(jnp.int32, (SWA_KV, wb), 1) < seq
    q = q_ref[...].astype(F32)
    kvn = _pad_rows(kvn_ref[...], LANES)
    kn_all, vn_all = kvn[:, :SWA_KV].T, kvn[:, SWA_KV:].T

    scores, values = [], []
    for e in range(group):
        rows = slice(e * seq, (e + 1) * seq)
        k_old, v_old = ck_ref[e], cv_ref[e]
        shift = (LANES - e * seq) % LANES
        k_new = pltpu.roll(kn_all, shift, 1) if shift else kn_all
        v_new = pltpu.roll(vn_all, shift, 1) if shift else vn_all
        nk_ref[e] = pltpu.roll(jnp.where(first, k_new, k_old), wb - seq, 1)
        nv_ref[e] = pltpu.roll(jnp.where(first, v_new, v_old), wb - seq, 1)
        ops = _swa_operands_t(jnp.concatenate([k_old, k_new], axis=1),
                              jnp.concatenate([v_old, v_new], axis=1))
        for kh in range(SWA_KVH):
            k_dup, v_even, v_odd = ops[kh]
            scores.append(_swa_scores(q[rows], kh, k_dup, keys_on_lanes=True))
            values.append((v_even, v_odd))
    s_all = jnp.concatenate(scores, axis=0)
    i = lax.broadcasted_iota(jnp.int32, s_all.shape, 0) & (seq - 1)
    j = lax.broadcasted_iota(jnp.int32, s_all.shape, 1)
    diff = i + wb - j
    sink_kh = [jnp.concatenate([jnp.full((seq, 1), sink_ref[_swa_head_of_block(kh, blk)], F32)
                                for blk in range(SWA_G)], axis=0) for kh in range(SWA_KVH)]
    p_all, rinv_all = _swa_soft_head(s_all, (diff >= 0) & (diff <= WINDOW),
                                     jnp.concatenate(sink_kh * group, axis=0))

    lo = lax.broadcasted_iota(jnp.int32, (seq, LANES), 1) < SWA_HD
    outs = [[] for _ in range(group)]
    for idx, (v_even, v_odd) in enumerate(values):
        base = idx * blk_rows
        half = pairs * seq
        o = (_dot_nt(p_all[base:base + half], v_even)
             + _dot_nt(p_all[base + half:base + blk_rows], v_odd))
        for p in range(pairs):
            r_even = rinv_all[base + p * seq:base + (p + 1) * seq]
            r_odd = rinv_all[base + half + p * seq:base + half + (p + 1) * seq]
            outs[idx // SWA_KVH].append(o[p * seq:(p + 1) * seq] * jnp.where(lo, r_even, r_odd))
    o = jnp.concatenate([jnp.concatenate(row, axis=1) for row in outs], axis=0)
    o_ref[...] = (o * _silu(sz_ref[...].astype(F32))).astype(BF16)


MEM_HALVES = X_HD // LANES
MEM_TILE_ROWS = X_H * MEM_HALVES


def _mem_rows_view(x):
    lead = x.shape[:-3]
    n = len(lead)
    x = x.reshape(*lead, MEM_LEN, X_H, MEM_HALVES, LANES)
    x = x.transpose(*range(n), n, n + 2, n + 1, n + 3)
    return x.reshape(*lead, MEM_LEN * MEM_TILE_ROWS, LANES)


def _mem_from_rows_view(x):
    lead = x.shape[:-2]
    n = len(lead)
    x = x.reshape(*lead, MEM_LEN, MEM_HALVES, X_H, LANES)
    x = x.transpose(*range(n), n, n + 2, n + 1, n + 3)
    return x.reshape(*lead, MEM_LEN, X_H, X_HD)


def _mem_head_rows(head, half):
    return pl.ds(half * X_H + head, MEM_LEN, stride=MEM_TILE_ROWS)


def _mem_kv_kernel(x_ref, g_ref, w_ref, *rest):
    kb_ref, vb_ref, kn_ref, vn_ref = rest[-4:]
    h = _rmsnorm_rows(x_ref[...], g_ref[...]).astype(BF16)
    for t, (b_ref, n_ref) in enumerate(((kb_ref, kn_ref), (vb_ref, vn_ref))):
        for head in range(X_H):
            c0 = head * X_HD
            res = _dot(h, w_ref[:, t * X_W + c0:t * X_W + c0 + X_HD])
            b_ref[:, c0:c0 + X_HD] = res.astype(BF16)
            for half in range(MEM_HALVES):
                n_ref[_mem_head_rows(head, half), :] = res[:, half * LANES:(half + 1) * LANES]


def _mem_kv(mem2d, g, w, prev, layer, depth, batch):
    d = mem2d.shape[1]
    rows_spec = pl.BlockSpec((None, None, MEM_LEN * MEM_TILE_ROWS, LANES), lambda i: (layer, i, 0, 0))
    rows_shape = jax.ShapeDtypeStruct((depth, batch, MEM_LEN * MEM_TILE_ROWS, LANES), F32)
    dense_spec = pl.BlockSpec((MEM_LEN, X_W), lambda i: (i, 0))
    dense_shape = jax.ShapeDtypeStruct((batch * MEM_LEN, X_W), BF16)
    in_specs = [pl.BlockSpec((MEM_LEN, d), lambda i: (i, 0)),
                pl.BlockSpec((1, d), lambda i: (0, 0)),
                _layer_weight_spec(w, layer)]
    args = [mem2d, g, w]
    aliases = {}
    if prev is not None:
        aliases = {len(args): 2, len(args) + 1: 3}
        in_specs += [pl.BlockSpec(memory_space=pl.ANY)] * 2
        args += list(prev)
    return pl.pallas_call(
        _mem_kv_kernel,
        grid=(batch,),
        in_specs=in_specs,
        out_specs=[dense_spec, dense_spec, rows_spec, rows_spec],
        out_shape=[dense_shape, dense_shape, rows_shape, rows_shape],
        input_output_aliases=aliases,
        compiler_params=_params("parallel"),
        name="mem_kv",
    )(*args)


def _mem_softmax(s):
    p = jnp.exp(s - jnp.max(s, axis=-1, keepdims=True))
    return p, 1.0 / jnp.sum(p, axis=-1, keepdims=True)


def _mem_sample_kernel(q_ref, mz_ref, mk_ref, mv_ref, o_ref, *, group, seq):
    def head_tile(ref, e, h):
        halves = [ref[e, _mem_head_rows(h, half), :] for half in range(MEM_HALVES)]
        return jnp.concatenate(halves, axis=1).astype(BF16)

    q = q_ref[...].astype(F32)
    pairs = [(e, h) for e in range(group) for h in range(X_H)]
    scores = []
    for e, h in pairs:
        q_eh = _pad_rows(q[e * seq:(e + 1) * seq, h * X_HD:(h + 1) * X_HD], BF16_ROWS).astype(BF16)
        scores.append(_dot_nt(q_eh, head_tile(mk_ref, e, h))[:seq])
    p, rinv = _mem_softmax(jnp.concatenate(scores, axis=0) * (X_HD ** -0.5))
    outs = [[] for _ in range(group)]
    for i, (e, h) in enumerate(pairs):
        p_eh = _pad_rows(p[i * seq:(i + 1) * seq], BF16_ROWS).astype(BF16)
        outs[e].append(_dot(p_eh, head_tile(mv_ref, e, h))[:seq] * rinv[i * seq:(i + 1) * seq])
    o = jnp.concatenate([jnp.concatenate(row, axis=1) for row in outs], axis=0)
    o_ref[...] = (o * _silu(mz_ref[...].astype(F32))).astype(BF16)


SAMPLE_GROUP = 4


def _sample_mixers_kernel(sink_ref, gq_ref, gk_ref, gv_ref, gz_ref, ga_ref, sq_ref, sz_ref, mq_ref,
                          mz_ref, kvn_ref, wa_ref, ba_ref, gout_ref, s0_ref, ck_ref, cv_ref, mk_ref,
                          mv_ref, *rest, seq):
    ogla_ref, oswa_ref, omem_ref, s_ref, nk_ref, nv_ref = rest[-6:]
    _mem_sample_kernel(mq_ref, mz_ref, mk_ref, mv_ref, omem_ref, group=SAMPLE_GROUP, seq=seq)
    _swa_sample_kernel(sink_ref, sq_ref, sz_ref, kvn_ref, ck_ref, cv_ref, oswa_ref, nk_ref, nv_ref,
                       group=SAMPLE_GROUP, seq=seq)
    _gla_sample_kernel(gq_ref, gk_ref, gv_ref, gz_ref, ga_ref, wa_ref, ba_ref, gout_ref, s0_ref,
                       ogla_ref, s_ref, seq=seq, group=SAMPLE_GROUP)


def _sample_mixers(z, kv, sink, wa, ba, gout, s0, ck, cv, mk, mv, prev, layer, batch, seq):
    group = SAMPLE_GROUP
    rows = group * seq
    wb = ck.shape[3]
    assert wb == LANES, "the sliding window is one lane tile wide"
    zspec = lambda width, col0: pl.BlockSpec((rows, width), lambda i: (i, col0 // width))
    const = lambda shape: pl.BlockSpec(shape, lambda i: (0,) * len(shape))
    layered = lambda *dims: pl.BlockSpec((None, group) + dims, lambda i: (layer, i) + (0,) * len(dims))
    state_spec = layered(GLA_H, GLA_DK, GLA_DV)
    cache_spec = layered(SWA_KV, wb)
    mem_spec = layered(MEM_LEN * MEM_TILE_ROWS, LANES)
    in_specs = [pl.BlockSpec(memory_space=pltpu.SMEM),
                zspec(GLA_QK, Z_GQ), zspec(GLA_QK, Z_GK), zspec(GLA_V, Z_GV), zspec(GLA_V, Z_GZ),
                zspec(LANES, Z_GA), zspec(SWA_W, Z_SQ), zspec(SWA_W, Z_SZ), zspec(X_W, Z_MQ),
                zspec(X_W, Z_MZ), pl.BlockSpec((rows, KV_COLS), lambda i: (i, 0)),
                _layer_weight_spec(wa, layer), const((1, GLA_QK)), const((1, GLA_DV)),
                state_spec, cache_spec, cache_spec, mem_spec, mem_spec]
    args = [sink] + [z] * 9 + [kv, wa, ba, gout, s0, ck, cv, mk, mv]
    aliases = {}
    if prev is not None:
        aliases = {len(args) + k: 3 + k for k in range(3)}
        in_specs += [pl.BlockSpec(memory_space=pl.ANY)] * 3
        args += list(prev)
    out_row = lambda width: pl.BlockSpec((rows, width), lambda i: (i, 0))
    return pl.pallas_call(
        functools.partial(_sample_mixers_kernel, seq=seq),
        grid=(batch // group,),
        in_specs=in_specs,
        out_specs=[out_row(GLA_V), out_row(SWA_W), out_row(X_W), state_spec, cache_spec, cache_spec],
        out_shape=[jax.ShapeDtypeStruct((batch * seq, GLA_V), BF16),
                   jax.ShapeDtypeStruct((batch * seq, SWA_W), BF16),
                   jax.ShapeDtypeStruct((batch * seq, X_W), BF16),
                   jax.ShapeDtypeStruct(s0.shape, F32), jax.ShapeDtypeStruct(ck.shape, F32),
                   jax.ShapeDtypeStruct(cv.shape, F32)],
        input_output_aliases=aliases,
        compiler_params=_params("parallel"),
        name="sample_mixers",
    )(*args)


FRONT_ROWS = 256
FRONT_GLA_CHUNK = 2 * GLA_CHUNK

def _front_kernel(sink_ref, x_ref, g_ref, w_ref, wa_ref, ba_ref, gout_ref, mk_ref, mv_ref, *rest):
    (ogla_ref, oswa_ref, omem_ref, mg_ref, kvl_ref, s_ref,
     gq_scr, gk_scr, gv_scr, gz_scr, ga_scr, sq_scr, sz_scr, mq_scr, mz_scr, kv_scr, kvp_scr) = rest[-17:]
    n = pl.program_id(1)
    tm = FRONT_ROWS

    @pl.when(n == 0)
    def _():
        s_ref[...] = jnp.zeros_like(s_ref)
        kvp_scr[...] = jnp.zeros_like(kvp_scr)

    h = _rmsnorm_rows(x_ref[...], g_ref[...]).astype(BF16)

    def project(chunk):
        dst, dst_c0, w_r0, width = chunk
        if dst is ga_scr:
            dst[...] = _project_decay(h, w_ref).astype(dst.dtype)
        else:
            dst[:, dst_c0:dst_c0 + width] = _project(h, w_ref, w_r0, width).astype(dst.dtype)

    def chunks_of(dst, w_r0, width):
        return [(dst, c, w_r0 + c, min(MXU_COLS, width - c)) for c in range(0, width, MXU_COLS)]

    swa_cols = chunks_of(kv_scr, W_SK, KV_COLS) + chunks_of(sq_scr, W_SQ, SWA_W) \
        + chunks_of(sz_scr, W_SZ, SWA_W)
    gla_cols = (chunks_of(ga_scr, W_GA, LANES) + chunks_of(gq_scr, W_GQ, GLA_QK)
                + chunks_of(gk_scr, W_GK, GLA_QK) + chunks_of(gv_scr, W_GV, GLA_V)
                + chunks_of(gz_scr, W_GZ, GLA_V))
    mem_cols = chunks_of(mq_scr, W_MQ, X_W) + chunks_of(mz_scr, W_MZ, X_W)
    merge_cols = chunks_of(mg_ref, W_MG, N_BRANCH * D_MODEL)

    i = lax.broadcasted_iota(jnp.int32, (WINDOW, 2 * WINDOW), 0)
    j = lax.broadcasted_iota(jnp.int32, (WINDOW, 2 * WINDOW), 1)
    diff = i + WINDOW - j
    band = (diff >= 0) & (diff <= WINDOW)
    swa = {}

    def swa_start(blk):
        prev = kvp_scr[...] if blk == 0 else kv_scr[(blk - 1) * WINDOW:blk * WINDOW, :]
        kvb = jnp.concatenate([prev, kv_scr[blk * WINDOW:(blk + 1) * WINDOW, :]], axis=0)
        swa["ops"] = _swa_operands(kvb[:, :SWA_KV], kvb[:, SWA_KV:])
        swa["valid"] = band & (j >= jnp.where(n > 0, 0, WINDOW)) if blk == 0 else band

    def swa_scores(blk, kh):
        swa["s"] = _swa_scores(sq_scr[blk * WINDOW:(blk + 1) * WINDOW, :], kh, swa["ops"][kh][0])
        swa["soft"] = []

    def swa_soft(kh, hb):
        swa["soft"].append(_swa_soft_head(swa["s"][hb * WINDOW:(hb + 1) * WINDOW], swa["valid"],
                                          sink_ref[_swa_head_of_block(kh, hb)]))

    def swa_out(blk, kh):
        _, v_even, v_odd = swa["ops"][kh]
        o = _swa_out([p for p, _ in swa["soft"]], [r for _, r in swa["soft"]], v_even, v_odd)
        rows = slice(blk * WINDOW, (blk + 1) * WINDOW)
        cols = slice(kh * SWA_G * SWA_HD, (kh + 1) * SWA_G * SWA_HD)
        oswa_ref[rows, cols] = (o * _silu(sz_scr[rows, cols].astype(F32))).astype(BF16)

    swa_steps = []
    for blk in range(tm // WINDOW):
        swa_steps.append((functools.partial(swa_start, blk), SWA_STEP_COST["start"]))
        for kh in range(SWA_KVH):
            swa_steps.append((functools.partial(swa_scores, blk, kh), SWA_STEP_COST["scores"]))
            swa_steps += [(functools.partial(swa_soft, kh, hb), SWA_STEP_COST["soft"])
                          for hb in range(SWA_G)]
            swa_steps.append((functools.partial(swa_out, blk, kh), SWA_STEP_COST["out"]))

    mem = {}

    def mem_scores(hd):
        cols = slice(hd * X_HD, (hd + 1) * X_HD)
        p, rinv = _mem_softmax(_dot_nt(mq_scr[:, cols], mk_ref[:, cols]) * (X_HD ** -0.5))
        mem["p"], mem["rinv"] = p.astype(BF16), rinv

    def mem_out(hd):
        cols = slice(hd * X_HD, (hd + 1) * X_HD)
        o = _dot(mem["p"], mv_ref[:, cols]) * mem["rinv"]
        omem_ref[:, cols] = (o * _silu(mz_scr[:, cols].astype(F32))).astype(BF16)

    mem_steps = []
    for hd in range(X_H):
        mem_steps += [(functools.partial(mem_scores, hd), MEM_STEP_COST["scores"]),
                      (functools.partial(mem_out, hd), MEM_STEP_COST["out"])]

    gla_steps = _gla_thunks(gq_scr, gk_scr, gv_scr, gz_scr, ga_scr, wa_ref, ba_ref, gout_ref, None,
                            ogla_ref, s_ref, chunk=FRONT_GLA_CHUNK, nchunks=tm // FRONT_GLA_CHUNK,
                            carry=True, anchored=True)

    gla_early = 5
    gla_head = 1 + 2 * GLA_H
    for chunk in gla_cols[:gla_early]:
        project(chunk)
    chunks = swa_cols + gla_cols[gla_early:] + mem_cols + merge_cols
    n_swa, n_gla = len(swa_cols), len(swa_cols) + len(gla_cols) - gla_early
    steps = [(t, w, 0) for t, w in gla_steps[:gla_head]]
    steps += [(t, w, n_swa) for t, w in swa_steps]
    steps += [(t, w, n_gla) for t, w in gla_steps[gla_head:]]
    steps += [(t, w, n_gla + len(mem_cols)) for t, w in mem_steps]
    _interleave(steps, chunks, project)

    kv_tail = kv_scr[tm - WINDOW:, :]
    kvp_scr[...] = kv_tail
    kvl_ref[...] = kv_tail.T


def _front(x2d, g, w, sink, wa, ba, gout, mk, mv, prev_state, layer, depth, batch, seq):
    t, d = x2d.shape
    tm = FRONT_ROWS
    nblk = seq // tm
    row = lambda width: pl.BlockSpec((tm, width), lambda b, n: (b * nblk + n, 0))
    const = lambda shape: pl.BlockSpec(shape, lambda b, n: (0,) * len(shape))
    mem_spec = pl.BlockSpec((MEM_LEN, X_W), lambda b, n: (b, 0))
    state_spec = pl.BlockSpec((None, None, GLA_H, GLA_DK, GLA_DV), lambda b, n: (layer, b, 0, 0, 0))
    in_specs = [pl.BlockSpec(memory_space=pltpu.SMEM), row(d), const((1, d)),
                _layer_weight_spec(w, layer, resident=True),
                _layer_weight_spec(wa, layer), const((1, GLA_QK)), const((1, GLA_DV)),
                mem_spec, mem_spec]
    args = [sink, x2d, g, w, wa, ba, gout, mk, mv]
    aliases = {}
    if prev_state is not None:
        aliases = {len(args): 5}
        in_specs.append(pl.BlockSpec(memory_space=pl.ANY))
        args.append(prev_state)
    scratch = [pltpu.VMEM((tm, width), BF16) for width in
               (GLA_QK, GLA_QK, GLA_V, GLA_V, LANES, SWA_W, SWA_W, X_W, X_W)]
    scratch += [pltpu.VMEM((tm, KV_COLS), F32), pltpu.VMEM((WINDOW, KV_COLS), F32)]
    return pl.pallas_call(
        _front_kernel,
        grid=(batch, nblk),
        in_specs=in_specs,
        out_specs=[row(GLA_V), row(SWA_W), row(X_W), row(N_BRANCH * D_MODEL),
                   pl.BlockSpec((None, KV_COLS, WINDOW), lambda b, n: (b, 0, 0)), state_spec],
        out_shape=[jax.ShapeDtypeStruct((t, GLA_V), BF16), jax.ShapeDtypeStruct((t, SWA_W), BF16),
                   jax.ShapeDtypeStruct((t, X_W), BF16),
                   jax.ShapeDtypeStruct((t, N_BRANCH * D_MODEL), BF16),
                   jax.ShapeDtypeStruct((batch, KV_COLS, WINDOW), F32),
                   jax.ShapeDtypeStruct((depth, batch, GLA_H, GLA_DK, GLA_DV), F32)],
        scratch_shapes=scratch,
        input_output_aliases=aliases,
        compiler_params=_params("parallel", "arbitrary"),
        name="front",
    )(*args)


def _merge_kernel(x_ref, a_ref, s_ref, m_ref, ga_ref, gs_ref, gm_ref, wa_ref, ws_ref, wm_ref,
                  wo_ref, *rest, final):
    if final:
        gf_ref, y_ref = rest
    else:
        (y_ref,) = rest
    merged = (_sigmoid(ga_ref[...].astype(F32)) * _dot(a_ref[...], wa_ref[...])
              + _sigmoid(gs_ref[...].astype(F32)) * _dot(s_ref[...], ws_ref[...])
              + _sigmoid(gm_ref[...].astype(F32)) * _dot(m_ref[...], wm_ref[...]))
    y = x_ref[...] + _dot(merged.astype(BF16), wo_ref[...])
    if final:
        y = _rmsnorm_rows(y, gf_ref[...])
    y_ref[...] = y


def _merge(x2d, o_gla, o_swa, o_mem, gates, gate_col0, w_gla, w_swa, w_mem, w_out, g_final, layer):
    t, d = x2d.shape
    tm = min(t, 2 * _row_tile(t))
    final = g_final is not None
    row = lambda width: pl.BlockSpec((tm, width), lambda i: (i, 0))
    gate = lambda k: pl.BlockSpec((tm, d), lambda i: (i, gate_col0 // d + k))
    in_specs = [row(d), row(d), row(d), row(d), gate(0), gate(1), gate(2)]
    in_specs += [_layer_weight_spec(w, layer, resident=True) for w in (w_gla, w_swa, w_mem, w_out)]
    args = [x2d, o_gla, o_swa, o_mem, gates, gates, gates, w_gla, w_swa, w_mem, w_out]
    if final:
        in_specs.append(pl.BlockSpec((1, d), lambda i: (0, 0)))
        args.append(g_final)
    return pl.pallas_call(
        functools.partial(_merge_kernel, final=final),
        grid=(t // tm,),
        in_specs=in_specs,
        out_specs=row(d),
        out_shape=jax.ShapeDtypeStruct((t, d), F32),
        compiler_params=_params("parallel"),
        name="merge_final" if final else "merge",
    )(*args)


def kernel(x_prompt, x_sample, mem_prompt, state_gla, cache_swa_k, cache_swa_v, cache_mem_k,
           cache_mem_v, g_norm, w_in, w_gla_a, b_gla_a, g_gla_out, swa_sink, g_mem, w_mem_kv,
           w_br_gla, w_br_swa, w_br_mem, w_out, g_final):
    depth = w_in.shape[0]
    batch, seq, d = x_prompt.shape
    dec_batch, dec_seq, _ = x_sample.shape
    wb = cache_swa_k.shape[2]

    w_in_r = _prep_w_in(w_in)
    w_a = jnp.pad(w_gla_a, ((0, 0), (0, LANES - GLA_RANK), (0, 0))).astype(BF16)
    w_mem_b = w_mem_kv.astype(BF16)
    w_g, w_s, w_m, w_o = (w.astype(BF16) for w in (w_br_gla, w_br_swa, w_br_mem, w_out))
    g_fin = g_final.reshape(1, d)

    yp = x_prompt.reshape(batch * seq, d)
    ys = x_sample.reshape(dec_batch * dec_seq, d)
    mem2d = mem_prompt.reshape(batch * MEM_LEN, d)
    swa_t = lambda c: c.transpose(0, 1, 3, 4, 2).reshape(depth, dec_batch, SWA_KV, wb)
    ck_all, cv_all = swa_t(cache_swa_k), swa_t(cache_swa_v)
    cmk_all = _mem_rows_view(cache_mem_k)
    cmv_all = _mem_rows_view(cache_mem_v)
    gla_p = sample_new = mem_new = None
    kp_l, vp_l = [], []
    for l in range(depth):
        last = l == depth - 1
        gn = g_norm[l].reshape(1, d)
        ba = b_gla_a[l].reshape(1, GLA_QK)
        gout = g_gla_out[l].reshape(1, GLA_DV)
        sink = swa_sink[l]
        branch_w = (w_g, w_s, w_m, w_o, g_fin if last else None, l)

        mk, mv, *mem_new = _mem_kv(mem2d, g_mem[l].reshape(1, d), w_mem_b, mem_new, l, depth,
                                   batch)
        o_gla, o_swa, o_mem, gates, kv_tail, gla_p = _front(
            yp, gn, w_in_r, sink, w_a, ba, gout, mk, mv, gla_p, l, depth, batch, seq)
        yp = _merge(yp, o_gla, o_swa, o_mem, gates, 0, *branch_w)
        kv_heads = kv_tail.reshape(batch, 2, SWA_KVH, SWA_HD, WINDOW).transpose(0, 1, 4, 2, 3)
        kp_l.append(kv_heads[:, 0])
        vp_l.append(kv_heads[:, 1])

        z, kv = _in_proj(ys, gn, w_in_r, l)
        o_gla, o_swa, o_mem, *sample_new = _sample_mixers(
            z, kv, sink, w_a, ba, gout, state_gla, ck_all, cv_all, cmk_all, cmv_all, sample_new, l,
            dec_batch, dec_seq)
        ys = _merge(ys, o_gla, o_swa, o_mem, z, Z_MG, *branch_w)

    gla_s = sample_new[0]
    k_s, v_s = (c.reshape(depth, dec_batch, SWA_KVH, SWA_HD, wb).transpose(0, 1, 4, 2, 3)
                for c in sample_new[1:])
    mk_p, mv_p = (_mem_from_rows_view(c) for c in mem_new)
    return (yp.reshape(batch, seq, d), ys.reshape(dec_batch, dec_seq, d), gla_p,
            jnp.stack(kp_l), jnp.stack(vp_l), mk_p, mv_p, gla_s, k_s, v_s)
```
